```python
import math
import jax, jax.numpy as jnp
from jax import lax
import numpy as np

D_MODEL = 1024
BATCH = 2
SEQ = 16384
DEPTH = 1
DEC_BATCH = 32
DEC_SEQ = 2048
PAST_LEN = 128

N_HEADS = 8
HEAD_DIM = 64
ATT_V_DIM = 2 * HEAD_DIM
ATT_WIDTH = N_HEADS * ATT_V_DIM
ROT_DIM = HEAD_DIM // 4
ROPE_THETA = 500000.0
Q_BLOCK = 128
SSM_CH = 16
SSM_GROUPS = 32
SSM_WIDTH = SSM_GROUPS * SSM_CH
SSM_STATE = 64
DT_MIN = 0.001
DT_MAX = 0.1
N_GROUPS = 4
EXPERTS_PER_GROUP = 8
N_EXPERTS = N_GROUPS * EXPERTS_PER_GROUP
TOP_K_INNER = 2
EXPERT_FF = 256
PLE_DIM = 256
EPS = 1e-6
IN_COLS = 3 * ATT_WIDTH + SSM_WIDTH + 2 * D_MODEL

kernel_name = 'hybrid_diffattn_s5_hmoe_encoder'


def rmsnorm(x, g):
    xf = x.astype(jnp.float32)
    y = xf * lax.rsqrt(jnp.mean(xf * xf, axis=-1, keepdims=True) + EPS)
    return (y * g.astype(jnp.float32)).astype(x.dtype)


def rotary_partial(x, pos):
    half = ROT_DIM // 2
    inv = ROPE_THETA ** (-jnp.arange(0, ROT_DIM, 2, dtype=jnp.float32) / ROT_DIM)
    ang = pos[:, None] * inv[None, :]
    cos, sin = jnp.cos(ang), jnp.sin(ang)
    xr = x[..., :ROT_DIM].astype(jnp.float32)
    x1, x2 = xr[..., :half], xr[..., half:]
    rot = jnp.concatenate([x1 * cos - x2 * sin, x2 * cos + x1 * sin], axis=-1)
    return jnp.concatenate([rot.astype(x.dtype), x[..., ROT_DIM:]], axis=-1)


def diff_attention(q, k, v, lam):
    Bsz, H, _, L, d = q.shape
    nb = L // Q_BLOCK
    qb = jnp.moveaxis(q.reshape(Bsz, H, 2, nb, Q_BLOCK, d), 3, 0)
    scale = d ** -0.5

    def block(qi):
        s = jnp.einsum('bhmqd,bhmkd->bhmqk', qi, k, preferred_element_type=jnp.float32) * scale
        pr = jax.nn.softmax(s, axis=-1)
        a = pr[:, :, 0] - lam * pr[:, :, 1]
        return jnp.einsum('bhqk,bhke->bhqe', a.astype(v.dtype), v)

    o = lax.map(block, qb)
    return jnp.moveaxis(o, 0, 2).reshape(Bsz, H, L, 2 * d)


def _ssm_combine(e1, e2):
    a1r, a1i, b1r, b1i = e1
    a2r, a2i, b2r, b2i = e2
    ar = a2r * a1r - a2i * a1i
    ai = a2r * a1i + a2i * a1r
    br = a2r * b1r - a2i * b1i + b2r
    bi = a2r * b1i + a2i * b1r + b2i
    return (ar, ai, br, bi)


def s5_bidir(u, lam_re, lam_im, log_dt, b_re, b_im, c_re, c_im, d_skip):
    Bsz, L, _ = u.shape
    uf = u.astype(jnp.float32).reshape(Bsz, L, SSM_GROUPS, SSM_CH)
    y = uf * d_skip.astype(jnp.float32)
    for dirn in range(2):
        dt = jnp.exp(log_dt[dirn].astype(jnp.float32))[:, None]
        lr = lam_re[dirn].astype(jnp.float32)
        li = lam_im[dirn].astype(jnp.float32)
        mag = jnp.exp(lr * dt)
        a_re = mag * jnp.cos(li * dt)
        a_im = mag * jnp.sin(li * dt)
        den = lr * lr + li * li
        n_re = a_re - 1.0
        q_re = (n_re * lr + a_im * li) / den
        q_im = (a_im * lr - n_re * li) / den
        br = b_re[dirn].astype(jnp.float32)
        bi = b_im[dirn].astype(jnp.float32)
        bb_re = q_re[..., None] * br - q_im[..., None] * bi
        bb_im = q_re[..., None] * bi + q_im[..., None] * br
        bu_re = jnp.einsum('blgc,gpc->blgp', uf, bb_re)
        bu_im = jnp.einsum('blgc,gpc->blgp', uf, bb_im)
        a_re_l = jnp.broadcast_to(a_re[None, None], (1, L, SSM_GROUPS, SSM_STATE))
        a_im_l = jnp.broadcast_to(a_im[None, None], (1, L, SSM_GROUPS, SSM_STATE))
        _, _, x_re, x_im = lax.associative_scan(
            _ssm_combine, (a_re_l, a_im_l, bu_re, bu_im), reverse=(dirn == 1), axis=1)
        y = y + jnp.einsum('blgp,gcp->blgc', x_re, c_re[dirn].astype(jnp.float32)) \
              - jnp.einsum('blgp,gcp->blgc', x_im, c_im[dirn].astype(jnp.float32))
    return y.reshape(Bsz, L, SSM_WIDTH)


def hier_moe(u, w_r1, b_r1, w_r2, b_r2, w_e1, w_e3, w_e2):
    Bsz, L, Dm = u.shape
    t = u.reshape(-1, Dm)
    n = t.shape[0]
    lg = (t @ w_r1).astype(jnp.float32) + b_r1.astype(jnp.float32)
    pg = jax.nn.softmax(lg, axis=-1)
    p_sel, g_sel = lax.top_k(pg, 1)
    l2 = ((t @ w_r2).astype(jnp.float32) + b_r2.astype(jnp.float32)).reshape(n, N_GROUPS, EXPERTS_PER_GROUP)
    l2s = jnp.take_along_axis(l2, g_sel[:, :, None], axis=1)[:, 0]
    p2 = jax.nn.softmax(l2s, axis=-1)
    w_top, i_top = lax.top_k(p2, TOP_K_INNER)
    w_top = w_top / jnp.sum(w_top, axis=-1, keepdims=True)
    eidx = g_sel * EXPERTS_PER_GROUP + i_top
    gates = p_sel * jnp.sum(jax.nn.one_hot(eidx, N_EXPERTS, dtype=jnp.float32) * w_top[..., None], axis=1)
    out = jnp.zeros((n, Dm), jnp.float32)
    for e in range(N_EXPERTS):
        hdn = jax.nn.silu(t @ w_e1[e]) * (t @ w_e3[e])
        out = out + gates[:, e:e + 1] * (hdn @ w_e2[e]).astype(jnp.float32)
    return out.reshape(Bsz, L, Dm).astype(u.dtype)


def _trunk(x, p, g_mix, w_in, lam_q1, lam_k1, lam_q2, lam_k2, subln_g,
           ssm_lam_re, ssm_lam_im, ssm_log_dt, ssm_b_re, ssm_b_im, ssm_c_re, ssm_c_im, ssm_d,
           w_glu, b_glu, w_attn_br, w_ssm_br, w_out, g_ffn, w_r1, b_r1, w_r2, b_r2,
           w_e1, w_e3, w_e2, g_ple, w_ple_gate, w_ple, g_final):
    Bsz, L, _ = x.shape
    pos = jnp.arange(L, dtype=jnp.float32)
    h = x
    for i in range(DEPTH):
        u = rmsnorm(h, g_mix[i])
        proj = u @ w_in[i]
        q = proj[..., :ATT_WIDTH].reshape(Bsz, L, N_HEADS, 2, HEAD_DIM).transpose(0, 2, 3, 1, 4)
        k = proj[..., ATT_WIDTH:2 * ATT_WIDTH].reshape(Bsz, L, N_HEADS, 2, HEAD_DIM).transpose(0, 2, 3, 1, 4)
        v = proj[..., 2 * ATT_WIDTH:3 * ATT_WIDTH].reshape(Bsz, L, N_HEADS, ATT_V_DIM).transpose(0, 2, 1, 3)
        s_in = proj[..., 3 * ATT_WIDTH:3 * ATT_WIDTH + SSM_WIDTH]
        gate_logits = proj[..., 3 * ATT_WIDTH + SSM_WIDTH:]
        q = rotary_partial(q, pos)
        k = rotary_partial(k, pos)
        lam_init = 0.8 - 0.6 * math.exp(-0.3 * i)
        lam = (jnp.exp(jnp.sum(lam_q1[i].astype(jnp.float32) * lam_k1[i].astype(jnp.float32)))
               - jnp.exp(jnp.sum(lam_q2[i].astype(jnp.float32) * lam_k2[i].astype(jnp.float32))) + lam_init)
        o = diff_attention(q, k, v, lam)
        o = rmsnorm(o, subln_g[i]) * (1.0 - lam_init)
        o = o.transpose(0, 2, 1, 3).reshape(Bsz, L, ATT_WIDTH)
        ys = s5_bidir(s_in, ssm_lam_re[i], ssm_lam_im[i], ssm_log_dt[i], ssm_b_re[i], ssm_b_im[i],
                      ssm_c_re[i], ssm_c_im[i], ssm_d[i])
        yg = jax.nn.gelu(ys)
        ys = (yg * jax.nn.sigmoid(yg @ w_glu[i].astype(jnp.float32) + b_glu[i].astype(jnp.float32))).astype(x.dtype)
        br_a = o @ w_attn_br[i]
        br_s = ys @ w_ssm_br[i]
        g_a = jax.nn.sigmoid(gate_logits[..., :D_MODEL])
        g_s = jax.nn.sigmoid(gate_logits[..., D_MODEL:])
        h = h + ((g_a * br_a + g_s * br_s) @ w_out[i]).astype(h.dtype)
        h = h + hier_moe(rmsnorm(h, g_ffn[i]), w_r1[i], b_r1[i], w_r2[i], b_r2[i], w_e1[i], w_e3[i], w_e2[i])
        ple_gate = jax.nn.sigmoid(rmsnorm(h, g_ple[i]) @ w_ple_gate[i])
        h = h + (ple_gate * (p[i] @ w_ple[i])).astype(h.dtype)
    return rmsnorm(h, g_final)


def setup_inputs(seed: int = 0) -> dict:
    key = jax.random.key(seed)
    ks = jax.random.split(key, 40)
    f32 = jnp.float32
    nrm = lambda k, shape, s: jax.random.normal(k, shape, f32) * s
    P, G, CH = SSM_STATE, SSM_GROUPS, SSM_CH
    lam_im_base = jnp.pi * jnp.arange(P, dtype=f32)
    return {
        'x_prompt': nrm(ks[0], (BATCH, SEQ, D_MODEL), 1.0),
        'x_sample': nrm(ks[1], (DEC_BATCH, DEC_SEQ, D_MODEL), 1.0),
        'p_prompt': nrm(ks[2], (DEPTH, BATCH, SEQ, PLE_DIM), 1.0),
        'p_sample': nrm(ks[3], (DEPTH, DEC_BATCH, DEC_SEQ, PLE_DIM), 1.0),
        'g_mix': 1.0 + nrm(ks[4], (DEPTH, D_MODEL), 0.01),
        'w_in': nrm(ks[5], (DEPTH, D_MODEL, IN_COLS), D_MODEL ** -0.5),
        'lam_q1': nrm(ks[6], (DEPTH, HEAD_DIM), 0.1),
        'lam_k1': nrm(ks[7], (DEPTH, HEAD_DIM), 0.1),
        'lam_q2': nrm(ks[8], (DEPTH, HEAD_DIM), 0.1),
        'lam_k2': nrm(ks[9], (DEPTH, HEAD_DIM), 0.1),
        'subln_g': 1.0 + nrm(ks[10], (DEPTH, ATT_V_DIM), 0.01),
        'ssm_lam_re': -0.5 + nrm(ks[11], (DEPTH, 2, G, P), 0.01),
        'ssm_lam_im': lam_im_base + nrm(ks[12], (DEPTH, 2, G, P), 0.01),
        'ssm_log_dt': jax.random.uniform(ks[13], (DEPTH, 2, G), f32, math.log(DT_MIN), math.log(DT_MAX)),
        'ssm_b_re': nrm(ks[14], (DEPTH, 2, G, P, CH), (2.0 * CH) ** -0.5),
        'ssm_b_im': nrm(ks[15], (DEPTH, 2, G, P, CH), (2.0 * CH) ** -0.5),
        'ssm_c_re': nrm(ks[16], (DEPTH, 2, G, CH, P), (2.0 * P) ** -0.5),
        'ssm_c_im': nrm(ks[17], (DEPTH, 2, G, CH, P), (2.0 * P) ** -0.5),
        'ssm_d': nrm(ks[18], (DEPTH, G, CH), 1.0),
        'w_glu': nrm(ks[19], (DEPTH, SSM_WIDTH, SSM_WIDTH), SSM_WIDTH ** -0.5),
        'b_glu': nrm(ks[20], (DEPTH, SSM_WIDTH), 0.01),
        'w_attn_br': nrm(ks[21], (DEPTH, ATT_WIDTH, D_MODEL), ATT_WIDTH ** -0.5),
        'w_ssm_br': nrm(ks[22], (DEPTH, SSM_WIDTH, D_MODEL), SSM_WIDTH ** -0.5),
        'w_out': nrm(ks[23], (DEPTH, D_MODEL, D_MODEL), D_MODEL ** -0.5),
        'g_ffn': 1.0 + nrm(ks[24], (DEPTH, D_MODEL), 0.01),
        'w_r1': nrm(ks[25], (DEPTH, D_MODEL, N_GROUPS), D_MODEL ** -0.5),
        'b_r1': nrm(ks[26], (DEPTH, N_GROUPS), 0.01),
        'w_r2': nrm(ks[27], (DEPTH, D_MODEL, N_EXPERTS), D_MODEL ** -0.5),
        'b_r2': nrm(ks[28], (DEPTH, N_EXPERTS), 0.01),
        'w_e1': nrm(ks[29], (DEPTH, N_EXPERTS, D_MODEL, EXPERT_FF), D_MODEL ** -0.5),
        'w_e3': nrm(ks[30], (DEPTH, N_EXPERTS, D_MODEL, EXPERT_FF), D_MODEL ** -0.5),
        'w_e2': nrm(ks[31], (DEPTH, N_EXPERTS, EXPERT_FF, D_MODEL), EXPERT_FF ** -0.5),
        'g_ple': 1.0 + nrm(ks[32], (DEPTH, D_MODEL), 0.01),
        'w_ple_gate': nrm(ks[33], (DEPTH, D_MODEL, D_MODEL), D_MODEL ** -0.5),
        'w_ple': nrm(ks[34], (DEPTH, PLE_DIM, D_MODEL), PLE_DIM ** -0.5),
        'g_final': 1.0 + nrm(ks[35], (D_MODEL,), 0.01),
    }


def reference(x_prompt, x_sample, p_prompt, p_sample, g_mix, w_in, lam_q1, lam_k1, lam_q2, lam_k2,
              subln_g, ssm_lam_re, ssm_lam_im, ssm_log_dt, ssm_b_re, ssm_b_im, ssm_c_re, ssm_c_im,
              ssm_d, w_glu, b_glu, w_attn_br, w_ssm_br, w_out, g_ffn, w_r1, b_r1, w_r2, b_r2,
              w_e1, w_e3, w_e2, g_ple, w_ple_gate, w_ple, g_final):
    weights = (g_mix, w_in, lam_q1, lam_k1, lam_q2, lam_k2, subln_g,
               ssm_lam_re, ssm_lam_im, ssm_log_dt, ssm_b_re, ssm_b_im, ssm_c_re, ssm_c_im, ssm_d,
               w_glu, b_glu, w_attn_br, w_ssm_br, w_out, g_ffn, w_r1, b_r1, w_r2, b_r2,
               w_e1, w_e3, w_e2, g_ple, w_ple_gate, w_ple, g_final)
    y_prompt = _trunk(x_prompt, p_prompt, *weights)
    y_sample = _trunk(x_sample, p_sample, *weights)
    return (y_prompt, y_sample)
```

```python
import functools
import math

import jax
import jax.numpy as jnp
from jax import lax
from jax.experimental import pallas as pl
from jax.experimental.pallas import tpu as pltpu

F32 = jnp.float32
BF16 = jnp.bfloat16
HIGHEST = lax.Precision.HIGHEST

D_MODEL = 1024
N_HEADS = 8
HEAD_DIM = 64
ATT_V_DIM = 2 * HEAD_DIM
ATT_WIDTH = N_HEADS * ATT_V_DIM
ROT_DIM = HEAD_DIM // 4
ROPE_THETA = 500000.0
SSM_CH = 16
SSM_GROUPS = 32
SSM_WIDTH = SSM_GROUPS * SSM_CH
SSM_STATE = 64
N_GROUPS = 4
EXPERTS_PER_GROUP = 8
N_EXPERTS = N_GROUPS * EXPERTS_PER_GROUP
EXPERT_FF = 256
PLE_DIM = 256
EPS = 1e-6
LAM_INIT = 0.8 - 0.6 * math.exp(-0.3 * 0)

LANES = 128
SUBLANES = 8
SSM_CHUNK = 32
SSM_PAIR_LANES = 2 * SSM_STATE
VMEM_LIMIT = 56 * 1024 * 1024
NEG_BIG = -1e30


def _params(sem):
    return pltpu.CompilerParams(dimension_semantics=sem, vmem_limit_bytes=VMEM_LIMIT)


def _rms(x, g):
    return x * lax.rsqrt(jnp.mean(x * x, axis=-1, keepdims=True) + EPS) * g


def _sigmoid(x):
    return 1.0 / (1.0 + jnp.exp(-x))


def _pick_tile(n, pref):
    t = min(n, pref)
    while n % t:
        t //= 2
    return t


def _in_proj_kernel(x_ref, g_ref, w_ref, cos_ref, sa_ref, sb_ref,
                    q_ref, k_ref, v_ref, s_ref, gl_ref):
    u = _rms(x_ref[...], g_ref[...]).astype(BF16)
    cos = cos_ref[...]
    sa = sa_ref[...]
    sb = sb_ref[...]
    for sec, out_ref, scale in ((0, q_ref, HEAD_DIM ** -0.5), (1, k_ref, 1.0)):
        y = jnp.dot(u, w_ref[:, sec * ATT_WIDTH:(sec + 1) * ATT_WIDTH], preferred_element_type=F32)
        for h in range(N_HEADS):
            yh = y[:, h * LANES:(h + 1) * LANES]
            r = yh * cos + pltpu.roll(yh, LANES - ROT_DIM // 2, 1) * sa + pltpu.roll(yh, ROT_DIM // 2, 1) * sb
            out_ref[:, h * LANES:(h + 1) * LANES] = (r * scale).astype(BF16)
    c0 = 2 * ATT_WIDTH
    v_ref[...] = jnp.dot(u, w_ref[:, c0:c0 + ATT_WIDTH], preferred_element_type=F32).astype(BF16)
    c0 += ATT_WIDTH
    s_ref[...] = jnp.dot(u, w_ref[:, c0:c0 + SSM_WIDTH], preferred_element_type=F32).astype(BF16)
    c0 += SSM_WIDTH
    gl_ref[...] = jnp.dot(u, w_ref[:, c0:c0 + 2 * D_MODEL], preferred_element_type=F32).astype(BF16)


def _rope_tables(seq):
    half = ROT_DIM // 2
    inv = ROPE_THETA ** (-jnp.arange(0, ROT_DIM, 2, dtype=F32) / ROT_DIM)
    ang = jnp.arange(seq, dtype=F32)[:, None] * inv[None, :]
    cos, sin = jnp.cos(ang), jnp.sin(ang)
    one = jnp.ones((seq, HEAD_DIM - ROT_DIM), F32)
    zero = jnp.zeros((seq, HEAD_DIM - ROT_DIM), F32)
    zh = jnp.zeros((seq, half), F32)
    cos_t = jnp.concatenate([cos, cos, one], axis=1)
    sa_t = jnp.concatenate([-sin, zh, zero], axis=1)
    sb_t = jnp.concatenate([zh, sin, zero], axis=1)
    tile2 = lambda t: jnp.concatenate([t, t], axis=1)
    return tile2(cos_t), tile2(sa_t), tile2(sb_t)


def _in_proj(x2, seq, g_mix, w_in):
    n = x2.shape[0]
    tm = _pick_tile(seq, 512)
    nl = seq // tm
    in_cols = w_in.shape[1]
    cos_t, sa_t, sb_t = _rope_tables(seq)
    row = lambda i: (i, 0)
    fixed = lambda i: (0, 0)
    tab = lambda i: (i % nl, 0)
    return pl.pallas_call(
        _in_proj_kernel,
        grid=(n // tm,),
        in_specs=[
            pl.BlockSpec((tm, D_MODEL), row),
            pl.BlockSpec((1, D_MODEL), fixed),
            pl.BlockSpec((D_MODEL, in_cols), fixed),
            pl.BlockSpec((tm, LANES), tab),
            pl.BlockSpec((tm, LANES), tab),
            pl.BlockSpec((tm, LANES), tab),
        ],
        out_specs=[
            pl.BlockSpec((tm, ATT_WIDTH), row),
            pl.BlockSpec((tm, ATT_WIDTH), row),
            pl.BlockSpec((tm, ATT_WIDTH), row),
            pl.BlockSpec((tm, SSM_WIDTH), row),
            pl.BlockSpec((tm, 2 * D_MODEL), row),
        ],
        out_shape=[
            jax.ShapeDtypeStruct((n, ATT_WIDTH), BF16),
            jax.ShapeDtypeStruct((n, ATT_WIDTH), BF16),
            jax.ShapeDtypeStruct((n, ATT_WIDTH), BF16),
            jax.ShapeDtypeStruct((n, SSM_WIDTH), BF16),
            jax.ShapeDtypeStruct((n, 2 * D_MODEL), BF16),
        ],
        compiler_params=_params(("parallel",)),
        name="in_proj",
    )(x2, g_mix.reshape(1, D_MODEL), w_in, cos_t, sa_t, sb_t)


def _attn_kernel(lam_ref, q_ref, k_ref, v_ref, g_ref, o_ref, vt_ref, m_ref, l_ref, acc_ref, *, tq, tk, nk):
    qi = pl.program_id(2)

    @pl.when(qi == 0)
    def _():
        def tr(j, c):
            off = pl.multiple_of(j * tk, tk)
            vt_ref[:, pl.ds(off, tk)] = v_ref[0, pl.ds(off, tk), :].astype(F32).T.astype(BF16)
            return c
        lax.fori_loop(0, nk, tr, 0)

    qt = q_ref[0].astype(F32).T
    row = lax.broadcasted_iota(jnp.int32, qt.shape, 0)
    qm = jnp.concatenate([jnp.where(row < HEAD_DIM, qt, 0.0),
                          jnp.where(row >= HEAD_DIM, qt, 0.0)], axis=1).astype(BF16)

    m_ref[...] = jnp.full(m_ref.shape, NEG_BIG, F32)
    l_ref[...] = jnp.zeros(l_ref.shape, F32)
    acc_ref[...] = jnp.zeros(acc_ref.shape, F32)

    def body(j, c):
        off = pl.multiple_of(j * tk, tk)
        kc = k_ref[0, pl.ds(off, tk), :]
        s = jnp.dot(kc, qm, preferred_element_type=F32)
        m_old = m_ref[...]
        m_new = jnp.maximum(m_old, jnp.max(s, axis=0, keepdims=True))
        alpha = jnp.exp(m_old - m_new)
        p = jnp.exp(s - m_new)
        l_ref[...] = alpha * l_ref[...] + jnp.sum(p, axis=0, keepdims=True)
        m_ref[...] = m_new
        pv = jnp.dot(vt_ref[:, pl.ds(off, tk)], p.astype(BF16), preferred_element_type=F32)
        acc_ref[...] = alpha * acc_ref[...] + pv
        return c
    lax.fori_loop(0, nk, body, 0)

    lam = lam_ref[0]
    inv_l = 1.0 / l_ref[...]
    o_all = acc_ref[...] * inv_l
    o = o_all[:, :tq] - lam * o_all[:, tq:]
    o = o * lax.rsqrt(jnp.mean(o * o, axis=0, keepdims=True) + EPS) * g_ref[...]
    o_ref[0] = (o * (1.0 - LAM_INIT)).T.astype(BF16)


def _attention(q, k, v, lam, subln_g, tq_pref=256, tk_pref=512):
    bsz, seq, _ = q.shape
    tq = _pick_tile(seq, tq_pref)
    tk = _pick_tile(seq, tk_pref)
    nk = seq // tk
    kern = functools.partial(_attn_kernel, tq=tq, tk=tk, nk=nk)
    return pl.pallas_call(
        kern,
        grid=(bsz, N_HEADS, seq // tq),
        in_specs=[
            pl.BlockSpec(memory_space=pltpu.SMEM),
            pl.BlockSpec((1, tq, LANES), lambda b, h, i: (b, i, h)),
            pl.BlockSpec((1, seq, LANES), lambda b, h, i: (b, 0, h)),
            pl.BlockSpec((1, seq, LANES), lambda b, h, i: (b, 0, h)),
            pl.BlockSpec((ATT_V_DIM, 1), lambda b, h, i: (0, 0)),
        ],
        out_specs=pl.BlockSpec((1, tq, LANES), lambda b, h, i: (b, i, h)),
        out_shape=jax.ShapeDtypeStruct((bsz, seq, ATT_WIDTH), BF16),
        scratch_shapes=[
            pltpu.VMEM((ATT_V_DIM, seq), BF16),
            pltpu.VMEM((1, 2 * tq), F32),
            pltpu.VMEM((1, 2 * tq), F32),
            pltpu.VMEM((ATT_V_DIM, 2 * tq), F32),
        ],
        compiler_params=_params(("parallel", "parallel", "arbitrary")),
        name="diff_attn",
    )(lam.reshape(1), q, k, v, subln_g.reshape(ATT_V_DIM, 1))


def _ssm_weights(lam_re, lam_im, log_dt, b_re, b_im, c_re, c_im, d_skip):
    T, G, P, CH = SSM_CHUNK, SSM_GROUPS, SSM_STATE, SSM_CH
    n = jnp.arange(T + 1, dtype=F32)[:, None, None]
    pw_re, pw_im, bb_re, bb_im = [], [], [], []
    for dirn in range(2):
        dt = jnp.exp(log_dt[dirn])[:, None]
        lr, li = lam_re[dirn], lam_im[dirn]
        mag = jnp.exp(lr * dt)
        a_re, a_im = mag * jnp.cos(li * dt), mag * jnp.sin(li * dt)
        den = lr * lr + li * li
        n_re = a_re - 1.0
        q_re = (n_re * lr + a_im * li) / den
        q_im = (a_im * lr - n_re * li) / den
        bb_re.append(q_re[..., None] * b_re[dirn] - q_im[..., None] * b_im[dirn])
        bb_im.append(q_re[..., None] * b_im[dirn] + q_im[..., None] * b_re[dirn])
        magn = jnp.exp(n * (lr * dt)[None])
        pw_re.append(magn * jnp.cos(n * (li * dt)[None]))
        pw_im.append(magn * jnp.sin(n * (li * dt)[None]))

    def cmul(ar, ai, br, bi):
        return ar * br - ai * bi, ar * bi + ai * br

    kern = []
    for dirn in range(2):
        car, cai = cmul(c_re[dirn][None], c_im[dirn][None],
                        pw_re[dirn][:T, :, None, :], pw_im[dirn][:T, :, None, :])
        kern.append(jnp.einsum('dgcp,gpk->dgck', car, bb_re[dirn], precision=HIGHEST)
                    - jnp.einsum('dgcp,gpk->dgck', cai, bb_im[dirn], precision=HIGHEST))
    tj = jnp.arange(T)
    lag = tj[None, :] - tj[:, None]
    mf = jnp.where((lag >= 0)[:, :, None, None, None], kern[0][jnp.clip(lag, 0, T - 1)], 0.0)
    mb = jnp.where((lag <= 0)[:, :, None, None, None], kern[1][jnp.clip(-lag, 0, T - 1)], 0.0)
    dsk = (lag == 0)[:, :, None, None, None] * (jnp.eye(CH, dtype=F32)[None, None, None] * d_skip[None, None, :, :, None])
    m_all = (mf + mb + dsk).transpose(2, 0, 4, 1, 3).reshape(G, T * CH, T * CH)

    wf_re, wf_im = cmul(pw_re[0][T - 1 - tj][:, :, :, None], pw_im[0][T - 1 - tj][:, :, :, None],
                        bb_re[0][None], bb_im[0][None])
    wb_re, wb_im = cmul(pw_re[1][tj][:, :, :, None], pw_im[1][tj][:, :, :, None], bb_re[1][None], bb_im[1][None])
    w_all = jnp.stack([wf_re, wf_im, wb_re, wb_im], axis=0).transpose(2, 1, 4, 0, 3)
    w_all = w_all.reshape(G // 2, 2, T * CH, 4, P)
    eye2 = jnp.eye(2, dtype=F32)
    w_pair = (w_all[:, :, :, :, None, :] * eye2[None, :, None, None, :, None]).reshape(G // 2, 2 * T * CH, 4 * 2 * P)

    zf_re, zf_im = cmul(c_re[0][None], c_im[0][None], pw_re[0][1:T + 1, :, None, :], pw_im[0][1:T + 1, :, None, :])
    zb_re, zb_im = cmul(c_re[1][None], c_im[1][None],
                        pw_re[1][T - tj][:, :, None, :], pw_im[1][T - tj][:, :, None, :])
    z_all = jnp.stack([zf_re, -zf_im, zb_re, -zb_im], axis=0).transpose(2, 0, 4, 1, 3)
    z_all = z_all.reshape(G // 2, 2, 4, P, T * CH)
    z_pair = (z_all.transpose(0, 2, 1, 3, 4)[:, :, :, :, None, :] * eye2[None, None, :, None, :, None])
    z_pair = z_pair.reshape(G // 2, 4 * 2 * P, 2 * T * CH)

    a_tab = []
    for dirn in range(2):
        dt = jnp.exp(log_dt[dirn])[:, None]
        order = jnp.arange(SUBLANES) if dirn == 0 else SUBLANES - 1 - jnp.arange(SUBLANES)
        nn = jnp.concatenate([order, jnp.array([1, 2, 4, 8])]).astype(F32)[:, None, None] * T
        magn = jnp.exp(nn * (lam_re[dirn] * dt)[None])
        ang = nn * (lam_im[dirn] * dt)[None]
        tab = jnp.stack([magn * jnp.cos(ang), magn * jnp.sin(ang)], axis=0)
        tab = jnp.pad(tab, ((0, 0), (0, 2 * SUBLANES - tab.shape[1]), (0, 0), (0, 0)))
        a_tab.append(tab.reshape(2, 2 * SUBLANES, G // 2, 2 * P).transpose(2, 0, 1, 3))
    a_tab = jnp.stack(a_tab, axis=1)
    return m_all.astype(BF16), w_pair.astype(BF16), z_pair.astype(BF16), a_tab


def _tile_scan(x_re, x_im, a_tab, reverse):
    row = lax.broadcasted_iota(jnp.int32, x_re.shape, 0)
    for lvl, d in enumerate((1, 2, 4)):
        ar = a_tab[0][SUBLANES + lvl:SUBLANES + lvl + 1, :]
        ai = a_tab[1][SUBLANES + lvl:SUBLANES + lvl + 1, :]
        if reverse:
            keep = row < SUBLANES - d
            shift = SUBLANES - d
        else:
            keep = row >= d
            shift = d
        sr = jnp.where(keep, pltpu.roll(x_re, shift, 0), 0.0)
        si = jnp.where(keep, pltpu.roll(x_im, shift, 0), 0.0)
        x_re, x_im = x_re + ar * sr - ai * si, x_im + ar * si + ai * sr
    return x_re, x_im


def _ssm_kernel(u_ref, w_ref, m_ref, z_ref, a_ref, y_ref, v_ref, s_ref, *, nc):
    pl_ = SSM_PAIR_LANES
    nt = nc // SUBLANES
    u0 = u_ref[0, 0]
    u1 = u_ref[0, 1]
    v_ref[...] = jnp.dot(jnp.concatenate([u0, u1], axis=1), w_ref[0], preferred_element_type=F32)

    row = lax.broadcasted_iota(jnp.int32, (SUBLANES, pl_), 0)

    def tile_step(i, carry):
        new = []
        for dirn in range(2):
            c_re, c_im = carry[2 * dirn], carry[2 * dirn + 1]
            a_tab = (a_ref[0, dirn, 0], a_ref[0, dirn, 1])
            t = i if dirn == 0 else nt - 1 - i
            r0 = pl.multiple_of(t * SUBLANES, SUBLANES)
            lo = 2 * dirn * pl_
            x_re, x_im = _tile_scan(v_ref[pl.ds(r0, SUBLANES), lo:lo + pl_],
                                    v_ref[pl.ds(r0, SUBLANES), lo + pl_:lo + 2 * pl_], a_tab, dirn == 1)
            if dirn == 0:
                keep, shift, last = row >= 1, 1, SUBLANES - 1
            else:
                keep, shift, last = row < SUBLANES - 1, SUBLANES - 1, 0
            pr, pi = a_tab[0][0:SUBLANES, :], a_tab[1][0:SUBLANES, :]
            s_ref[pl.ds(r0, SUBLANES), lo:lo + pl_] = (
                pr * c_re - pi * c_im + jnp.where(keep, pltpu.roll(x_re, shift, 0), 0.0))
            s_ref[pl.ds(r0, SUBLANES), lo + pl_:lo + 2 * pl_] = (
                pr * c_im + pi * c_re + jnp.where(keep, pltpu.roll(x_im, shift, 0), 0.0))
            a8r = a_tab[0][SUBLANES + 3:SUBLANES + 4, :]
            a8i = a_tab[1][SUBLANES + 3:SUBLANES + 4, :]
            new.append(a8r * c_re - a8i * c_im + x_re[last:last + 1, :])
            new.append(a8r * c_im + a8i * c_re + x_im[last:last + 1, :])
        return tuple(new)

    zero = jnp.zeros((1, pl_), F32)
    lax.fori_loop(0, nt, tile_step, (zero, zero, zero, zero))

    ys = jnp.dot(s_ref[...].astype(BF16), z_ref[0], preferred_element_type=F32)
    half = ys.shape[1] // 2
    y_ref[0, 0] = ys[:, :half] + jnp.dot(u0, m_ref[0], preferred_element_type=F32)
    y_ref[0, 1] = ys[:, half:] + jnp.dot(u1, m_ref[1], preferred_element_type=F32)


def _ssm(s_in, bsz, seq, weights):
    m_all, w_pair, z_pair, a_pair = weights
    T, G, CH = SSM_CHUNK, SSM_GROUPS, SSM_CH
    nc = seq // T
    tc = T * CH
    ug = s_in.reshape(bsz, nc, T, G, CH).transpose(0, 3, 1, 2, 4).reshape(bsz, G, nc, tc)
    kern = functools.partial(_ssm_kernel, nc=nc)
    yg = pl.pallas_call(
        kern,
        grid=(G // 2, bsz),
        in_specs=[
            pl.BlockSpec((1, 2, nc, tc), lambda g, b: (b, g, 0, 0)),
            pl.BlockSpec((1, 2 * tc, 4 * SSM_PAIR_LANES), lambda g, b: (g, 0, 0)),
            pl.BlockSpec((2, tc, tc), lambda g, b: (g, 0, 0)),
            pl.BlockSpec((1, 4 * SSM_PAIR_LANES, 2 * tc), lambda g, b: (g, 0, 0)),
            pl.BlockSpec((1, 2, 2, 2 * SUBLANES, SSM_PAIR_LANES), lambda g, b: (g, 0, 0, 0, 0)),
        ],
        out_specs=pl.BlockSpec((1, 2, nc, tc), lambda g, b: (b, g, 0, 0)),
        out_shape=jax.ShapeDtypeStruct((bsz, G, nc, tc), F32),
        scratch_shapes=[
            pltpu.VMEM((nc, 4 * SSM_PAIR_LANES), F32),
            pltpu.VMEM((nc, 4 * SSM_PAIR_LANES), F32),
        ],
        compiler_params=_params(("parallel", "arbitrary")),
        name="s5_scan",
    )(ug, w_pair, m_all, z_pair, a_pair)
    return yg.reshape(bsz, G, nc, T, CH).transpose(0, 2, 3, 1, 4).reshape(bsz * seq, SSM_WIDTH)


def _mix_kernel(h_ref, o_ref, ys_ref, gl_ref, wglu_ref, bglu_ref, wa_ref, ws_ref, wo_ref, out_ref):
    yg = jax.nn.gelu(ys_ref[...])
    z = jnp.dot(yg.astype(BF16), wglu_ref[...], preferred_element_type=F32) + bglu_ref[...]
    ysg = (yg * _sigmoid(z)).astype(BF16)
    br_a = jnp.dot(o_ref[...], wa_ref[...], preferred_element_type=F32)
    br_s = jnp.dot(ysg, ws_ref[...], preferred_element_type=F32)
    gl = gl_ref[...].astype(F32)
    mix = _sigmoid(gl[:, :D_MODEL]) * br_a + _sigmoid(gl[:, D_MODEL:]) * br_s
    out_ref[...] = h_ref[...] + jnp.dot(mix.astype(BF16), wo_ref[...], preferred_element_type=F32)


def _mix(h, o, ys, gl, w_glu, b_glu, w_attn_br, w_ssm_br, w_out):
    n = h.shape[0]
    tm = _pick_tile(n, 512)
    row = lambda i: (i, 0)
    fixed = lambda i: (0, 0)
    return pl.pallas_call(
        _mix_kernel,
        grid=(n // tm,),
        in_specs=[
            pl.BlockSpec((tm, D_MODEL), row),
            pl.BlockSpec((tm, ATT_WIDTH), row),
            pl.BlockSpec((tm, SSM_WIDTH), row),
            pl.BlockSpec((tm, 2 * D_MODEL), row),
            pl.BlockSpec((SSM_WIDTH, SSM_WIDTH), fixed),
            pl.BlockSpec((1, SSM_WIDTH), fixed),
            pl.BlockSpec((ATT_WIDTH, D_MODEL), fixed),
            pl.BlockSpec((SSM_WIDTH, D_MODEL), fixed),
            pl.BlockSpec((D_MODEL, D_MODEL), fixed),
        ],
        out_specs=pl.BlockSpec((tm, D_MODEL), row),
        out_shape=jax.ShapeDtypeStruct((n, D_MODEL), F32),
        compiler_params=_params(("parallel",)),
        name="branch_mix",
    )(h, o, ys, gl, w_glu, b_glu.reshape(1, SSM_WIDTH), w_attn_br, w_ssm_br, w_out)


def _lane_max(x):
    return jnp.max(x, axis=-1, keepdims=True)


def _router(u, wr_ref, br_ref):
    logits = jnp.dot(u, wr_ref[...], preferred_element_type=F32) + br_ref[...]
    lane = lax.broadcasted_iota(jnp.int32, logits.shape, 1).astype(F32)
    far = float(4 * LANES)
    is_grp = (lane >= N_EXPERTS) & (lane < N_EXPERTS + N_GROUPS)
    lg = jnp.where(is_grp, logits, NEG_BIG)
    g_max = _lane_max(lg)
    p_sel = 1.0 / jnp.sum(jnp.where(is_grp, jnp.exp(lg - g_max), 0.0), axis=-1, keepdims=True)
    g_lo = (jnp.min(jnp.where(lg == g_max, lane, far), axis=-1, keepdims=True) - N_EXPERTS) * EXPERTS_PER_GROUP
    in_grp = (lane >= g_lo) & (lane < g_lo + EXPERTS_PER_GROUP)
    l2 = jnp.where(in_grp, logits, NEG_BIG)
    m1 = _lane_max(l2)
    i1 = jnp.min(jnp.where(l2 == m1, lane, far), axis=-1, keepdims=True)
    l2b = jnp.where(lane == i1, NEG_BIG, l2)
    m2 = _lane_max(l2b)
    i2 = jnp.min(jnp.where(l2b == m2, lane, far), axis=-1, keepdims=True)
    e2 = jnp.exp(m2 - m1)
    w1 = 1.0 / (1.0 + e2)
    w2 = e2 * w1
    return p_sel * (jnp.where(lane == i1, w1, 0.0) + jnp.where(lane == i2, w2, 0.0))


def _moe_kernel(h_ref, g_ref, wr_ref, br_ref, w13_ref, w2_ref, out_ref, t_ref, gate_ref, acc_ref):
    e = pl.program_id(1)

    @pl.when(e == 0)
    def _():
        u = _rms(h_ref[...], g_ref[...]).astype(BF16)
        t_ref[...] = u
        gate_ref[...] = _router(u, wr_ref, br_ref)
        acc_ref[...] = jnp.zeros(acc_ref.shape, F32)

    ab = jnp.dot(t_ref[...], w13_ref[0], preferred_element_type=F32)
    a = ab[:, :EXPERT_FF]
    hdn = a * _sigmoid(a) * ab[:, EXPERT_FF:]
    gates = gate_ref[...]
    lane = lax.broadcasted_iota(jnp.int32, gates.shape, 1)
    ge = jnp.sum(jnp.where(lane == e, gates, 0.0), axis=-1, keepdims=True)
    acc_ref[...] += jnp.dot((hdn * ge).astype(BF16), w2_ref[0], preferred_element_type=F32)

    @pl.when(e == N_EXPERTS - 1)
    def _():
        out_ref[...] = h_ref[...] + acc_ref[...]


def _moe(h, g_ffn, w_router, b_router, w13, w2):
    n = h.shape[0]
    tm = _pick_tile(n, 1024)
    row = lambda i, e: (i, 0)
    fixed = lambda i, e: (0, 0)
    return pl.pallas_call(
        _moe_kernel,
        grid=(n // tm, N_EXPERTS),
        in_specs=[
            pl.BlockSpec((tm, D_MODEL), row),
            pl.BlockSpec((1, D_MODEL), fixed),
            pl.BlockSpec((D_MODEL, LANES), fixed),
            pl.BlockSpec((1, LANES), fixed),
            pl.BlockSpec((1, D_MODEL, 2 * EXPERT_FF), lambda i, e: (e, 0, 0)),
            pl.BlockSpec((1, EXPERT_FF, D_MODEL), lambda i, e: (e, 0, 0)),
        ],
        out_specs=pl.BlockSpec((tm, D_MODEL), row),
        out_shape=jax.ShapeDtypeStruct((n, D_MODEL), F32),
        scratch_shapes=[
            pltpu.VMEM((tm, D_MODEL), BF16),
            pltpu.VMEM((tm, LANES), F32),
            pltpu.VMEM((tm, D_MODEL), F32),
        ],
        compiler_params=_params(("parallel", "arbitrary")),
        name="hier_moe",
    )(h, g_ffn.reshape(1, D_MODEL), w_router, b_router, w13, w2)


def _ple_kernel(h_ref, p_ref, gp_ref, wg_ref, wp_ref, gf_ref, out_ref):
    h = h_ref[...]
    gate = _sigmoid(jnp.dot(_rms(h, gp_ref[...]).astype(BF16), wg_ref[...], preferred_element_type=F32))
    pe = jnp.dot(p_ref[...].astype(BF16), wp_ref[...], preferred_element_type=F32)
    out_ref[...] = _rms(h + gate * pe, gf_ref[...])


def _ple(h, p, g_ple, w_ple_gate, w_ple, g_final):
    n = h.shape[0]
    tm = _pick_tile(n, 512)
    row = lambda i: (i, 0)
    fixed = lambda i: (0, 0)
    return pl.pallas_call(
        _ple_kernel,
        grid=(n // tm,),
        in_specs=[
            pl.BlockSpec((tm, D_MODEL), row),
            pl.BlockSpec((tm, PLE_DIM), row),
            pl.BlockSpec((1, D_MODEL), fixed),
            pl.BlockSpec((D_MODEL, D_MODEL), fixed),
            pl.BlockSpec((PLE_DIM, D_MODEL), fixed),
            pl.BlockSpec((1, D_MODEL), fixed),
        ],
        out_specs=pl.BlockSpec((tm, D_MODEL), row),
        out_shape=jax.ShapeDtypeStruct((n, D_MODEL), F32),
        compiler_params=_params(("parallel",)),
        name="ple_final",
    )(h, p, g_ple.reshape(1, D_MODEL), w_ple_gate, w_ple, g_final.reshape(1, D_MODEL))


def _prep_weights(g_mix, w_in, lam_q1, lam_k1, lam_q2, lam_k2, subln_g,
                  ssm_lam_re, ssm_lam_im, ssm_log_dt, ssm_b_re, ssm_b_im, ssm_c_re, ssm_c_im, ssm_d,
                  w_glu, b_glu, w_attn_br, w_ssm_br, w_out, g_ffn, w_r1, b_r1, w_r2, b_r2,
                  w_e1, w_e3, w_e2, g_ple, w_ple_gate, w_ple, g_final):
    i = 0
    lam = (jnp.exp(jnp.sum(lam_q1[i] * lam_k1[i])) - jnp.exp(jnp.sum(lam_q2[i] * lam_k2[i])) + LAM_INIT)
    pad = LANES - N_EXPERTS - N_GROUPS
    w_router = jnp.concatenate([w_r2[i], w_r1[i], jnp.zeros((D_MODEL, pad), F32)], axis=1).astype(BF16)
    b_router = jnp.concatenate([b_r2[i], b_r1[i], jnp.zeros((pad,), F32)]).reshape(1, LANES)
    return dict(
        g_mix=g_mix[i], w_in=w_in[i].astype(BF16), lam=lam.astype(F32), subln_g=subln_g[i],
        ssm=_ssm_weights(ssm_lam_re[i], ssm_lam_im[i], ssm_log_dt[i], ssm_b_re[i], ssm_b_im[i],
                         ssm_c_re[i], ssm_c_im[i], ssm_d[i]),
        w_glu=w_glu[i].astype(BF16), b_glu=b_glu[i], w_attn_br=w_attn_br[i].astype(BF16),
        w_ssm_br=w_ssm_br[i].astype(BF16), w_out=w_out[i].astype(BF16), g_ffn=g_ffn[i],
        w_router=w_router, b_router=b_router,
        w13=jnp.concatenate([w_e1[i], w_e3[i]], axis=-1).astype(BF16), w2=w_e2[i].astype(BF16),
        g_ple=g_ple[i], w_ple_gate=w_ple_gate[i].astype(BF16), w_ple=w_ple[i].astype(BF16), g_final=g_final,
    )


def _trunk(x, p, w):
    bsz, seq, _ = x.shape
    n = bsz * seq
    x2 = x.reshape(n, D_MODEL)
    q, k, v, s_in, gl = _in_proj(x2, seq, w["g_mix"], w["w_in"])
    shp = (bsz, seq, ATT_WIDTH)
    o = _attention(q.reshape(shp), k.reshape(shp), v.reshape(shp), w["lam"], w["subln_g"])
    ys = _ssm(s_in, bsz, seq, w["ssm"])
    h = _mix(x2, o.reshape(n, ATT_WIDTH), ys, gl, w["w_glu"], w["b_glu"], w["w_attn_br"], w["w_ssm_br"], w["w_out"])
    h = _moe(h, w["g_ffn"], w["w_router"], w["b_router"], w["w13"], w["w2"])
    y = _ple(h, p[0].reshape(n, PLE_DIM), w["g_ple"], w["w_ple_gate"], w["w_ple"], w["g_final"])
    return y.reshape(bsz, seq, D_MODEL)


def kernel(x_prompt, x_sample, p_prompt, p_sample, g_mix, w_in, lam_q1, lam_k1, lam_q2, lam_k2, subln_g, ssm_lam_re, ssm_lam_im, ssm_log_dt, ssm_b_re, ssm_b_im, ssm_c_re, ssm_c_im, ssm_d, w_glu, b_glu, w_attn_br, w_ssm_br, w_out, g_ffn, w_r1, b_r1, w_r2, b_r2, w_e1, w_e3, w_e2, g_ple, w_ple_gate, w_ple, g_final):
    w = _prep_weights(g_mix, w_in, lam_q1, lam_k1, lam_q2, lam_k2, subln_g,
                      ssm_lam_re, ssm_lam_im, ssm_log_dt, ssm_b_re, ssm_b_im, ssm_c_re, ssm_c_im, ssm_d,
                      w_glu, b_glu, w_attn_br, w_ssm_br, w_out, g_ffn, w_r1, b_r1, w_r2, b_r2,
                      w_e1, w_e3, w_e2, g_ple, w_ple_gate, w_ple, g_final)
    return (_trunk(x_prompt, p_prompt, w), _trunk(x_sample, p_sample, w))
```

```python
import functools
import math

import jax
import jax.numpy as jnp
from jax import lax
from jax.experimental import pallas as pl
from jax.experimental.pallas import tpu as pltpu

F32 = jnp.float32
BF16 = jnp.bfloat16
HIGHEST = lax.Precision.HIGHEST

D_MODEL = 1024
N_HEADS = 8
HEAD_DIM = 64
ATT_V_DIM = 2 * HEAD_DIM
ATT_WIDTH = N_HEADS * ATT_V_DIM
ROT_DIM = HEAD_DIM // 4
ROPE_THETA = 500000.0
SSM_CH = 16
SSM_GROUPS = 32
SSM_WIDTH = SSM_GROUPS * SSM_CH
SSM_STATE = 64
N_GROUPS = 4
EXPERTS_PER_GROUP = 8
N_EXPERTS = N_GROUPS * EXPERTS_PER_GROUP
EXPERT_FF = 256
PLE_DIM = 256
EPS = 1e-6
LAM_INIT = 0.8 - 0.6 * math.exp(-0.3 * 0)

LANES = 128
SUBLANES = 8
SSM_CHUNK = 32
SSM_PAIR_LANES = 2 * SSM_STATE
VMEM_LIMIT = 56 * 1024 * 1024
NEG_BIG = -1e30
LOG2E = math.log2(math.e)


def _params(sem):
    return pltpu.CompilerParams(dimension_semantics=sem, vmem_limit_bytes=VMEM_LIMIT)


def _rms(x, g):
    return x * lax.rsqrt(jnp.mean(x * x, axis=-1, keepdims=True) + EPS) * g


def _sigmoid(x):
    return 1.0 / (1.0 + jnp.exp(-x))


def _pick_tile(n, pref):
    t = min(n, pref)
    while n % t:
        t //= 2
    return t


def _in_proj_kernel(x_ref, g_ref, w_ref, cos_ref, sa_ref, sb_ref,
                    q_ref, k_ref, v_ref, s_ref, gl_ref):
    u = _rms(x_ref[...], g_ref[...]).astype(BF16)
    cos = cos_ref[...]
    sa = sa_ref[...]
    sb = sb_ref[...]
    for sec, out_ref, scale in ((0, q_ref, HEAD_DIM ** -0.5 * LOG2E), (1, k_ref, 1.0)):
        y = jnp.dot(u, w_ref[:, sec * ATT_WIDTH:(sec + 1) * ATT_WIDTH], preferred_element_type=F32)
        for h in range(N_HEADS):
            yh = y[:, h * LANES:(h + 1) * LANES]
            r = yh * cos + pltpu.roll(yh, LANES - ROT_DIM // 2, 1) * sa + pltpu.roll(yh, ROT_DIM // 2, 1) * sb
            out_ref[:, h * LANES:(h + 1) * LANES] = (r * scale).astype(BF16)
    c0 = 2 * ATT_WIDTH
    v_ref[...] = jnp.dot(u, w_ref[:, c0:c0 + ATT_WIDTH], preferred_element_type=F32).astype(BF16)
    c0 += ATT_WIDTH
    s_ref[...] = jnp.dot(u, w_ref[:, c0:c0 + SSM_WIDTH], preferred_element_type=F32).astype(BF16)
    c0 += SSM_WIDTH
    gl_ref[...] = jnp.dot(u, w_ref[:, c0:c0 + 2 * D_MODEL], preferred_element_type=F32).astype(BF16)


def _rope_tables(seq):
    half = ROT_DIM // 2
    inv = ROPE_THETA ** (-jnp.arange(0, ROT_DIM, 2, dtype=F32) / ROT_DIM)
    ang = jnp.arange(seq, dtype=F32)[:, None] * inv[None, :]
    cos, sin = jnp.cos(ang), jnp.sin(ang)
    one = jnp.ones((seq, HEAD_DIM - ROT_DIM), F32)
    zero = jnp.zeros((seq, HEAD_DIM - ROT_DIM), F32)
    zh = jnp.zeros((seq, half), F32)
    cos_t = jnp.concatenate([cos, cos, one], axis=1)
    sa_t = jnp.concatenate([-sin, zh, zero], axis=1)
    sb_t = jnp.concatenate([zh, sin, zero], axis=1)
    tile2 = lambda t: jnp.concatenate([t, t], axis=1)
    return tile2(cos_t), tile2(sa_t), tile2(sb_t)


def _in_proj(x2, seq, g_mix, w_in):
    n = x2.shape[0]
    tm = _pick_tile(seq, 512)
    nl = seq // tm
    in_cols = w_in.shape[1]
    cos_t, sa_t, sb_t = _rope_tables(seq)
    row = lambda i: (i, 0)
    fixed = lambda i: (0, 0)
    tab = lambda i: (i % nl, 0)
    return pl.pallas_call(
        _in_proj_kernel,
        grid=(n // tm,),
        in_specs=[
            pl.BlockSpec((tm, D_MODEL), row),
            pl.BlockSpec((1, D_MODEL), fixed),
            pl.BlockSpec((D_MODEL, in_cols), fixed),
            pl.BlockSpec((tm, LANES), tab),
            pl.BlockSpec((tm, LANES), tab),
            pl.BlockSpec((tm, LANES), tab),
        ],
        out_specs=[
            pl.BlockSpec((tm, ATT_WIDTH), row),
            pl.BlockSpec((tm, ATT_WIDTH), row),
            pl.BlockSpec((tm, ATT_WIDTH), row),
            pl.BlockSpec((tm, SSM_WIDTH), row),
            pl.BlockSpec((tm, 2 * D_MODEL), row),
        ],
        out_shape=[
            jax.ShapeDtypeStruct((n, ATT_WIDTH), BF16),
            jax.ShapeDtypeStruct((n, ATT_WIDTH), BF16),
            jax.ShapeDtypeStruct((n, ATT_WIDTH), BF16),
            jax.ShapeDtypeStruct((n, SSM_WIDTH), BF16),
            jax.ShapeDtypeStruct((n, 2 * D_MODEL), BF16),
        ],
        compiler_params=_params(("parallel",)),
        name="in_proj",
    )(x2, g_mix.reshape(1, D_MODEL), w_in, cos_t, sa_t, sb_t)


def _rows_reduce(x, pair_op, reduce_op):
    r = x.shape[0]
    while r > SUBLANES and r % (2 * SUBLANES) == 0:
        r //= 2
        x = pair_op(x[:r], x[r:])
    return reduce_op(x, axis=0, keepdims=True)


def _attn_kernel(lam_ref, q_ref, k_ref, v_ref, g_ref, o_ref, vt_ref, m_ref, l_ref, acc_ref, s_ref, p_ref, al_ref,
                 *, tq, tk, nk):
    qi = pl.program_id(2)

    @pl.when(qi == 0)
    def _():
        def tr(j, c):
            off = pl.multiple_of(j * tk, tk)
            vt_ref[:, pl.ds(off, tk)] = v_ref[0, pl.ds(off, tk), :].astype(F32).T.astype(BF16)
            return c
        lax.fori_loop(0, nk, tr, 0)

    qt = q_ref[0].astype(F32).T
    row = lax.broadcasted_iota(jnp.int32, qt.shape, 0)
    qm = jnp.concatenate([jnp.where(row < HEAD_DIM, qt, 0.0),
                          jnp.where(row >= HEAD_DIM, qt, 0.0)], axis=1).astype(BF16)

    m_ref[...] = jnp.full(m_ref.shape, NEG_BIG, F32)
    l_ref[...] = jnp.zeros(l_ref.shape, F32)
    acc_ref[...] = jnp.zeros(acc_ref.shape, F32)

    def scores(j, slot):
        off = pl.multiple_of(j * tk, tk)
        s_ref[slot] = jnp.dot(k_ref[0, pl.ds(off, tk), :], qm, preferred_element_type=F32)

    def softmax(slot):
        s = s_ref[slot]
        m_old = m_ref[...]
        m_new = jnp.maximum(m_old, _rows_reduce(s, jnp.maximum, jnp.max))
        alpha = jnp.exp2(m_old - m_new)
        p = jnp.exp2(s - m_new)
        l_ref[...] = alpha * l_ref[...] + _rows_reduce(p, jnp.add, jnp.sum)
        m_ref[...] = m_new
        al_ref[slot] = alpha
        p_ref[slot] = p.astype(BF16)

    def values(j, slot):
        off = pl.multiple_of(j * tk, tk)
        pv = jnp.dot(vt_ref[:, pl.ds(off, tk)], p_ref[slot], preferred_element_type=F32)
        acc_ref[...] = al_ref[slot] * acc_ref[...] + pv

    def step(j, slot):
        scores(j + 1, 1 - slot)
        softmax(slot)
        values(j - 1, 1 - slot)

    scores(0, 0)
    if nk == 1:
        softmax(0)
    else:
        scores(1, 1)
        softmax(0)

        def body(i, c):
            step(2 * i + 1, 1)
            step(2 * i + 2, 0)
            return c
        lax.fori_loop(0, (nk - 2) // 2, body, 0)
        softmax(1)
        values(nk - 2, 0)
    values(nk - 1, (nk - 1) % 2)

    lam = lam_ref[0]
    inv_l = 1.0 / l_ref[...]
    o_all = acc_ref[...] * inv_l
    o = o_all[:, :tq] - lam * o_all[:, tq:]
    o = o * lax.rsqrt(jnp.mean(o * o, axis=0, keepdims=True) + EPS) * g_ref[...]
    o_ref[0] = (o * (1.0 - LAM_INIT)).T.astype(BF16)


def _attention(q, k, v, lam, subln_g, tq_pref=256, tk_pref=256):
    bsz, seq, _ = q.shape
    tq = _pick_tile(seq, tq_pref)
    tk = _pick_tile(seq, tk_pref)
    nk = seq // tk
    kern = functools.partial(_attn_kernel, tq=tq, tk=tk, nk=nk)
    return pl.pallas_call(
        kern,
        grid=(bsz, N_HEADS, seq // tq),
        in_specs=[
            pl.BlockSpec(memory_space=pltpu.SMEM),
            pl.BlockSpec((1, tq, LANES), lambda b, h, i: (b, i, h)),
            pl.BlockSpec((1, seq, LANES), lambda b, h, i: (b, 0, h)),
            pl.BlockSpec((1, seq, LANES), lambda b, h, i: (b, 0, h)),
            pl.BlockSpec((ATT_V_DIM, 1), lambda b, h, i: (0, 0)),
        ],
        out_specs=pl.BlockSpec((1, tq, LANES), lambda b, h, i: (b, i, h)),
        out_shape=jax.ShapeDtypeStruct((bsz, seq, ATT_WIDTH), BF16),
        scratch_shapes=[
            pltpu.VMEM((ATT_V_DIM, seq), BF16),
            pltpu.VMEM((1, 2 * tq), F32),
            pltpu.VMEM((1, 2 * tq), F32),
            pltpu.VMEM((ATT_V_DIM, 2 * tq), F32),
            pltpu.VMEM((2, tk, 2 * tq), F32),
            pltpu.VMEM((2, tk, 2 * tq), BF16),
            pltpu.VMEM((2, 1, 2 * tq), F32),
        ],
        compiler_params=_params(("parallel", "parallel", "arbitrary")),
        name="diff_attn",
    )(lam.reshape(1), q, k, v, subln_g.reshape(ATT_V_DIM, 1))


def _ssm_weights(lam_re, lam_im, log_dt, b_re, b_im, c_re, c_im, d_skip):
    T, G, P, CH = SSM_CHUNK, SSM_GROUPS, SSM_STATE, SSM_CH
    n = jnp.arange(T + 1, dtype=F32)[:, None, None]
    pw_re, pw_im, bb_re, bb_im = [], [], [], []
    for dirn in range(2):
        dt = jnp.exp(log_dt[dirn])[:, None]
        lr, li = lam_re[dirn], lam_im[dirn]
        mag = jnp.exp(lr * dt)
        a_re, a_im = mag * jnp.cos(li * dt), mag * jnp.sin(li * dt)
        den = lr * lr + li * li
        n_re = a_re - 1.0
        q_re = (n_re * lr + a_im * li) / den
        q_im = (a_im * lr - n_re * li) / den
        bb_re.append(q_re[..., None] * b_re[dirn] - q_im[..., None] * b_im[dirn])
        bb_im.append(q_re[..., None] * b_im[dirn] + q_im[..., None] * b_re[dirn])
        magn = jnp.exp(n * (lr * dt)[None])
        pw_re.append(magn * jnp.cos(n * (li * dt)[None]))
        pw_im.append(magn * jnp.sin(n * (li * dt)[None]))

    def cmul(ar, ai, br, bi):
        return ar * br - ai * bi, ar * bi + ai * br

    kern = []
    for dirn in range(2):
        car, cai = cmul(c_re[dirn][None], c_im[dirn][None],
                        pw_re[dirn][:T, :, None, :], pw_im[dirn][:T, :, None, :])
        kern.append(jnp.einsum('dgcp,gpk->dgck', car, bb_re[dirn], precision=HIGHEST)
                    - jnp.einsum('dgcp,gpk->dgck', cai, bb_im[dirn], precision=HIGHEST))
    tj = jnp.arange(T)
    lag = tj[None, :] - tj[:, None]
    mf = jnp.where((lag >= 0)[:, :, None, None, None], kern[0][jnp.clip(lag, 0, T - 1)], 0.0)
    mb = jnp.where((lag <= 0)[:, :, None, None, None], kern[1][jnp.clip(-lag, 0, T - 1)], 0.0)
    dsk = (lag == 0)[:, :, None, None, None] * (jnp.eye(CH, dtype=F32)[None, None, None] * d_skip[None, None, :, :, None])
    m_all = (mf + mb + dsk).transpose(2, 0, 4, 1, 3).reshape(G, T * CH, T * CH)

    wf_re, wf_im = cmul(pw_re[0][T - 1 - tj][:, :, :, None], pw_im[0][T - 1 - tj][:, :, :, None],
                        bb_re[0][None], bb_im[0][None])
    wb_re, wb_im = cmul(pw_re[1][tj][:, :, :, None], pw_im[1][tj][:, :, :, None], bb_re[1][None], bb_im[1][None])
    w_all = jnp.stack([wf_re, wf_im, wb_re, wb_im], axis=0).transpose(2, 1, 4, 0, 3)
    w_all = w_all.reshape(G // 2, 2, T * CH, 4, P)
    eye2 = jnp.eye(2, dtype=F32)
    w_pair = (w_all[:, :, :, :, None, :] * eye2[None, :, None, None, :, None]).reshape(G // 2, 2 * T * CH, 4 * 2 * P)

    zf_re, zf_im = cmul(c_re[0][None], c_im[0][None], pw_re[0][1:T + 1, :, None, :], pw_im[0][1:T + 1, :, None, :])
    zb_re, zb_im = cmul(c_re[1][None], c_im[1][None],
                        pw_re[1][T - tj][:, :, None, :], pw_im[1][T - tj][:, :, None, :])
    z_all = jnp.stack([zf_re, -zf_im, zb_re, -zb_im], axis=0).transpose(2, 0, 4, 1, 3)
    z_all = z_all.reshape(G // 2, 2, 4, P, T * CH)
    z_pair = (z_all.transpose(0, 2, 1, 3, 4)[:, :, :, :, None, :] * eye2[None, None, :, None, :, None])
    z_pair = z_pair.reshape(G // 2, 4 * 2 * P, 2 * T * CH)

    a_tab = []
    for dirn in range(2):
        dt = jnp.exp(log_dt[dirn])[:, None]
        order = jnp.arange(SUBLANES) if dirn == 0 else SUBLANES - 1 - jnp.arange(SUBLANES)
        nn = jnp.concatenate([order, jnp.array([1, 2, 4, 8])]).astype(F32)[:, None, None] * T
        magn = jnp.exp(nn * (lam_re[dirn] * dt)[None])
        ang = nn * (lam_im[dirn] * dt)[None]
        tab = jnp.stack([magn * jnp.cos(ang), magn * jnp.sin(ang)], axis=0)
        tab = jnp.pad(tab, ((0, 0), (0, 2 * SUBLANES - tab.shape[1]), (0, 0), (0, 0)))
        a_tab.append(tab.reshape(2, 2 * SUBLANES, G // 2, 2 * P).transpose(2, 0, 1, 3))
    a_tab = jnp.stack(a_tab, axis=1)
    return m_all.astype(BF16), w_pair.astype(BF16), z_pair.astype(BF16), a_tab


def _tile_scan(x_re, x_im, a_tab, reverse):
    row = lax.broadcasted_iota(jnp.int32, x_re.shape, 0)
    for lvl, d in enumerate((1, 2, 4)):
        ar = a_tab[0][SUBLANES + lvl:SUBLANES + lvl + 1, :]
        ai = a_tab[1][SUBLANES + lvl:SUBLANES + lvl + 1, :]
        if reverse:
            keep = row < SUBLANES - d
            shift = SUBLANES - d
        else:
            keep = row >= d
            shift = d
        sr = jnp.where(keep, pltpu.roll(x_re, shift, 0), 0.0)
        si = jnp.where(keep, pltpu.roll(x_im, shift, 0), 0.0)
        x_re, x_im = x_re + ar * sr - ai * si, x_im + ar * si + ai * sr
    return x_re, x_im


def _ssm_kernel(u_ref, w_ref, m_ref, z_ref, a_ref, y_ref, v_ref, s_ref, *, nc):
    pl_ = SSM_PAIR_LANES
    nt = nc // SUBLANES
    u0 = u_ref[0, 0]
    u1 = u_ref[0, 1]
    v_ref[...] = jnp.dot(jnp.concatenate([u0, u1], axis=1), w_ref[0], preferred_element_type=F32)

    row = lax.broadcasted_iota(jnp.int32, (SUBLANES, pl_), 0)

    def tile_step(i, carry):
        new = []
        for dirn in range(2):
            c_re, c_im = carry[2 * dirn], carry[2 * dirn + 1]
            a_tab = (a_ref[0, dirn, 0], a_ref[0, dirn, 1])
            t = i if dirn == 0 else nt - 1 - i
            r0 = pl.multiple_of(t * SUBLANES, SUBLANES)
            lo = 2 * dirn * pl_
            x_re, x_im = _tile_scan(v_ref[pl.ds(r0, SUBLANES), lo:lo + pl_],
                                    v_ref[pl.ds(r0, SUBLANES), lo + pl_:lo + 2 * pl_], a_tab, dirn == 1)
            if dirn == 0:
                keep, shift, last = row >= 1, 1, SUBLANES - 1
            else:
                keep, shift, last = row < SUBLANES - 1, SUBLANES - 1, 0
            pr, pi = a_tab[0][0:SUBLANES, :], a_tab[1][0:SUBLANES, :]
            s_ref[pl.ds(r0, SUBLANES), lo:lo + pl_] = (
                pr * c_re - pi * c_im + jnp.where(keep, pltpu.roll(x_re, shift, 0), 0.0))
            s_ref[pl.ds(r0, SUBLANES), lo + pl_:lo + 2 * pl_] = (
                pr * c_im + pi * c_re + jnp.where(keep, pltpu.roll(x_im, shift, 0), 0.0))
            a8r = a_tab[0][SUBLANES + 3:SUBLANES + 4, :]
            a8i = a_tab[1][SUBLANES + 3:SUBLANES + 4, :]
            new.append(a8r * c_re - a8i * c_im + x_re[last:last + 1, :])
            new.append(a8r * c_im + a8i * c_re + x_im[last:last + 1, :])
        return tuple(new)

    zero = jnp.zeros((1, pl_), F32)
    lax.fori_loop(0, nt, tile_step, (zero, zero, zero, zero))

    ys = jnp.dot(s_ref[...].astype(BF16), z_ref[0], preferred_element_type=F32)
    half = ys.shape[1] // 2
    y_ref[0, 0] = ys[:, :half] + jnp.dot(u0, m_ref[0], preferred_element_type=F32)
    y_ref[0, 1] = ys[:, half:] + jnp.dot(u1, m_ref[1], preferred_element_type=F32)


def _ssm(s_in, bsz, seq, weights):
    m_all, w_pair, z_pair, a_pair = weights
    T, G, CH = SSM_CHUNK, SSM_GROUPS, SSM_CH
    nc = seq // T
    tc = T * CH
    ug = s_in.reshape(bsz, nc, T, G, CH).transpose(0, 3, 1, 2, 4).reshape(bsz, G, nc, tc)
    kern = functools.partial(_ssm_kernel, nc=nc)
    yg = pl.pallas_call(
        kern,
        grid=(G // 2, bsz),
        in_specs=[
            pl.BlockSpec((1, 2, nc, tc), lambda g, b: (b, g, 0, 0)),
            pl.BlockSpec((1, 2 * tc, 4 * SSM_PAIR_LANES), lambda g, b: (g, 0, 0)),
            pl.BlockSpec((2, tc, tc), lambda g, b: (g, 0, 0)),
            pl.BlockSpec((1, 4 * SSM_PAIR_LANES, 2 * tc), lambda g, b: (g, 0, 0)),
            pl.BlockSpec((1, 2, 2, 2 * SUBLANES, SSM_PAIR_LANES), lambda g, b: (g, 0, 0, 0, 0)),
        ],
        out_specs=pl.BlockSpec((1, 2, nc, tc), lambda g, b: (b, g, 0, 0)),
        out_shape=jax.ShapeDtypeStruct((bsz, G, nc, tc), F32),
        scratch_shapes=[
            pltpu.VMEM((nc, 4 * SSM_PAIR_LANES), F32),
            pltpu.VMEM((nc, 4 * SSM_PAIR_LANES), F32),
        ],
        compiler_params=_params(("parallel", "arbitrary")),
        name="s5_scan",
    )(ug, w_pair, m_all, z_pair, a_pair)
    return yg.reshape(bsz, G, nc, T, CH).transpose(0, 2, 3, 1, 4).reshape(bsz * seq, SSM_WIDTH)


def _mix_kernel(h_ref, o_ref, ys_ref, gl_ref, wglu_ref, bglu_ref, wa_ref, ws_ref, wo_ref, out_ref):
    yg = jax.nn.gelu(ys_ref[...])
    z = jnp.dot(yg.astype(BF16), wglu_ref[...], preferred_element_type=F32) + bglu_ref[...]
    ysg = (yg * _sigmoid(z)).astype(BF16)
    br_a = jnp.dot(o_ref[...], wa_ref[...], preferred_element_type=F32)
    br_s = jnp.dot(ysg, ws_ref[...], preferred_element_type=F32)
    gl = gl_ref[...].astype(F32)
    mix = _sigmoid(gl[:, :D_MODEL]) * br_a + _sigmoid(gl[:, D_MODEL:]) * br_s
    out_ref[...] = h_ref[...] + jnp.dot(mix.astype(BF16), wo_ref[...], preferred_element_type=F32)


def _mix(h, o, ys, gl, w_glu, b_glu, w_attn_br, w_ssm_br, w_out):
    n = h.shape[0]
    tm = _pick_tile(n, 512)
    row = lambda i: (i, 0)
    fixed = lambda i: (0, 0)
    return pl.pallas_call(
        _mix_kernel,
        grid=(n // tm,),
        in_specs=[
            pl.BlockSpec((tm, D_MODEL), row),
            pl.BlockSpec((tm, ATT_WIDTH), row),
            pl.BlockSpec((tm, SSM_WIDTH), row),
            pl.BlockSpec((tm, 2 * D_MODEL), row),
            pl.BlockSpec((SSM_WIDTH, SSM_WIDTH), fixed),
            pl.BlockSpec((1, SSM_WIDTH), fixed),
            pl.BlockSpec((ATT_WIDTH, D_MODEL), fixed),
            pl.BlockSpec((SSM_WIDTH, D_MODEL), fixed),
            pl.BlockSpec((D_MODEL, D_MODEL), fixed),
        ],
        out_specs=pl.BlockSpec((tm, D_MODEL), row),
        out_shape=jax.ShapeDtypeStruct((n, D_MODEL), F32),
        compiler_params=_params(("parallel",)),
        name="branch_mix",
    )(h, o, ys, gl, w_glu, b_glu.reshape(1, SSM_WIDTH), w_attn_br, w_ssm_br, w_out)


def _lane_max(x):
    return jnp.max(x, axis=-1, keepdims=True)


def _router(u, wr_ref, br_ref):
    logits = jnp.dot(u, wr_ref[...], preferred_element_type=F32) + br_ref[...]
    lane = lax.broadcasted_iota(jnp.int32, logits.shape, 1).astype(F32)
    far = float(4 * LANES)
    is_grp = (lane >= N_EXPERTS) & (lane < N_EXPERTS + N_GROUPS)
    lg = jnp.where(is_grp, logits, NEG_BIG)
    g_max = _lane_max(lg)
    p_sel = 1.0 / jnp.sum(jnp.where(is_grp, jnp.exp(lg - g_max), 0.0), axis=-1, keepdims=True)
    g_lo = (jnp.min(jnp.where(lg == g_max, lane, far), axis=-1, keepdims=True) - N_EXPERTS) * EXPERTS_PER_GROUP
    in_grp = (lane >= g_lo) & (lane < g_lo + EXPERTS_PER_GROUP)
    l2 = jnp.where(in_grp, logits, NEG_BIG)
    m1 = _lane_max(l2)
    i1 = jnp.min(jnp.where(l2 == m1, lane, far), axis=-1, keepdims=True)
    l2b = jnp.where(lane == i1, NEG_BIG, l2)
    m2 = _lane_max(l2b)
    i2 = jnp.min(jnp.where(l2b == m2, lane, far), axis=-1, keepdims=True)
    e2 = jnp.exp(m2 - m1)
    w1 = 1.0 / (1.0 + e2)
    w2 = e2 * w1
    return p_sel * (jnp.where(lane == i1, w1, 0.0) + jnp.where(lane == i2, w2, 0.0))


def _moe_kernel(h_ref, g_ref, wr_ref, br_ref, w13_ref, w2_ref, out_ref, t_ref, gate_ref, acc_ref):
    e = pl.program_id(1)

    @pl.when(e == 0)
    def _():
        u = _rms(h_ref[...], g_ref[...]).astype(BF16)
        t_ref[...] = u
        gate_ref[...] = _router(u, wr_ref, br_ref)
        acc_ref[...] = jnp.zeros(acc_ref.shape, F32)

    ab = jnp.dot(t_ref[...], w13_ref[0], preferred_element_type=F32)
    a = ab[:, :EXPERT_FF]
    hdn = a * _sigmoid(a) * ab[:, EXPERT_FF:]
    gates = gate_ref[...]
    lane = lax.broadcasted_iota(jnp.int32, gates.shape, 1)
    ge = jnp.sum(jnp.where(lane == e, gates, 0.0), axis=-1, keepdims=True)
    acc_ref[...] += jnp.dot((hdn * ge).astype(BF16), w2_ref[0], preferred_element_type=F32)

    @pl.when(e == N_EXPERTS - 1)
    def _():
        out_ref[...] = h_ref[...] + acc_ref[...]


def _moe(h, g_ffn, w_router, b_router, w13, w2):
    n = h.shape[0]
    tm = _pick_tile(n, 1024)
    row = lambda i, e: (i, 0)
    fixed = lambda i, e: (0, 0)
    return pl.pallas_call(
        _moe_kernel,
        grid=(n // tm, N_EXPERTS),
        in_specs=[
            pl.BlockSpec((tm, D_MODEL), row),
            pl.BlockSpec((1, D_MODEL), fixed),
            pl.BlockSpec((D_MODEL, LANES), fixed),
            pl.BlockSpec((1, LANES), fixed),
            pl.BlockSpec((1, D_MODEL, 2 * EXPERT_FF), lambda i, e: (e, 0, 0)),
            pl.BlockSpec((1, EXPERT_FF, D_MODEL), lambda i, e: (e, 0, 0)),
        ],
        out_specs=pl.BlockSpec((tm, D_MODEL), row),
        out_shape=jax.ShapeDtypeStruct((n, D_MODEL), F32),
        scratch_shapes=[
            pltpu.VMEM((tm, D_MODEL), BF16),
            pltpu.VMEM((tm, LANES), F32),
            pltpu.VMEM((tm, D_MODEL), F32),
        ],
        compiler_params=_params(("parallel", "arbitrary")),
        name="hier_moe",
    )(h, g_ffn.reshape(1, D_MODEL), w_router, b_router, w13, w2)


def _ple_kernel(h_ref, p_ref, gp_ref, wg_ref, wp_ref, gf_ref, out_ref):
    h = h_ref[...]
    gate = _sigmoid(jnp.dot(_rms(h, gp_ref[...]).astype(BF16), wg_ref[...], preferred_element_type=F32))
    pe = jnp.dot(p_ref[...].astype(BF16), wp_ref[...], preferred_element_type=F32)
    out_ref[...] = _rms(h + gate * pe, gf_ref[...])


def _ple(h, p, g_ple, w_ple_gate, w_ple, g_final):
    n = h.shape[0]
    tm = _pick_tile(n, 512)
    row = lambda i: (i, 0)
    fixed = lambda i: (0, 0)
    return pl.pallas_call(
        _ple_kernel,
        grid=(n // tm,),
        in_specs=[
            pl.BlockSpec((tm, D_MODEL), row),
            pl.BlockSpec((tm, PLE_DIM), row),
            pl.BlockSpec((1, D_MODEL), fixed),
            pl.BlockSpec((D_MODEL, D_MODEL), fixed),
            pl.BlockSpec((PLE_DIM, D_MODEL), fixed),
            pl.BlockSpec((1, D_MODEL), fixed),
        ],
        out_specs=pl.BlockSpec((tm, D_MODEL), row),
        out_shape=jax.ShapeDtypeStruct((n, D_MODEL), F32),
        compiler_params=_params(("parallel",)),
        name="ple_final",
    )(h, p, g_ple.reshape(1, D_MODEL), w_ple_gate, w_ple, g_final.reshape(1, D_MODEL))


def _prep_weights(g_mix, w_in, lam_q1, lam_k1, lam_q2, lam_k2, subln_g,
                  ssm_lam_re, ssm_lam_im, ssm_log_dt, ssm_b_re, ssm_b_im, ssm_c_re, ssm_c_im, ssm_d,
                  w_glu, b_glu, w_attn_br, w_ssm_br, w_out, g_ffn, w_r1, b_r1, w_r2, b_r2,
                  w_e1, w_e3, w_e2, g_ple, w_ple_gate, w_ple, g_final):
    i = 0
    lam = (jnp.exp(jnp.sum(lam_q1[i] * lam_k1[i])) - jnp.exp(jnp.sum(lam_q2[i] * lam_k2[i])) + LAM_INIT)
    pad = LANES - N_EXPERTS - N_GROUPS
    w_router = jnp.concatenate([w_r2[i], w_r1[i], jnp.zeros((D_MODEL, pad), F32)], axis=1).astype(BF16)
    b_router = jnp.concatenate([b_r2[i], b_r1[i], jnp.zeros((pad,), F32)]).reshape(1, LANES)
    return dict(
        g_mix=g_mix[i], w_in=w_in[i].astype(BF16), lam=lam.astype(F32), subln_g=subln_g[i],
        ssm=_ssm_weights(ssm_lam_re[i], ssm_lam_im[i], ssm_log_dt[i], ssm_b_re[i], ssm_b_im[i],
                         ssm_c_re[i], ssm_c_im[i], ssm_d[i]),
        w_glu=w_glu[i].astype(BF16), b_glu=b_glu[i], w_attn_br=w_attn_br[i].astype(BF16),
        w_ssm_br=w_ssm_br[i].astype(BF16), w_out=w_out[i].astype(BF16), g_ffn=g_ffn[i],
        w_router=w_router, b_router=b_router,
        w13=jnp.concatenate([w_e1[i], w_e3[i]], axis=-1).astype(BF16), w2=w_e2[i].astype(BF16),
        g_ple=g_ple[i], w_ple_gate=w_ple_gate[i].astype(BF16), w_ple=w_ple[i].astype(BF16), g_final=g_final,
    )


def _trunk(x, p, w):
    bsz, seq, _ = x.shape
    n = bsz * seq
    x2 = x.reshape(n, D_MODEL)
    q, k, v, s_in, gl = _in_proj(x2, seq, w["g_mix"], w["w_in"])
    shp = (bsz, seq, ATT_WIDTH)
    o = _attention(q.reshape(shp), k.reshape(shp), v.reshape(shp), w["lam"], w["subln_g"])
    ys = _ssm(s_in, bsz, seq, w["ssm"])
    h = _mix(x2, o.reshape(n, ATT_WIDTH), ys, gl, w["w_glu"], w["b_glu"], w["w_attn_br"], w["w_ssm_br"], w["w_out"])
    h = _moe(h, w["g_ffn"], w["w_router"], w["b_router"], w["w13"], w["w2"])
    y = _ple(h, p[0].reshape(n, PLE_DIM), w["g_ple"], w["w_ple_gate"], w["w_ple"], w["g_final"])
    return y.reshape(bsz, seq, D_MODEL)


def kernel(x_prompt, x_sample, p_prompt, p_sample, g_mix, w_in, lam_q1, lam_k1, lam_q2, lam_k2, subln_g, ssm_lam_re, ssm_lam_im, ssm_log_dt, ssm_b_re, ssm_b_im, ssm_c_re, ssm_c_im, ssm_d, w_glu, b_glu, w_attn_br, w_ssm_br, w_out, g_ffn, w_r1, b_r1, w_r2, b_r2, w_e1, w_e3, w_e2, g_ple, w_ple_gate, w_ple, g_final):
    w = _prep_weights(g_mix, w_in, lam_q1, lam_k1, lam_q2, lam_k2, subln_g,
                      ssm_lam_re, ssm_lam_im, ssm_log_dt, ssm_b_re, ssm_b_im, ssm_c_re, ssm_c_im, ssm_d,
                      w_glu, b_glu, w_attn_br, w_ssm_br, w_out, g_ffn, w_r1, b_r1, w_r2, b_r2,
                      w_e1, w_e3, w_e2, g_ple, w_ple_gate, w_ple, g_final)
    return (_trunk(x_prompt, p_prompt, w), _trunk(x_sample, p_sample, w))
```

```python
import functools
import math

import jax
import jax.numpy as jnp
from jax import lax
from jax.experimental import pallas as pl
from jax.experimental.pallas import tpu as pltpu

F32 = jnp.float32
BF16 = jnp.bfloat16
HIGHEST = lax.Precision.HIGHEST

D_MODEL = 1024
N_HEADS = 8
HEAD_DIM = 64
ATT_V_DIM = 2 * HEAD_DIM
ATT_WIDTH = N_HEADS * ATT_V_DIM
ROT_DIM = HEAD_DIM // 4
ROPE_THETA = 500000.0
SSM_CH = 16
SSM_GROUPS = 32
SSM_WIDTH = SSM_GROUPS * SSM_CH
SSM_STATE = 64
N_GROUPS = 4
EXPERTS_PER_GROUP = 8
N_EXPERTS = N_GROUPS * EXPERTS_PER_GROUP
EXPERT_FF = 256
PLE_DIM = 256
EPS = 1e-6
LAM_INIT = 0.8 - 0.6 * math.exp(-0.3 * 0)

LANES = 128
SUBLANES = 8
SSM_CHUNK = 32
SSM_PAIR_LANES = 2 * SSM_STATE
VMEM_LIMIT = 56 * 1024 * 1024
NEG_BIG = -1e30
LOG2E = math.log2(math.e)
ATTN_ROW_BLOCK = 64
ATTN_OVERFLOW_MARGIN = 100.0
ATTN_SHORT_SEQ = 2048
ATTN_TILES_SHORT = (2048, 512)
ATTN_TILES_LONG = (1024, 1024)


def _params(sem):
    return pltpu.CompilerParams(dimension_semantics=sem, vmem_limit_bytes=VMEM_LIMIT)


def _rms(x, g):
    return x * lax.rsqrt(jnp.mean(x * x, axis=-1, keepdims=True) + EPS) * g


def _sigmoid(x):
    return 1.0 / (1.0 + jnp.exp(-x))


def _pick_tile(n, pref):
    t = min(n, pref)
    while n % t:
        t //= 2
    return t


def _in_proj_kernel(x_ref, g_ref, w_ref, cos_ref, sa_ref, sb_ref,
                    q_ref, k_ref, v_ref, s_ref, gl_ref):
    u = _rms(x_ref[...], g_ref[...]).astype(BF16)
    cos = cos_ref[...]
    sa = sa_ref[...]
    sb = sb_ref[...]
    for sec, out_ref, scale in ((0, q_ref, HEAD_DIM ** -0.5 * LOG2E), (1, k_ref, 1.0)):
        y = jnp.dot(u, w_ref[:, sec * ATT_WIDTH:(sec + 1) * ATT_WIDTH], preferred_element_type=F32)
        for h in range(N_HEADS):
            yh = y[:, h * LANES:(h + 1) * LANES]
            r = yh * cos + pltpu.roll(yh, LANES - ROT_DIM // 2, 1) * sa + pltpu.roll(yh, ROT_DIM // 2, 1) * sb
            out_ref[:, h * LANES:(h + 1) * LANES] = (r * scale).astype(BF16)
    c0 = 2 * ATT_WIDTH
    v_ref[...] = jnp.dot(u, w_ref[:, c0:c0 + ATT_WIDTH], preferred_element_type=F32).astype(BF16)
    c0 += ATT_WIDTH
    s_ref[...] = jnp.dot(u, w_ref[:, c0:c0 + SSM_WIDTH], preferred_element_type=F32).astype(BF16)
    c0 += SSM_WIDTH
    gl_ref[...] = jnp.dot(u, w_ref[:, c0:c0 + 2 * D_MODEL], preferred_element_type=F32).astype(BF16)


def _rope_tables(seq):
    half = ROT_DIM // 2
    inv = ROPE_THETA ** (-jnp.arange(0, ROT_DIM, 2, dtype=F32) / ROT_DIM)
    ang = jnp.arange(seq, dtype=F32)[:, None] * inv[None, :]
    cos, sin = jnp.cos(ang), jnp.sin(ang)
    one = jnp.ones((seq, HEAD_DIM - ROT_DIM), F32)
    zero = jnp.zeros((seq, HEAD_DIM - ROT_DIM), F32)
    zh = jnp.zeros((seq, half), F32)
    cos_t = jnp.concatenate([cos, cos, one], axis=1)
    sa_t = jnp.concatenate([-sin, zh, zero], axis=1)
    sb_t = jnp.concatenate([zh, sin, zero], axis=1)
    tile2 = lambda t: jnp.concatenate([t, t], axis=1)
    return tile2(cos_t), tile2(sa_t), tile2(sb_t)


def _in_proj(x2, seq, g_mix, w_in):
    n = x2.shape[0]
    tm = _pick_tile(seq, 512)
    nl = seq // tm
    in_cols = w_in.shape[1]
    cos_t, sa_t, sb_t = _rope_tables(seq)
    row = lambda i: (i, 0)
    fixed = lambda i: (0, 0)
    tab = lambda i: (i % nl, 0)
    return pl.pallas_call(
        _in_proj_kernel,
        grid=(n // tm,),
        in_specs=[
            pl.BlockSpec((tm, D_MODEL), row),
            pl.BlockSpec((1, D_MODEL), fixed),
            pl.BlockSpec((D_MODEL, in_cols), fixed),
            pl.BlockSpec((tm, LANES), tab),
            pl.BlockSpec((tm, LANES), tab),
            pl.BlockSpec((tm, LANES), tab),
        ],
        out_specs=[
            pl.BlockSpec((tm, ATT_WIDTH), row),
            pl.BlockSpec((tm, ATT_WIDTH), row),
            pl.BlockSpec((tm, ATT_WIDTH), row),
            pl.BlockSpec((tm, SSM_WIDTH), row),
            pl.BlockSpec((tm, 2 * D_MODEL), row),
        ],
        out_shape=[
            jax.ShapeDtypeStruct((n, ATT_WIDTH), BF16),
            jax.ShapeDtypeStruct((n, ATT_WIDTH), BF16),
            jax.ShapeDtypeStruct((n, ATT_WIDTH), BF16),
            jax.ShapeDtypeStruct((n, SSM_WIDTH), BF16),
            jax.ShapeDtypeStruct((n, 2 * D_MODEL), BF16),
        ],
        compiler_params=_params(("parallel",)),
        name="in_proj",
    )(x2, g_mix.reshape(1, D_MODEL), w_in, cos_t, sa_t, sb_t)


def _col_reduce(x, pair_op, reduce_op):
    r = x.shape[0]
    rb = min(ATTN_ROW_BLOCK, r)
    out = None
    for i in range(r // rb):
        blk = x[i * rb:(i + 1) * rb]
        n = rb
        while n > SUBLANES:
            n //= 2
            blk = pair_op(blk[:n], blk[n:])
        out = blk if out is None else pair_op(out, blk)
    return reduce_op(out, axis=0, keepdims=True)


def _attn_kernel(lam_ref, q_ref, k_ref, v_ref, g_ref, o_ref, vt_ref, r_ref, l_ref, acc_ref, p_ref, be_ref, fl_ref,
                 *, tq, tk, nk):
    qi = pl.program_id(2)

    @pl.when(qi == 0)
    def _():
        def tr(j, c):
            off = pl.multiple_of(j * tk, tk)
            vt_ref[:, pl.ds(off, tk)] = v_ref[0, pl.ds(off, tk), :].astype(F32).T.astype(BF16)
            return c
        lax.fori_loop(0, nk, tr, 0)

    qt = q_ref[0].astype(F32).T
    row = lax.broadcasted_iota(jnp.int32, qt.shape, 0)
    qm = jnp.concatenate([jnp.where(row < HEAD_DIM, qt, 0.0),
                          jnp.where(row >= HEAD_DIM, qt, 0.0)], axis=1).astype(BF16)

    def scores(j):
        off = pl.multiple_of(j * tk, tk)
        return jnp.dot(k_ref[0, pl.ds(off, tk), :], qm, preferred_element_type=F32)

    def values(j, slot):
        off = pl.multiple_of(j * tk, tk)
        pv = jnp.dot(vt_ref[:, pl.ds(off, tk)], p_ref[slot], preferred_element_type=F32)
        acc_ref[...] = be_ref[slot] * (acc_ref[...] + pv)

    s = scores(0)
    r0 = _col_reduce(s, jnp.maximum, jnp.max)
    p = jnp.exp2(s - r0)
    r_ref[...] = r0
    l_ref[...] = _col_reduce(p, jnp.add, jnp.sum)
    p_ref[0] = p.astype(BF16)
    be_ref[0] = jnp.ones(be_ref.shape[1:], F32)
    fl_ref[...] = jnp.zeros(fl_ref.shape, F32)
    acc_ref[...] = jnp.zeros(acc_ref.shape, F32)

    def step(j, slot):
        s = scores(j)
        ref = r_ref[...]
        p = jnp.exp2(s - ref)
        cm = _col_reduce(s, jnp.maximum, jnp.max)
        new = jnp.maximum(ref, cm)
        beta = jnp.exp2(ref - new)
        fl_ref[...] = jnp.maximum(fl_ref[...], cm - ref)
        l_ref[...] = beta * (l_ref[...] + _col_reduce(p, jnp.add, jnp.sum))
        r_ref[...] = new
        be_ref[slot] = beta
        p_ref[slot] = p.astype(BF16)
        values(j - 1, 1 - slot)

    if nk > 1:
        def body(i, c):
            step(2 * i + 1, 1)
            step(2 * i + 2, 0)
            return c
        lax.fori_loop(0, (nk - 2) // 2, body, 0)
        step(nk - 1, 1)
    values(nk - 1, (nk - 1) % 2)

    @pl.when(jnp.max(fl_ref[...]) > ATTN_OVERFLOW_MARGIN)
    def _():
        r_ref[...] = jnp.full(r_ref.shape, NEG_BIG, F32)
        l_ref[...] = jnp.zeros(l_ref.shape, F32)
        acc_ref[...] = jnp.zeros(acc_ref.shape, F32)

        def safe(j, c):
            s = scores(j)
            m_old = r_ref[...]
            m_new = jnp.maximum(m_old, _col_reduce(s, jnp.maximum, jnp.max))
            alpha = jnp.exp2(m_old - m_new)
            p = jnp.exp2(s - m_new)
            l_ref[...] = alpha * l_ref[...] + _col_reduce(p, jnp.add, jnp.sum)
            r_ref[...] = m_new
            off = pl.multiple_of(j * tk, tk)
            pv = jnp.dot(vt_ref[:, pl.ds(off, tk)], p.astype(BF16), preferred_element_type=F32)
            acc_ref[...] = alpha * acc_ref[...] + pv
            return c
        lax.fori_loop(0, nk, safe, 0)

    lam = lam_ref[0]
    o_all = acc_ref[...] * (1.0 / l_ref[...])
    o = o_all[:, :tq] - lam * o_all[:, tq:]
    o = o * lax.rsqrt(jnp.mean(o * o, axis=0, keepdims=True) + EPS) * g_ref[...]
    o_ref[0] = (o * (1.0 - LAM_INIT)).T.astype(BF16)


def _attention(q, k, v, lam, subln_g):
    bsz, seq, _ = q.shape
    tq_pref, tk_pref = ATTN_TILES_SHORT if seq <= ATTN_SHORT_SEQ else ATTN_TILES_LONG
    tq = _pick_tile(seq, tq_pref)
    tk = _pick_tile(seq, tk_pref)
    nk = seq // tk
    assert nk == 1 or nk % 2 == 0
    kern = functools.partial(_attn_kernel, tq=tq, tk=tk, nk=nk)
    return pl.pallas_call(
        kern,
        grid=(bsz, N_HEADS, seq // tq),
        in_specs=[
            pl.BlockSpec(memory_space=pltpu.SMEM),
            pl.BlockSpec((1, tq, LANES), lambda b, h, i: (b, i, h)),
            pl.BlockSpec((1, seq, LANES), lambda b, h, i: (b, 0, h)),
            pl.BlockSpec((1, seq, LANES), lambda b, h, i: (b, 0, h)),
            pl.BlockSpec((ATT_V_DIM, 1), lambda b, h, i: (0, 0)),
        ],
        out_specs=pl.BlockSpec((1, tq, LANES), lambda b, h, i: (b, i, h)),
        out_shape=jax.ShapeDtypeStruct((bsz, seq, ATT_WIDTH), BF16),
        scratch_shapes=[
            pltpu.VMEM((ATT_V_DIM, seq), BF16),
            pltpu.VMEM((1, 2 * tq), F32),
            pltpu.VMEM((1, 2 * tq), F32),
            pltpu.VMEM((ATT_V_DIM, 2 * tq), F32),
            pltpu.VMEM((2, tk, 2 * tq), BF16),
            pltpu.VMEM((2, 1, 2 * tq), F32),
            pltpu.VMEM((1, 2 * tq), F32),
        ],
        compiler_params=_params(("parallel", "parallel", "arbitrary")),
        name="diff_attn",
    )(lam.reshape(1), q, k, v, subln_g.reshape(ATT_V_DIM, 1))


def _ssm_weights(lam_re, lam_im, log_dt, b_re, b_im, c_re, c_im, d_skip):
    T, G, P, CH = SSM_CHUNK, SSM_GROUPS, SSM_STATE, SSM_CH
    n = jnp.arange(T + 1, dtype=F32)[:, None, None]
    pw_re, pw_im, bb_re, bb_im = [], [], [], []
    for dirn in range(2):
        dt = jnp.exp(log_dt[dirn])[:, None]
        lr, li = lam_re[dirn], lam_im[dirn]
        mag = jnp.exp(lr * dt)
        a_re, a_im = mag * jnp.cos(li * dt), mag * jnp.sin(li * dt)
        den = lr * lr + li * li
        n_re = a_re - 1.0
        q_re = (n_re * lr + a_im * li) / den
        q_im = (a_im * lr - n_re * li) / den
        bb_re.append(q_re[..., None] * b_re[dirn] - q_im[..., None] * b_im[dirn])
        bb_im.append(q_re[..., None] * b_im[dirn] + q_im[..., None] * b_re[dirn])
        magn = jnp.exp(n * (lr * dt)[None])
        pw_re.append(magn * jnp.cos(n * (li * dt)[None]))
        pw_im.append(magn * jnp.sin(n * (li * dt)[None]))

    def cmul(ar, ai, br, bi):
        return ar * br - ai * bi, ar * bi + ai * br

    kern = []
    for dirn in range(2):
        car, cai = cmul(c_re[dirn][None], c_im[dirn][None],
                        pw_re[dirn][:T, :, None, :], pw_im[dirn][:T, :, None, :])
        kern.append(jnp.einsum('dgcp,gpk->dgck', car, bb_re[dirn], precision=HIGHEST)
                    - jnp.einsum('dgcp,gpk->dgck', cai, bb_im[dirn], precision=HIGHEST))
    tj = jnp.arange(T)
    lag = tj[None, :] - tj[:, None]
    mf = jnp.where((lag >= 0)[:, :, None, None, None], kern[0][jnp.clip(lag, 0, T - 1)], 0.0)
    mb = jnp.where((lag <= 0)[:, :, None, None, None], kern[1][jnp.clip(-lag, 0, T - 1)], 0.0)
    dsk = (lag == 0)[:, :, None, None, None] * (jnp.eye(CH, dtype=F32)[None, None, None] * d_skip[None, None, :, :, None])
    m_all = (mf + mb + dsk).transpose(2, 0, 4, 1, 3).reshape(G, T * CH, T * CH)

    wf_re, wf_im = cmul(pw_re[0][T - 1 - tj][:, :, :, None], pw_im[0][T - 1 - tj][:, :, :, None],
                        bb_re[0][None], bb_im[0][None])
    wb_re, wb_im = cmul(pw_re[1][tj][:, :, :, None], pw_im[1][tj][:, :, :, None], bb_re[1][None], bb_im[1][None])
    w_all = jnp.stack([wf_re, wf_im, wb_re, wb_im], axis=0).transpose(2, 1, 4, 0, 3)
    w_all = w_all.reshape(G // 2, 2, T * CH, 4, P)
    eye2 = jnp.eye(2, dtype=F32)
    w_pair = (w_all[:, :, :, :, None, :] * eye2[None, :, None, None, :, None]).reshape(G // 2, 2 * T * CH, 4 * 2 * P)

    zf_re, zf_im = cmul(c_re[0][None], c_im[0][None], pw_re[0][1:T + 1, :, None, :], pw_im[0][1:T + 1, :, None, :])
    zb_re, zb_im = cmul(c_re[1][None], c_im[1][None],
                        pw_re[1][T - tj][:, :, None, :], pw_im[1][T - tj][:, :, None, :])
    z_all = jnp.stack([zf_re, -zf_im, zb_re, -zb_im], axis=0).transpose(2, 0, 4, 1, 3)
    z_all = z_all.reshape(G // 2, 2, 4, P, T * CH)
    z_pair = (z_all.transpose(0, 2, 1, 3, 4)[:, :, :, :, None, :] * eye2[None, None, :, None, :, None])
    z_pair = z_pair.reshape(G // 2, 4 * 2 * P, 2 * T * CH)

    a_tab = []
    for dirn in range(2):
        dt = jnp.exp(log_dt[dirn])[:, None]
        order = jnp.arange(SUBLANES) if dirn == 0 else SUBLANES - 1 - jnp.arange(SUBLANES)
        nn = jnp.concatenate([order, jnp.array([1, 2, 4, 8])]).astype(F32)[:, None, None] * T
        magn = jnp.exp(nn * (lam_re[dirn] * dt)[None])
        ang = nn * (lam_im[dirn] * dt)[None]
        tab = jnp.stack([magn * jnp.cos(ang), magn * jnp.sin(ang)], axis=0)
        tab = jnp.pad(tab, ((0, 0), (0, 2 * SUBLANES - tab.shape[1]), (0, 0), (0, 0)))
        a_tab.append(tab.reshape(2, 2 * SUBLANES, G // 2, 2 * P).transpose(2, 0, 1, 3))
    a_tab = jnp.stack(a_tab, axis=1)
    return m_all.astype(BF16), w_pair.astype(BF16), z_pair.astype(BF16), a_tab


def _tile_scan(x_re, x_im, a_tab, reverse):
    row = lax.broadcasted_iota(jnp.int32, x_re.shape, 0)
    for lvl, d in enumerate((1, 2, 4)):
        ar = a_tab[0][SUBLANES + lvl:SUBLANES + lvl + 1, :]
        ai = a_tab[1][SUBLANES + lvl:SUBLANES + lvl + 1, :]
        if reverse:
            keep = row < SUBLANES - d
            shift = SUBLANES - d
        else:
            keep = row >= d
            shift = d
        sr = jnp.where(keep, pltpu.roll(x_re, shift, 0), 0.0)
        si = jnp.where(keep, pltpu.roll(x_im, shift, 0), 0.0)
        x_re, x_im = x_re + ar * sr - ai * si, x_im + ar * si + ai * sr
    return x_re, x_im


def _ssm_kernel(u_ref, w_ref, m_ref, z_ref, a_ref, y_ref, v_ref, s_ref, *, nc):
    pl_ = SSM_PAIR_LANES
    nt = nc // SUBLANES
    u0 = u_ref[0, 0]
    u1 = u_ref[0, 1]
    v_ref[...] = jnp.dot(jnp.concatenate([u0, u1], axis=1), w_ref[0], preferred_element_type=F32)

    row = lax.broadcasted_iota(jnp.int32, (SUBLANES, pl_), 0)

    def tile_step(i, carry):
        new = []
        for dirn in range(2):
            c_re, c_im = carry[2 * dirn], carry[2 * dirn + 1]
            a_tab = (a_ref[0, dirn, 0], a_ref[0, dirn, 1])
            t = i if dirn == 0 else nt - 1 - i
            r0 = pl.multiple_of(t * SUBLANES, SUBLANES)
            lo = 2 * dirn * pl_
            x_re, x_im = _tile_scan(v_ref[pl.ds(r0, SUBLANES), lo:lo + pl_],
                                    v_ref[pl.ds(r0, SUBLANES), lo + pl_:lo + 2 * pl_], a_tab, dirn == 1)
            if dirn == 0:
                keep, shift, last = row >= 1, 1, SUBLANES - 1
            else:
                keep, shift, last = row < SUBLANES - 1, SUBLANES - 1, 0
            pr, pi = a_tab[0][0:SUBLANES, :], a_tab[1][0:SUBLANES, :]
            s_ref[pl.ds(r0, SUBLANES), lo:lo + pl_] = (
                pr * c_re - pi * c_im + jnp.where(keep, pltpu.roll(x_re, shift, 0), 0.0))
            s_ref[pl.ds(r0, SUBLANES), lo + pl_:lo + 2 * pl_] = (
                pr * c_im + pi * c_re + jnp.where(keep, pltpu.roll(x_im, shift, 0), 0.0))
            a8r = a_tab[0][SUBLANES + 3:SUBLANES + 4, :]
            a8i = a_tab[1][SUBLANES + 3:SUBLANES + 4, :]
            new.append(a8r * c_re - a8i * c_im + x_re[last:last + 1, :])
            new.append(a8r * c_im + a8i * c_re + x_im[last:last + 1, :])
        return tuple(new)

    zero = jnp.zeros((1, pl_), F32)
    lax.fori_loop(0, nt, tile_step, (zero, zero, zero, zero))

    ys = jnp.dot(s_ref[...].astype(BF16), z_ref[0], preferred_element_type=F32)
    half = ys.shape[1] // 2
    y_ref[0, 0] = jax.nn.gelu(ys[:, :half] + jnp.dot(u0, m_ref[0], preferred_element_type=F32)).astype(BF16)
    y_ref[0, 1] = jax.nn.gelu(ys[:, half:] + jnp.dot(u1, m_ref[1], preferred_element_type=F32)).astype(BF16)


def _ssm(s_in, bsz, seq, weights):
    m_all, w_pair, z_pair, a_pair = weights
    T, G, CH = SSM_CHUNK, SSM_GROUPS, SSM_CH
    nc = seq // T
    tc = T * CH
    ug = s_in.reshape(bsz, nc, T, G, CH).transpose(0, 3, 1, 2, 4).reshape(bsz, G, nc, tc)
    kern = functools.partial(_ssm_kernel, nc=nc)
    yg = pl.pallas_call(
        kern,
        grid=(G // 2, bsz),
        in_specs=[
            pl.BlockSpec((1, 2, nc, tc), lambda g, b: (b, g, 0, 0)),
            pl.BlockSpec((1, 2 * tc, 4 * SSM_PAIR_LANES), lambda g, b: (g, 0, 0)),
            pl.BlockSpec((2, tc, tc), lambda g, b: (g, 0, 0)),
            pl.BlockSpec((1, 4 * SSM_PAIR_LANES, 2 * tc), lambda g, b: (g, 0, 0)),
            pl.BlockSpec((1, 2, 2, 2 * SUBLANES, SSM_PAIR_LANES), lambda g, b: (g, 0, 0, 0, 0)),
        ],
        out_specs=pl.BlockSpec((1, 2, nc, tc), lambda g, b: (b, g, 0, 0)),
        out_shape=jax.ShapeDtypeStruct((bsz, G, nc, tc), BF16),
        scratch_shapes=[
            pltpu.VMEM((nc, 4 * SSM_PAIR_LANES), F32),
            pltpu.VMEM((nc, 4 * SSM_PAIR_LANES), F32),
        ],
        compiler_params=_params(("parallel", "arbitrary")),
        name="s5_scan",
    )(ug, w_pair, m_all, z_pair, a_pair)
    return yg.reshape(bsz, G, nc, T, CH).transpose(0, 2, 3, 1, 4).reshape(bsz * seq, SSM_WIDTH)


def _mix_kernel(h_ref, o_ref, yg_ref, gl_ref, wglu_ref, bglu_ref, wa_ref, ws_ref, wo_ref, out_ref):
    yg = yg_ref[...]
    z = jnp.dot(yg, wglu_ref[...], preferred_element_type=F32) + bglu_ref[...]
    ysg = (yg.astype(F32) * _sigmoid(z)).astype(BF16)
    br_a = jnp.dot(o_ref[...], wa_ref[...], preferred_element_type=F32)
    br_s = jnp.dot(ysg, ws_ref[...], preferred_element_type=F32)
    gl = gl_ref[...].astype(F32)
    mix = _sigmoid(gl[:, :D_MODEL]) * br_a + _sigmoid(gl[:, D_MODEL:]) * br_s
    out_ref[...] = h_ref[...] + jnp.dot(mix.astype(BF16), wo_ref[...], preferred_element_type=F32)


def _mix(h, o, ys, gl, w_glu, b_glu, w_attn_br, w_ssm_br, w_out):
    n = h.shape[0]
    tm = _pick_tile(n, 512)
    row = lambda i: (i, 0)
    fixed = lambda i: (0, 0)
    return pl.pallas_call(
        _mix_kernel,
        grid=(n // tm,),
        in_specs=[
            pl.BlockSpec((tm, D_MODEL), row),
            pl.BlockSpec((tm, ATT_WIDTH), row),
            pl.BlockSpec((tm, SSM_WIDTH), row),
            pl.BlockSpec((tm, 2 * D_MODEL), row),
            pl.BlockSpec((SSM_WIDTH, SSM_WIDTH), fixed),
            pl.BlockSpec((1, SSM_WIDTH), fixed),
            pl.BlockSpec((ATT_WIDTH, D_MODEL), fixed),
            pl.BlockSpec((SSM_WIDTH, D_MODEL), fixed),
            pl.BlockSpec((D_MODEL, D_MODEL), fixed),
        ],
        out_specs=pl.BlockSpec((tm, D_MODEL), row),
        out_shape=jax.ShapeDtypeStruct((n, D_MODEL), F32),
        compiler_params=_params(("parallel",)),
        name="branch_mix",
    )(h, o, ys, gl, w_glu, b_glu.reshape(1, SSM_WIDTH), w_attn_br, w_ssm_br, w_out)


def _lane_max(x):
    return jnp.max(x, axis=-1, keepdims=True)


def _router(u, wr_ref, br_ref):
    logits = jnp.dot(u, wr_ref[...], preferred_element_type=F32) + br_ref[...]
    lane = lax.broadcasted_iota(jnp.int32, logits.shape, 1).astype(F32)
    far = float(4 * LANES)
    is_grp = (lane >= N_EXPERTS) & (lane < N_EXPERTS + N_GROUPS)
    lg = jnp.where(is_grp, logits, NEG_BIG)
    g_max = _lane_max(lg)
    p_sel = 1.0 / jnp.sum(jnp.where(is_grp, jnp.exp(lg - g_max), 0.0), axis=-1, keepdims=True)
    g_lo = (jnp.min(jnp.where(lg == g_max, lane, far), axis=-1, keepdims=True) - N_EXPERTS) * EXPERTS_PER_GROUP
    in_grp = (lane >= g_lo) & (lane < g_lo + EXPERTS_PER_GROUP)
    l2 = jnp.where(in_grp, logits, NEG_BIG)
    m1 = _lane_max(l2)
    i1 = jnp.min(jnp.where(l2 == m1, lane, far), axis=-1, keepdims=True)
    l2b = jnp.where(lane == i1, NEG_BIG, l2)
    m2 = _lane_max(l2b)
    i2 = jnp.min(jnp.where(l2b == m2, lane, far), axis=-1, keepdims=True)
    e2 = jnp.exp(m2 - m1)
    w1 = 1.0 / (1.0 + e2)
    w2 = e2 * w1
    return p_sel * (jnp.where(lane == i1, w1, 0.0) + jnp.where(lane == i2, w2, 0.0))


def _moe_kernel(h_ref, g_ref, wr_ref, br_ref, w13_ref, w2_ref, out_ref, t_ref, gate_ref, acc_ref):
    e = pl.program_id(1)

    @pl.when(e == 0)
    def _():
        u = _rms(h_ref[...], g_ref[...]).astype(BF16)
        t_ref[...] = u
        gate_ref[...] = _router(u, wr_ref, br_ref)
        acc_ref[...] = jnp.zeros(acc_ref.shape, F32)

    ab = jnp.dot(t_ref[...], w13_ref[0], preferred_element_type=F32)
    a = ab[:, :EXPERT_FF]
    hdn = a * _sigmoid(a) * ab[:, EXPERT_FF:]
    gates = gate_ref[...]
    lane = lax.broadcasted_iota(jnp.int32, gates.shape, 1)
    ge = jnp.sum(jnp.where(lane == e, gates, 0.0), axis=-1, keepdims=True)
    acc_ref[...] += jnp.dot((hdn * ge).astype(BF16), w2_ref[0], preferred_element_type=F32)

    @pl.when(e == N_EXPERTS - 1)
    def _():
        out_ref[...] = h_ref[...] + acc_ref[...]


def _moe(h, g_ffn, w_router, b_router, w13, w2):
    n = h.shape[0]
    tm = _pick_tile(n, 1024)
    row = lambda i, e: (i, 0)
    fixed = lambda i, e: (0, 0)
    return pl.pallas_call(
        _moe_kernel,
        grid=(n // tm, N_EXPERTS),
        in_specs=[
            pl.BlockSpec((tm, D_MODEL), row),
            pl.BlockSpec((1, D_MODEL), fixed),
            pl.BlockSpec((D_MODEL, LANES), fixed),
            pl.BlockSpec((1, LANES), fixed),
            pl.BlockSpec((1, D_MODEL, 2 * EXPERT_FF), lambda i, e: (e, 0, 0)),
            pl.BlockSpec((1, EXPERT_FF, D_MODEL), lambda i, e: (e, 0, 0)),
        ],
        out_specs=pl.BlockSpec((tm, D_MODEL), row),
        out_shape=jax.ShapeDtypeStruct((n, D_MODEL), F32),
        scratch_shapes=[
            pltpu.VMEM((tm, D_MODEL), BF16),
            pltpu.VMEM((tm, LANES), F32),
            pltpu.VMEM((tm, D_MODEL), F32),
        ],
        compiler_params=_params(("parallel", "arbitrary")),
        name="hier_moe",
    )(h, g_ffn.reshape(1, D_MODEL), w_router, b_router, w13, w2)


def _ple_kernel(h_ref, p_ref, gp_ref, wg_ref, wp_ref, gf_ref, out_ref):
    h = h_ref[...]
    gate = _sigmoid(jnp.dot(_rms(h, gp_ref[...]).astype(BF16), wg_ref[...], preferred_element_type=F32))
    pe = jnp.dot(p_ref[...].astype(BF16), wp_ref[...], preferred_element_type=F32)
    out_ref[...] = _rms(h + gate * pe, gf_ref[...])


def _ple(h, p, g_ple, w_ple_gate, w_ple, g_final):
    n = h.shape[0]
    tm = _pick_tile(n, 512)
    row = lambda i: (i, 0)
    fixed = lambda i: (0, 0)
    return pl.pallas_call(
        _ple_kernel,
        grid=(n // tm,),
        in_specs=[
            pl.BlockSpec((tm, D_MODEL), row),
            pl.BlockSpec((tm, PLE_DIM), row),
            pl.BlockSpec((1, D_MODEL), fixed),
            pl.BlockSpec((D_MODEL, D_MODEL), fixed),
            pl.BlockSpec((PLE_DIM, D_MODEL), fixed),
            pl.BlockSpec((1, D_MODEL), fixed),
        ],
        out_specs=pl.BlockSpec((tm, D_MODEL), row),
        out_shape=jax.ShapeDtypeStruct((n, D_MODEL), F32),
        compiler_params=_params(("parallel",)),
        name="ple_final",
    )(h, p, g_ple.reshape(1, D_MODEL), w_ple_gate, w_ple, g_final.reshape(1, D_MODEL))


def _prep_weights(g_mix, w_in, lam_q1, lam_k1, lam_q2, lam_k2, subln_g,
                  ssm_lam_re, ssm_lam_im, ssm_log_dt, ssm_b_re, ssm_b_im, ssm_c_re, ssm_c_im, ssm_d,
                  w_glu, b_glu, w_attn_br, w_ssm_br, w_out, g_ffn, w_r1, b_r1, w_r2, b_r2,
                  w_e1, w_e3, w_e2, g_ple, w_ple_gate, w_ple, g_final):
    i = 0
    lam = (jnp.exp(jnp.sum(lam_q1[i] * lam_k1[i])) - jnp.exp(jnp.sum(lam_q2[i] * lam_k2[i])) + LAM_INIT)
    pad = LANES - N_EXPERTS - N_GROUPS
    w_router = jnp.concatenate([w_r2[i], w_r1[i], jnp.zeros((D_MODEL, pad), F32)], axis=1).astype(BF16)
    b_router = jnp.concatenate([b_r2[i], b_r1[i], jnp.zeros((pad,), F32)]).reshape(1, LANES)
    return dict(
        g_mix=g_mix[i], w_in=w_in[i].astype(BF16), lam=lam.astype(F32), subln_g=subln_g[i],
        ssm=_ssm_weights(ssm_lam_re[i], ssm_lam_im[i], ssm_log_dt[i], ssm_b_re[i], ssm_b_im[i],
                         ssm_c_re[i], ssm_c_im[i], ssm_d[i]),
        w_glu=w_glu[i].astype(BF16), b_glu=b_glu[i], w_attn_br=w_attn_br[i].astype(BF16),
        w_ssm_br=w_ssm_br[i].astype(BF16), w_out=w_out[i].astype(BF16), g_ffn=g_ffn[i],
        w_router=w_router, b_router=b_router,
        w13=jnp.concatenate([w_e1[i], w_e3[i]], axis=-1).astype(BF16), w2=w_e2[i].astype(BF16),
        g_ple=g_ple[i], w_ple_gate=w_ple_gate[i].astype(BF16), w_ple=w_ple[i].astype(BF16), g_final=g_final,
    )


def _trunk(x, p, w):
    bsz, seq, _ = x.shape
    n = bsz * seq
    x2 = x.reshape(n, D_MODEL)
    q, k, v, s_in, gl = _in_proj(x2, seq, w["g_mix"], w["w_in"])
    shp = (bsz, seq, ATT_WIDTH)
    o = _attention(q.reshape(shp), k.reshape(shp), v.reshape(shp), w["lam"], w["subln_g"])
    ys = _ssm(s_in, bsz, seq, w["ssm"])
    h = _mix(x2, o.reshape(n, ATT_WIDTH), ys, gl, w["w_glu"], w["b_glu"], w["w_attn_br"], w["w_ssm_br"], w["w_out"])
    h = _moe(h, w["g_ffn"], w["w_router"], w["b_router"], w["w13"], w["w2"])
    y = _ple(h, p[0].reshape(n, PLE_DIM), w["g_ple"], w["w_ple_gate"], w["w_ple"], w["g_final"])
    return y.reshape(bsz, seq, D_MODEL)


def kernel(x_prompt, x_sample, p_prompt, p_sample, g_mix, w_in, lam_q1, lam_k1, lam_q2, lam_k2, subln_g, ssm_lam_re, ssm_lam_im, ssm_log_dt, ssm_b_re, ssm_b_im, ssm_c_re, ssm_c_im, ssm_d, w_glu, b_glu, w_attn_br, w_ssm_br, w_out, g_ffn, w_r1, b_r1, w_r2, b_r2, w_e1, w_e3, w_e2, g_ple, w_ple_gate, w_ple, g_final):
    w = _prep_weights(g_mix, w_in, lam_q1, lam_k1, lam_q2, lam_k2, subln_g,
                      ssm_lam_re, ssm_lam_im, ssm_log_dt, ssm_b_re, ssm_b_im, ssm_c_re, ssm_c_im, ssm_d,
                      w_glu, b_glu, w_attn_br, w_ssm_br, w_out, g_ffn, w_r1, b_r1, w_r2, b_r2,
                      w_e1, w_e3, w_e2, g_ple, w_ple_gate, w_ple, g_final)
    return (_trunk(x_prompt, p_prompt, w), _trunk(x_sample, p_sample, w))
```

```python
import functools
import math

import jax
import jax.numpy as jnp
from jax import lax
from jax.experimental import pallas as pl
from jax.experimental.pallas import tpu as pltpu

F32 = jnp.float32
BF16 = jnp.bfloat16
HIGHEST = lax.Precision.HIGHEST

D_MODEL = 1024
N_HEADS = 8
HEAD_DIM = 64
ATT_V_DIM = 2 * HEAD_DIM
ATT_WIDTH = N_HEADS * ATT_V_DIM
ROT_DIM = HEAD_DIM // 4
ROPE_THETA = 500000.0
SSM_CH = 16
SSM_GROUPS = 32
SSM_WIDTH = SSM_GROUPS * SSM_CH
SSM_STATE = 64
N_GROUPS = 4
EXPERTS_PER_GROUP = 8
N_EXPERTS = N_GROUPS * EXPERTS_PER_GROUP
EXPERT_FF = 256
PLE_DIM = 256
EPS = 1e-6
LAM_INIT = 0.8 - 0.6 * math.exp(-0.3 * 0)

LANES = 128
SUBLANES = 8
SSM_CHUNK = 32
SSM_PAIR_LANES = 2 * SSM_STATE
VMEM_LIMIT = 56 * 1024 * 1024
NEG_BIG = -1e30
LOG2E = math.log2(math.e)
ATTN_ROW_BLOCK = 64
ATTN_OVERFLOW_MARGIN = 100.0
ATTN_SHORT_SEQ = 2048
ATTN_TILES_SHORT = (2048, 512)
ATTN_TILES_LONG = (1024, 1024)
MOE_WINDOW = 320
MOE_EXPERTS_PER_STEP = 4


def _params(sem):
    return pltpu.CompilerParams(dimension_semantics=sem, vmem_limit_bytes=VMEM_LIMIT)


def _rms(x, g):
    return x * lax.rsqrt(jnp.mean(x * x, axis=-1, keepdims=True) + EPS) * g


def _sigmoid(x):
    return 1.0 / (1.0 + jnp.exp(-x))


def _pick_tile(n, pref):
    t = min(n, pref)
    while n % t:
        t //= 2
    return t


def _in_proj_kernel(x_ref, g_ref, w_ref, cos_ref, sa_ref, sb_ref,
                    q_ref, k_ref, v_ref, s_ref, gl_ref):
    u = _rms(x_ref[...], g_ref[...]).astype(BF16)
    cos = cos_ref[...]
    sa = sa_ref[...]
    sb = sb_ref[...]
    for sec, out_ref, scale in ((0, q_ref, HEAD_DIM ** -0.5 * LOG2E), (1, k_ref, 1.0)):
        y = jnp.dot(u, w_ref[:, sec * ATT_WIDTH:(sec + 1) * ATT_WIDTH], preferred_element_type=F32)
        for h in range(N_HEADS):
            yh = y[:, h * LANES:(h + 1) * LANES]
            r = yh * cos + pltpu.roll(yh, LANES - ROT_DIM // 2, 1) * sa + pltpu.roll(yh, ROT_DIM // 2, 1) * sb
            out_ref[:, h * LANES:(h + 1) * LANES] = (r * scale).astype(BF16)
    c0 = 2 * ATT_WIDTH
    v_ref[...] = jnp.dot(u, w_ref[:, c0:c0 + ATT_WIDTH], preferred_element_type=F32).astype(BF16)
    c0 += ATT_WIDTH
    s_ref[...] = jnp.dot(u, w_ref[:, c0:c0 + SSM_WIDTH], preferred_element_type=F32).astype(BF16)
    c0 += SSM_WIDTH
    gl_ref[...] = jnp.dot(u, w_ref[:, c0:c0 + 2 * D_MODEL], preferred_element_type=F32).astype(BF16)


def _rope_tables(seq):
    half = ROT_DIM // 2
    inv = ROPE_THETA ** (-jnp.arange(0, ROT_DIM, 2, dtype=F32) / ROT_DIM)
    ang = jnp.arange(seq, dtype=F32)[:, None] * inv[None, :]
    cos, sin = jnp.cos(ang), jnp.sin(ang)
    one = jnp.ones((seq, HEAD_DIM - ROT_DIM), F32)
    zero = jnp.zeros((seq, HEAD_DIM - ROT_DIM), F32)
    zh = jnp.zeros((seq, half), F32)
    cos_t = jnp.concatenate([cos, cos, one], axis=1)
    sa_t = jnp.concatenate([-sin, zh, zero], axis=1)
    sb_t = jnp.concatenate([zh, sin, zero], axis=1)
    tile2 = lambda t: jnp.concatenate([t, t], axis=1)
    return tile2(cos_t), tile2(sa_t), tile2(sb_t)


def _in_proj(x2, seq, g_mix, w_in):
    n = x2.shape[0]
    tm = _pick_tile(seq, 512)
    nl = seq // tm
    in_cols = w_in.shape[1]
    cos_t, sa_t, sb_t = _rope_tables(seq)
    row = lambda i: (i, 0)
    fixed = lambda i: (0, 0)
    tab = lambda i: (i % nl, 0)
    return pl.pallas_call(
        _in_proj_kernel,
        grid=(n // tm,),
        in_specs=[
            pl.BlockSpec((tm, D_MODEL), row),
            pl.BlockSpec((1, D_MODEL), fixed),
            pl.BlockSpec((D_MODEL, in_cols), fixed),
            pl.BlockSpec((tm, LANES), tab),
            pl.BlockSpec((tm, LANES), tab),
            pl.BlockSpec((tm, LANES), tab),
        ],
        out_specs=[
            pl.BlockSpec((tm, ATT_WIDTH), row),
            pl.BlockSpec((tm, ATT_WIDTH), row),
            pl.BlockSpec((tm, ATT_WIDTH), row),
            pl.BlockSpec((tm, SSM_WIDTH), row),
            pl.BlockSpec((tm, 2 * D_MODEL), row),
        ],
        out_shape=[
            jax.ShapeDtypeStruct((n, ATT_WIDTH), BF16),
            jax.ShapeDtypeStruct((n, ATT_WIDTH), BF16),
            jax.ShapeDtypeStruct((n, ATT_WIDTH), BF16),
            jax.ShapeDtypeStruct((n, SSM_WIDTH), BF16),
            jax.ShapeDtypeStruct((n, 2 * D_MODEL), BF16),
        ],
        compiler_params=_params(("parallel",)),
        name="in_proj",
    )(x2, g_mix.reshape(1, D_MODEL), w_in, cos_t, sa_t, sb_t)


def _col_reduce(x, pair_op, reduce_op):
    r = x.shape[0]
    rb = min(ATTN_ROW_BLOCK, r)
    out = None
    for i in range(r // rb):
        blk = x[i * rb:(i + 1) * rb]
        n = rb
        while n > SUBLANES:
            n //= 2
            blk = pair_op(blk[:n], blk[n:])
        out = blk if out is None else pair_op(out, blk)
    return reduce_op(out, axis=0, keepdims=True)


def _attn_kernel(lam_ref, q_ref, k_ref, v_ref, g_ref, o_ref, vt_ref, r_ref, l_ref, acc_ref, p_ref, be_ref, fl_ref,
                 *, tq, tk, nk):
    qi = pl.program_id(2)

    @pl.when(qi == 0)
    def _():
        def tr(j, c):
            off = pl.multiple_of(j * tk, tk)
            vt_ref[:, pl.ds(off, tk)] = v_ref[0, pl.ds(off, tk), :].astype(F32).T.astype(BF16)
            return c
        lax.fori_loop(0, nk, tr, 0)

    qt = q_ref[0].astype(F32).T
    row = lax.broadcasted_iota(jnp.int32, qt.shape, 0)
    qm = jnp.concatenate([jnp.where(row < HEAD_DIM, qt, 0.0),
                          jnp.where(row >= HEAD_DIM, qt, 0.0)], axis=1).astype(BF16)

    def scores(j):
        off = pl.multiple_of(j * tk, tk)
        return jnp.dot(k_ref[0, pl.ds(off, tk), :], qm, preferred_element_type=F32)

    def values(j, slot):
        off = pl.multiple_of(j * tk, tk)
        pv = jnp.dot(vt_ref[:, pl.ds(off, tk)], p_ref[slot], preferred_element_type=F32)
        acc_ref[...] = be_ref[slot] * (acc_ref[...] + pv)

    s = scores(0)
    r0 = _col_reduce(s, jnp.maximum, jnp.max)
    p = jnp.exp2(s - r0)
    r_ref[...] = r0
    l_ref[...] = _col_reduce(p, jnp.add, jnp.sum)
    p_ref[0] = p.astype(BF16)
    be_ref[0] = jnp.ones(be_ref.shape[1:], F32)
    fl_ref[...] = jnp.zeros(fl_ref.shape, F32)
    acc_ref[...] = jnp.zeros(acc_ref.shape, F32)

    def step(j, slot):
        s = scores(j)
        ref = r_ref[...]
        p = jnp.exp2(s - ref)
        cm = _col_reduce(s, jnp.maximum, jnp.max)
        new = jnp.maximum(ref, cm)
        beta = jnp.exp2(ref - new)
        fl_ref[...] = jnp.maximum(fl_ref[...], cm - ref)
        l_ref[...] = beta * (l_ref[...] + _col_reduce(p, jnp.add, jnp.sum))
        r_ref[...] = new
        be_ref[slot] = beta
        p_ref[slot] = p.astype(BF16)
        values(j - 1, 1 - slot)

    if nk > 1:
        def body(i, c):
            step(2 * i + 1, 1)
            step(2 * i + 2, 0)
            return c
        lax.fori_loop(0, (nk - 2) // 2, body, 0)
        step(nk - 1, 1)
    values(nk - 1, (nk - 1) % 2)

    @pl.when(jnp.max(fl_ref[...]) > ATTN_OVERFLOW_MARGIN)
    def _():
        r_ref[...] = jnp.full(r_ref.shape, NEG_BIG, F32)
        l_ref[...] = jnp.zeros(l_ref.shape, F32)
        acc_ref[...] = jnp.zeros(acc_ref.shape, F32)

        def safe(j, c):
            s = scores(j)
            m_old = r_ref[...]
            m_new = jnp.maximum(m_old, _col_reduce(s, jnp.maximum, jnp.max))
            alpha = jnp.exp2(m_old - m_new)
            p = jnp.exp2(s - m_new)
            l_ref[...] = alpha * l_ref[...] + _col_reduce(p, jnp.add, jnp.sum)
            r_ref[...] = m_new
            off = pl.multiple_of(j * tk, tk)
            pv = jnp.dot(vt_ref[:, pl.ds(off, tk)], p.astype(BF16), preferred_element_type=F32)
            acc_ref[...] = alpha * acc_ref[...] + pv
            return c
        lax.fori_loop(0, nk, safe, 0)

    lam = lam_ref[0]
    o_all = acc_ref[...] * (1.0 / l_ref[...])
    o = o_all[:, :tq] - lam * o_all[:, tq:]
    o = o * lax.rsqrt(jnp.mean(o * o, axis=0, keepdims=True) + EPS) * g_ref[...]
    o_ref[0] = (o * (1.0 - LAM_INIT)).T.astype(BF16)


def _attention(q, k, v, lam, subln_g):
    bsz, seq, _ = q.shape
    tq_pref, tk_pref = ATTN_TILES_SHORT if seq <= ATTN_SHORT_SEQ else ATTN_TILES_LONG
    tq = _pick_tile(seq, tq_pref)
    tk = _pick_tile(seq, tk_pref)
    nk = seq // tk
    assert nk == 1 or nk % 2 == 0
    kern = functools.partial(_attn_kernel, tq=tq, tk=tk, nk=nk)
    return pl.pallas_call(
        kern,
        grid=(bsz, N_HEADS, seq // tq),
        in_specs=[
            pl.BlockSpec(memory_space=pltpu.SMEM),
            pl.BlockSpec((1, tq, LANES), lambda b, h, i: (b, i, h)),
            pl.BlockSpec((1, seq, LANES), lambda b, h, i: (b, 0, h)),
            pl.BlockSpec((1, seq, LANES), lambda b, h, i: (b, 0, h)),
            pl.BlockSpec((ATT_V_DIM, 1), lambda b, h, i: (0, 0)),
        ],
        out_specs=pl.BlockSpec((1, tq, LANES), lambda b, h, i: (b, i, h)),
        out_shape=jax.ShapeDtypeStruct((bsz, seq, ATT_WIDTH), BF16),
        scratch_shapes=[
            pltpu.VMEM((ATT_V_DIM, seq), BF16),
            pltpu.VMEM((1, 2 * tq), F32),
            pltpu.VMEM((1, 2 * tq), F32),
            pltpu.VMEM((ATT_V_DIM, 2 * tq), F32),
            pltpu.VMEM((2, tk, 2 * tq), BF16),
            pltpu.VMEM((2, 1, 2 * tq), F32),
            pltpu.VMEM((1, 2 * tq), F32),
        ],
        compiler_params=_params(("parallel", "parallel", "arbitrary")),
        name="diff_attn",
    )(lam.reshape(1), q, k, v, subln_g.reshape(ATT_V_DIM, 1))


def _ssm_weights(lam_re, lam_im, log_dt, b_re, b_im, c_re, c_im, d_skip):
    T, G, P, CH = SSM_CHUNK, SSM_GROUPS, SSM_STATE, SSM_CH
    n = jnp.arange(T + 1, dtype=F32)[:, None, None]
    pw_re, pw_im, bb_re, bb_im = [], [], [], []
    for dirn in range(2):
        dt = jnp.exp(log_dt[dirn])[:, None]
        lr, li = lam_re[dirn], lam_im[dirn]
        mag = jnp.exp(lr * dt)
        a_re, a_im = mag * jnp.cos(li * dt), mag * jnp.sin(li * dt)
        den = lr * lr + li * li
        n_re = a_re - 1.0
        q_re = (n_re * lr + a_im * li) / den
        q_im = (a_im * lr - n_re * li) / den
        bb_re.append(q_re[..., None] * b_re[dirn] - q_im[..., None] * b_im[dirn])
        bb_im.append(q_re[..., None] * b_im[dirn] + q_im[..., None] * b_re[dirn])
        magn = jnp.exp(n * (lr * dt)[None])
        pw_re.append(magn * jnp.cos(n * (li * dt)[None]))
        pw_im.append(magn * jnp.sin(n * (li * dt)[None]))

    def cmul(ar, ai, br, bi):
        return ar * br - ai * bi, ar * bi + ai * br

    kern = []
    for dirn in range(2):
        car, cai = cmul(c_re[dirn][None], c_im[dirn][None],
                        pw_re[dirn][:T, :, None, :], pw_im[dirn][:T, :, None, :])
        kern.append(jnp.einsum('dgcp,gpk->dgck', car, bb_re[dirn], precision=HIGHEST)
                    - jnp.einsum('dgcp,gpk->dgck', cai, bb_im[dirn], precision=HIGHEST))
    tj = jnp.arange(T)
    lag = tj[None, :] - tj[:, None]
    mf = jnp.where((lag >= 0)[:, :, None, None, None], kern[0][jnp.clip(lag, 0, T - 1)], 0.0)
    mb = jnp.where((lag <= 0)[:, :, None, None, None], kern[1][jnp.clip(-lag, 0, T - 1)], 0.0)
    dsk = (lag == 0)[:, :, None, None, None] * (jnp.eye(CH, dtype=F32)[None, None, None] * d_skip[None, None, :, :, None])
    m_all = (mf + mb + dsk).transpose(2, 0, 4, 1, 3).reshape(G, T * CH, T * CH)

    wf_re, wf_im = cmul(pw_re[0][T - 1 - tj][:, :, :, None], pw_im[0][T - 1 - tj][:, :, :, None],
                        bb_re[0][None], bb_im[0][None])
    wb_re, wb_im = cmul(pw_re[1][tj][:, :, :, None], pw_im[1][tj][:, :, :, None], bb_re[1][None], bb_im[1][None])
    w_all = jnp.stack([wf_re, wf_im, wb_re, wb_im], axis=0).transpose(2, 1, 4, 0, 3)
    w_all = w_all.reshape(G // 2, 2, T * CH, 4, P)
    eye2 = jnp.eye(2, dtype=F32)
    w_pair = (w_all[:, :, :, :, None, :] * eye2[None, :, None, None, :, None]).reshape(G // 2, 2 * T * CH, 4 * 2 * P)

    zf_re, zf_im = cmul(c_re[0][None], c_im[0][None], pw_re[0][1:T + 1, :, None, :], pw_im[0][1:T + 1, :, None, :])
    zb_re, zb_im = cmul(c_re[1][None], c_im[1][None],
                        pw_re[1][T - tj][:, :, None, :], pw_im[1][T - tj][:, :, None, :])
    z_all = jnp.stack([zf_re, -zf_im, zb_re, -zb_im], axis=0).transpose(2, 0, 4, 1, 3)
    z_all = z_all.reshape(G // 2, 2, 4, P, T * CH)
    z_pair = (z_all.transpose(0, 2, 1, 3, 4)[:, :, :, :, None, :] * eye2[None, None, :, None, :, None])
    z_pair = z_pair.reshape(G // 2, 4 * 2 * P, 2 * T * CH)

    a_tab = []
    for dirn in range(2):
        dt = jnp.exp(log_dt[dirn])[:, None]
        order = jnp.arange(SUBLANES) if dirn == 0 else SUBLANES - 1 - jnp.arange(SUBLANES)
        nn = jnp.concatenate([order, jnp.array([1, 2, 4, 8])]).astype(F32)[:, None, None] * T
        magn = jnp.exp(nn * (lam_re[dirn] * dt)[None])
        ang = nn * (lam_im[dirn] * dt)[None]
        tab = jnp.stack([magn * jnp.cos(ang), magn * jnp.sin(ang)], axis=0)
        tab = jnp.pad(tab, ((0, 0), (0, 2 * SUBLANES - tab.shape[1]), (0, 0), (0, 0)))
        a_tab.append(tab.reshape(2, 2 * SUBLANES, G // 2, 2 * P).transpose(2, 0, 1, 3))
    a_tab = jnp.stack(a_tab, axis=1)
    return m_all.astype(BF16), w_pair.astype(BF16), z_pair.astype(BF16), a_tab


def _tile_scan(x_re, x_im, a_tab, reverse):
    row = lax.broadcasted_iota(jnp.int32, x_re.shape, 0)
    for lvl, d in enumerate((1, 2, 4)):
        ar = a_tab[0][SUBLANES + lvl:SUBLANES + lvl + 1, :]
        ai = a_tab[1][SUBLANES + lvl:SUBLANES + lvl + 1, :]
        if reverse:
            keep = row < SUBLANES - d
            shift = SUBLANES - d
        else:
            keep = row >= d
            shift = d
        sr = jnp.where(keep, pltpu.roll(x_re, shift, 0), 0.0)
        si = jnp.where(keep, pltpu.roll(x_im, shift, 0), 0.0)
        x_re, x_im = x_re + ar * sr - ai * si, x_im + ar * si + ai * sr
    return x_re, x_im


def _ssm_kernel(u_ref, w_ref, m_ref, z_ref, a_ref, y_ref, v_ref, s_ref, *, nc):
    pl_ = SSM_PAIR_LANES
    nt = nc // SUBLANES
    u0 = u_ref[0, 0]
    u1 = u_ref[0, 1]
    v_ref[...] = jnp.dot(jnp.concatenate([u0, u1], axis=1), w_ref[0], preferred_element_type=F32)

    row = lax.broadcasted_iota(jnp.int32, (SUBLANES, pl_), 0)

    def tile_step(i, carry):
        new = []
        for dirn in range(2):
            c_re, c_im = carry[2 * dirn], carry[2 * dirn + 1]
            a_tab = (a_ref[0, dirn, 0], a_ref[0, dirn, 1])
            t = i if dirn == 0 else nt - 1 - i
            r0 = pl.multiple_of(t * SUBLANES, SUBLANES)
            lo = 2 * dirn * pl_
            x_re, x_im = _tile_scan(v_ref[pl.ds(r0, SUBLANES), lo:lo + pl_],
                                    v_ref[pl.ds(r0, SUBLANES), lo + pl_:lo + 2 * pl_], a_tab, dirn == 1)
            if dirn == 0:
                keep, shift, last = row >= 1, 1, SUBLANES - 1
            else:
                keep, shift, last = row < SUBLANES - 1, SUBLANES - 1, 0
            pr, pi = a_tab[0][0:SUBLANES, :], a_tab[1][0:SUBLANES, :]
            s_ref[pl.ds(r0, SUBLANES), lo:lo + pl_] = (
                pr * c_re - pi * c_im + jnp.where(keep, pltpu.roll(x_re, shift, 0), 0.0))
            s_ref[pl.ds(r0, SUBLANES), lo + pl_:lo + 2 * pl_] = (
                pr * c_im + pi * c_re + jnp.where(keep, pltpu.roll(x_im, shift, 0), 0.0))
            a8r = a_tab[0][SUBLANES + 3:SUBLANES + 4, :]
            a8i = a_tab[1][SUBLANES + 3:SUBLANES + 4, :]
            new.append(a8r * c_re - a8i * c_im + x_re[last:last + 1, :])
            new.append(a8r * c_im + a8i * c_re + x_im[last:last + 1, :])
        return tuple(new)

    zero = jnp.zeros((1, pl_), F32)
    lax.fori_loop(0, nt, tile_step, (zero, zero, zero, zero))

    ys = jnp.dot(s_ref[...].astype(BF16), z_ref[0], preferred_element_type=F32)
    half = ys.shape[1] // 2
    y_ref[0, 0] = jax.nn.gelu(ys[:, :half] + jnp.dot(u0, m_ref[0], preferred_element_type=F32)).astype(BF16)
    y_ref[0, 1] = jax.nn.gelu(ys[:, half:] + jnp.dot(u1, m_ref[1], preferred_element_type=F32)).astype(BF16)


def _ssm(s_in, bsz, seq, weights):
    m_all, w_pair, z_pair, a_pair = weights
    T, G, CH = SSM_CHUNK, SSM_GROUPS, SSM_CH
    nc = seq // T
    tc = T * CH
    ug = s_in.reshape(bsz, nc, T, G, CH).transpose(0, 3, 1, 2, 4).reshape(bsz, G, nc, tc)
    kern = functools.partial(_ssm_kernel, nc=nc)
    yg = pl.pallas_call(
        kern,
        grid=(G // 2, bsz),
        in_specs=[
            pl.BlockSpec((1, 2, nc, tc), lambda g, b: (b, g, 0, 0)),
            pl.BlockSpec((1, 2 * tc, 4 * SSM_PAIR_LANES), lambda g, b: (g, 0, 0)),
            pl.BlockSpec((2, tc, tc), lambda g, b: (g, 0, 0)),
            pl.BlockSpec((1, 4 * SSM_PAIR_LANES, 2 * tc), lambda g, b: (g, 0, 0)),
            pl.BlockSpec((1, 2, 2, 2 * SUBLANES, SSM_PAIR_LANES), lambda g, b: (g, 0, 0, 0, 0)),
        ],
        out_specs=pl.BlockSpec((1, 2, nc, tc), lambda g, b: (b, g, 0, 0)),
        out_shape=jax.ShapeDtypeStruct((bsz, G, nc, tc), BF16),
        scratch_shapes=[
            pltpu.VMEM((nc, 4 * SSM_PAIR_LANES), F32),
            pltpu.VMEM((nc, 4 * SSM_PAIR_LANES), F32),
        ],
        compiler_params=_params(("parallel", "arbitrary")),
        name="s5_scan",
    )(ug, w_pair, m_all, z_pair, a_pair)
    return yg.reshape(bsz, G, nc, T, CH).transpose(0, 2, 3, 1, 4).reshape(bsz * seq, SSM_WIDTH)


def _mix_kernel(h_ref, o_ref, yg_ref, gl_ref, wglu_ref, bglu_ref, wa_ref, ws_ref, wo_ref, out_ref):
    yg = yg_ref[...]
    z = jnp.dot(yg, wglu_ref[...], preferred_element_type=F32) + bglu_ref[...]
    ysg = (yg.astype(F32) * _sigmoid(z)).astype(BF16)
    br_a = jnp.dot(o_ref[...], wa_ref[...], preferred_element_type=F32)
    br_s = jnp.dot(ysg, ws_ref[...], preferred_element_type=F32)
    gl = gl_ref[...].astype(F32)
    mix = _sigmoid(gl[:, :D_MODEL]) * br_a + _sigmoid(gl[:, D_MODEL:]) * br_s
    out_ref[...] = h_ref[...] + jnp.dot(mix.astype(BF16), wo_ref[...], preferred_element_type=F32)


def _mix(h, o, ys, gl, w_glu, b_glu, w_attn_br, w_ssm_br, w_out):
    n = h.shape[0]
    tm = _pick_tile(n, 512)
    row = lambda i: (i, 0)
    fixed = lambda i: (0, 0)
    return pl.pallas_call(
        _mix_kernel,
        grid=(n // tm,),
        in_specs=[
            pl.BlockSpec((tm, D_MODEL), row),
            pl.BlockSpec((tm, ATT_WIDTH), row),
            pl.BlockSpec((tm, SSM_WIDTH), row),
            pl.BlockSpec((tm, 2 * D_MODEL), row),
            pl.BlockSpec((SSM_WIDTH, SSM_WIDTH), fixed),
            pl.BlockSpec((1, SSM_WIDTH), fixed),
            pl.BlockSpec((ATT_WIDTH, D_MODEL), fixed),
            pl.BlockSpec((SSM_WIDTH, D_MODEL), fixed),
            pl.BlockSpec((D_MODEL, D_MODEL), fixed),
        ],
        out_specs=pl.BlockSpec((tm, D_MODEL), row),
        out_shape=jax.ShapeDtypeStruct((n, D_MODEL), F32),
        compiler_params=_params(("parallel",)),
        name="branch_mix",
    )(h, o, ys, gl, w_glu, b_glu.reshape(1, SSM_WIDTH), w_attn_br, w_ssm_br, w_out)


def _lane_max(x):
    return jnp.max(x, axis=-1, keepdims=True)


def _router(u, wr_ref, br_ref):
    logits = jnp.dot(u, wr_ref[...], preferred_element_type=F32) + br_ref[...]
    lane = lax.broadcasted_iota(jnp.int32, logits.shape, 1).astype(F32)
    far = float(4 * LANES)
    is_grp = (lane >= N_EXPERTS) & (lane < N_EXPERTS + N_GROUPS)
    lg = jnp.where(is_grp, logits, NEG_BIG)
    g_max = _lane_max(lg)
    p_sel = 1.0 / jnp.sum(jnp.where(is_grp, jnp.exp(lg - g_max), 0.0), axis=-1, keepdims=True)
    g_sel = jnp.min(jnp.where(lg == g_max, lane, far), axis=-1, keepdims=True) - N_EXPERTS
    g_lo = g_sel * EXPERTS_PER_GROUP
    in_grp = (lane >= g_lo) & (lane < g_lo + EXPERTS_PER_GROUP)
    l2 = jnp.where(in_grp, logits, NEG_BIG)
    m1 = _lane_max(l2)
    i1 = jnp.min(jnp.where(l2 == m1, lane, far), axis=-1, keepdims=True)
    l2b = jnp.where(lane == i1, NEG_BIG, l2)
    m2 = _lane_max(l2b)
    i2 = jnp.min(jnp.where(l2b == m2, lane, far), axis=-1, keepdims=True)
    e2 = jnp.exp(m2 - m1)
    w1 = 1.0 / (1.0 + e2)
    w2 = e2 * w1
    gates = p_sel * (jnp.where(lane == i1, w1, 0.0) + jnp.where(lane == i2, w2, 0.0))
    return gates, jnp.where(lane == g_sel, 1.0, 0.0)


def _moe_kernel(h_ref, g_ref, wr_ref, br_ref, tri_ref, w13_ref, w2_ref, out_ref,
                ts_ref, gs_ref, acc_ref, pt_ref, meta_ref, *, tm, win, eps):
    step = pl.program_id(1)

    @pl.when(step == 0)
    def _():
        u = _rms(h_ref[...], g_ref[...]).astype(BF16)
        gates, onehot = _router(u, wr_ref, br_ref)
        rank = jnp.dot(tri_ref[...], onehot.astype(BF16), preferred_element_type=F32)
        cnt = jnp.sum(onehot, axis=0, keepdims=True)
        off = pltpu.roll(cnt, 1, 1) + pltpu.roll(cnt, 2, 1) + pltpu.roll(cnt, 3, 1)
        pos = jnp.sum(onehot * (off + rank), axis=-1, keepdims=True)
        slot_lane = lax.broadcasted_iota(jnp.int32, (tm, tm), 1).astype(F32)
        pt_ref[...] = jnp.where(slot_lane == pos, 1.0, 0.0).astype(BF16)
        pos_row = jnp.broadcast_to(pos, (tm, LANES)).T[0:1, :]
        slot_sub = lax.broadcasted_iota(jnp.int32, (tm, tm), 0).astype(F32)
        perm = jnp.where(slot_sub == pos_row, 1.0, 0.0).astype(BF16)
        ts_ref[0:tm, :] = jnp.dot(perm, u, preferred_element_type=F32).astype(BF16)
        g_hi = gates.astype(BF16)
        g_lo = (gates - g_hi.astype(F32)).astype(BF16)
        gs_ref[0:tm, :] = (jnp.dot(perm, g_hi, preferred_element_type=F32)
                           + jnp.dot(perm, g_lo, preferred_element_type=F32))
        ts_ref[tm:tm + win, :] = jnp.zeros((win, D_MODEL), BF16)
        gs_ref[tm:tm + win, :] = jnp.zeros((win, LANES), F32)
        acc_ref[...] = jnp.zeros(acc_ref.shape, F32)
        off_i = off.astype(jnp.int32)
        end_i = (off + cnt).astype(jnp.int32)
        for g in range(N_GROUPS):
            meta_ref[0, g] = off_i[0, g]
            meta_ref[1, g] = end_i[0, g]

    e0 = step * eps
    grp = lax.shift_right_logical(e0, 3)
    start = lax.shift_left(lax.shift_right_logical(meta_ref[0, grp], 4), 4)
    span = meta_ref[1, grp] - start
    nwin = 1 + sum((span > k * win).astype(jnp.int32) for k in range(1, tm // win + 2))

    def window(i, c):
        r0 = pl.multiple_of(start + i * win, 16)
        x = ts_ref[pl.ds(r0, win), :]
        gsl = gs_ref[pl.ds(r0, win), :]
        lane = lax.broadcasted_iota(jnp.int32, gsl.shape, 1)
        y = None
        for k in range(eps):
            ab = jnp.dot(x, w13_ref[k], preferred_element_type=F32)
            a = ab[:, :EXPERT_FF]
            hdn = a * _sigmoid(a) * ab[:, EXPERT_FF:]
            ge = jnp.sum(jnp.where(lane == e0 + k, gsl, 0.0), axis=-1, keepdims=True)
            yk = jnp.dot((hdn * ge).astype(BF16), w2_ref[k], preferred_element_type=F32)
            y = yk if y is None else y + yk
        acc_ref[pl.ds(r0, win), :] += y
        return c
    lax.fori_loop(0, nwin, window, 0)

    @pl.when(step == pl.num_programs(1) - 1)
    def _():
        out_ref[...] = h_ref[...] + jnp.dot(pt_ref[...], acc_ref[0:tm, :].astype(BF16), preferred_element_type=F32)


def _moe(h, g_ffn, w_router, b_router, w13, w2, win=MOE_WINDOW, eps=MOE_EXPERTS_PER_STEP):
    n = h.shape[0]
    tm = _pick_tile(n, 1024)
    tri = jnp.tri(tm, tm, -1, dtype=BF16)
    row = lambda i, e: (i, 0)
    fixed = lambda i, e: (0, 0)
    kern = functools.partial(_moe_kernel, tm=tm, win=win, eps=eps)
    return pl.pallas_call(
        kern,
        grid=(n // tm, N_EXPERTS // eps),
        in_specs=[
            pl.BlockSpec((tm, D_MODEL), row),
            pl.BlockSpec((1, D_MODEL), fixed),
            pl.BlockSpec((D_MODEL, LANES), fixed),
            pl.BlockSpec((1, LANES), fixed),
            pl.BlockSpec((tm, tm), fixed),
            pl.BlockSpec((eps, D_MODEL, 2 * EXPERT_FF), lambda i, e: (e, 0, 0)),
            pl.BlockSpec((eps, EXPERT_FF, D_MODEL), lambda i, e: (e, 0, 0)),
        ],
        out_specs=pl.BlockSpec((tm, D_MODEL), row),
        out_shape=jax.ShapeDtypeStruct((n, D_MODEL), F32),
        scratch_shapes=[
            pltpu.VMEM((tm + win, D_MODEL), BF16),
            pltpu.VMEM((tm + win, LANES), F32),
            pltpu.VMEM((tm + win, D_MODEL), F32),
            pltpu.VMEM((tm, tm), BF16),
            pltpu.SMEM((2, N_GROUPS), jnp.int32),
        ],
        compiler_params=_params(("parallel", "arbitrary")),
        name="hier_moe",
    )(h, g_ffn.reshape(1, D_MODEL), w_router, b_router, tri, w13, w2)


def _ple_kernel(h_ref, p_ref, gp_ref, wg_ref, wp_ref, gf_ref, out_ref):
    h = h_ref[...]
    gate = _sigmoid(jnp.dot(_rms(h, gp_ref[...]).astype(BF16), wg_ref[...], preferred_element_type=F32))
    pe = jnp.dot(p_ref[...].astype(BF16), wp_ref[...], preferred_element_type=F32)
    out_ref[...] = _rms(h + gate * pe, gf_ref[...])


def _ple(h, p, g_ple, w_ple_gate, w_ple, g_final):
    n = h.shape[0]
    tm = _pick_tile(n, 512)
    row = lambda i: (i, 0)
    fixed = lambda i: (0, 0)
    return pl.pallas_call(
        _ple_kernel,
        grid=(n // tm,),
        in_specs=[
            pl.BlockSpec((tm, D_MODEL), row),
            pl.BlockSpec((tm, PLE_DIM), row),
            pl.BlockSpec((1, D_MODEL), fixed),
            pl.BlockSpec((D_MODEL, D_MODEL), fixed),
            pl.BlockSpec((PLE_DIM, D_MODEL), fixed),
            pl.BlockSpec((1, D_MODEL), fixed),
        ],
        out_specs=pl.BlockSpec((tm, D_MODEL), row),
        out_shape=jax.ShapeDtypeStruct((n, D_MODEL), F32),
        compiler_params=_params(("parallel",)),
        name="ple_final",
    )(h, p, g_ple.reshape(1, D_MODEL), w_ple_gate, w_ple, g_final.reshape(1, D_MODEL))


def _prep_weights(g_mix, w_in, lam_q1, lam_k1, lam_q2, lam_k2, subln_g,
                  ssm_lam_re, ssm_lam_im, ssm_log_dt, ssm_b_re, ssm_b_im, ssm_c_re, ssm_c_im, ssm_d,
                  w_glu, b_glu, w_attn_br, w_ssm_br, w_out, g_ffn, w_r1, b_r1, w_r2, b_r2,
                  w_e1, w_e3, w_e2, g_ple, w_ple_gate, w_ple, g_final):
    i = 0
    lam = (jnp.exp(jnp.sum(lam_q1[i] * lam_k1[i])) - jnp.exp(jnp.sum(lam_q2[i] * lam_k2[i])) + LAM_INIT)
    pad = LANES - N_EXPERTS - N_GROUPS
    w_router = jnp.concatenate([w_r2[i], w_r1[i], jnp.zeros((D_MODEL, pad), F32)], axis=1).astype(BF16)
    b_router = jnp.concatenate([b_r2[i], b_r1[i], jnp.zeros((pad,), F32)]).reshape(1, LANES)
    return dict(
        g_mix=g_mix[i], w_in=w_in[i].astype(BF16), lam=lam.astype(F32), subln_g=subln_g[i],
        ssm=_ssm_weights(ssm_lam_re[i], ssm_lam_im[i], ssm_log_dt[i], ssm_b_re[i], ssm_b_im[i],
                         ssm_c_re[i], ssm_c_im[i], ssm_d[i]),
        w_glu=w_glu[i].astype(BF16), b_glu=b_glu[i], w_attn_br=w_attn_br[i].astype(BF16),
        w_ssm_br=w_ssm_br[i].astype(BF16), w_out=w_out[i].astype(BF16), g_ffn=g_ffn[i],
        w_router=w_router, b_router=b_router,
        w13=jnp.concatenate([w_e1[i], w_e3[i]], axis=-1).astype(BF16), w2=w_e2[i].astype(BF16),
        g_ple=g_ple[i], w_ple_gate=w_ple_gate[i].astype(BF16), w_ple=w_ple[i].astype(BF16), g_final=g_final,
    )


def _trunk(x, p, w):
    bsz, seq, _ = x.shape
    n = bsz * seq
    x2 = x.reshape(n, D_MODEL)
    q, k, v, s_in, gl = _in_proj(x2, seq, w["g_mix"], w["w_in"])
    shp = (bsz, seq, ATT_WIDTH)
    o = _attention(q.reshape(shp), k.reshape(shp), v.reshape(shp), w["lam"], w["subln_g"])
    ys = _ssm(s_in, bsz, seq, w["ssm"])
    h = _mix(x2, o.reshape(n, ATT_WIDTH), ys, gl, w["w_glu"], w["b_glu"], w["w_attn_br"], w["w_ssm_br"], w["w_out"])
    h = _moe(h, w["g_ffn"], w["w_router"], w["b_router"], w["w13"], w["w2"])
    y = _ple(h, p[0].reshape(n, PLE_DIM), w["g_ple"], w["w_ple_gate"], w["w_ple"], w["g_final"])
    return y.reshape(bsz, seq, D_MODEL)


def kernel(x_prompt, x_sample, p_prompt, p_sample, g_mix, w_in, lam_q1, lam_k1, lam_q2, lam_k2, subln_g, ssm_lam_re, ssm_lam_im, ssm_log_dt, ssm_b_re, ssm_b_im, ssm_c_re, ssm_c_im, ssm_d, w_glu, b_glu, w_attn_br, w_ssm_br, w_out, g_ffn, w_r1, b_r1, w_r2, b_r2, w_e1, w_e3, w_e2, g_ple, w_ple_gate, w_ple, g_final):
    w = _prep_weights(g_mix, w_in, lam_q1, lam_k1, lam_q2, lam_k2, subln_g,
                      ssm_lam_re, ssm_lam_im, ssm_log_dt, ssm_b_re, ssm_b_im, ssm_c_re, ssm_c_im, ssm_d,
                      w_glu, b_glu, w_attn_br, w_ssm_br, w_out, g_ffn, w_r1, b_r1, w_r2, b_r2,
                      w_e1, w_e3, w_e2, g_ple, w_ple_gate, w_ple, g_final)
    return (_trunk(x_prompt, p_prompt, w), _trunk(x_sample, p_sample, w))
```

```python
import functools
import math

import jax
import jax.numpy as jnp
from jax import lax
from jax.experimental import pallas as pl
from jax.experimental.pallas import tpu as pltpu

F32 = jnp.float32
BF16 = jnp.bfloat16
HIGHEST = lax.Precision.HIGHEST

D_MODEL = 1024
N_HEADS = 8
HEAD_DIM = 64
ATT_V_DIM = 2 * HEAD_DIM
ATT_WIDTH = N_HEADS * ATT_V_DIM
ROT_DIM = HEAD_DIM // 4
ROPE_THETA = 500000.0
SSM_CH = 16
SSM_GROUPS = 32
SSM_WIDTH = SSM_GROUPS * SSM_CH
SSM_STATE = 64
N_GROUPS = 4
EXPERTS_PER_GROUP = 8
N_EXPERTS = N_GROUPS * EXPERTS_PER_GROUP
EXPERT_FF = 256
PLE_DIM = 256
EPS = 1e-6
LAM_INIT = 0.8 - 0.6 * math.exp(-0.3 * 0)

LANES = 128
SUBLANES = 8
SSM_CHUNK = 16
SSM_PAIR_LANES = 2 * SSM_STATE
VMEM_LIMIT = 56 * 1024 * 1024
NEG_BIG = -1e30
LOG2E = math.log2(math.e)
ATTN_ROW_BLOCK = 64
ATTN_OVERFLOW_MARGIN = 100.0
ATTN_SHORT_SEQ = 2048
ATTN_TILES_SHORT = (2048, 512)
ATTN_TILES_LONG = (1024, 1024)
MOE_WINDOW = 320
MOE_EXPERTS_PER_STEP = 4


def _params(sem):
    return pltpu.CompilerParams(dimension_semantics=sem, vmem_limit_bytes=VMEM_LIMIT)


def _rms(x, g):
    return x * lax.rsqrt(jnp.mean(x * x, axis=-1, keepdims=True) + EPS) * g


def _sigmoid(x):
    return 1.0 / (1.0 + jnp.exp(-x))


def _pick_tile(n, pref):
    t = min(n, pref)
    while n % t:
        t //= 2
    return t


def _in_proj_kernel(x_ref, g_ref, w_ref, rope_ref, q_ref, k_ref, v_ref, s_ref, gl_ref):
    u = _rms(x_ref[...], g_ref[...]).astype(BF16)
    cos = rope_ref[:, 0:LANES]
    sa = rope_ref[:, LANES:2 * LANES]
    sb = rope_ref[:, 2 * LANES:3 * LANES]
    for sec, out_ref, scale in ((0, q_ref, HEAD_DIM ** -0.5 * LOG2E), (1, k_ref, 1.0)):
        y = jnp.dot(u, w_ref[:, sec * ATT_WIDTH:(sec + 1) * ATT_WIDTH], preferred_element_type=F32)
        for h in range(N_HEADS):
            yh = y[:, h * LANES:(h + 1) * LANES]
            r = yh * cos + pltpu.roll(yh, LANES - ROT_DIM // 2, 1) * sa + pltpu.roll(yh, ROT_DIM // 2, 1) * sb
            out_ref[:, h * LANES:(h + 1) * LANES] = (r * scale).astype(BF16)
    c0 = 2 * ATT_WIDTH
    v_ref[...] = jnp.dot(u, w_ref[:, c0:c0 + ATT_WIDTH], preferred_element_type=F32).astype(BF16)
    c0 += ATT_WIDTH
    s_ref[...] = jnp.dot(u, w_ref[:, c0:c0 + SSM_WIDTH], preferred_element_type=F32).astype(BF16)
    c0 += SSM_WIDTH
    gl_ref[...] = jnp.dot(u, w_ref[:, c0:c0 + 2 * D_MODEL], preferred_element_type=F32).astype(BF16)


def _rope_tables(seq):
    half = ROT_DIM // 2
    inv = ROPE_THETA ** (-jnp.arange(0, ROT_DIM, 2, dtype=F32) / ROT_DIM)
    ang = jnp.arange(seq, dtype=F32)[:, None] * inv[None, :]
    cos, sin = jnp.cos(ang), jnp.sin(ang)
    one = jnp.ones((seq, HEAD_DIM - ROT_DIM), F32)
    zero = jnp.zeros((seq, HEAD_DIM - ROT_DIM), F32)
    zh = jnp.zeros((seq, half), F32)
    cos_t = [cos, cos, one]
    sa_t = [-sin, zh, zero]
    sb_t = [zh, sin, zero]
    return jnp.concatenate(cos_t * 2 + sa_t * 2 + sb_t * 2, axis=1)


def _in_proj(x2, seq, g_mix, w_in, rope):
    n = x2.shape[0]
    tm = _pick_tile(seq, 512)
    nl = seq // tm
    in_cols = w_in.shape[1]
    row = lambda i: (i, 0)
    fixed = lambda i: (0, 0)
    tab = lambda i: (i % nl, 0)
    return pl.pallas_call(
        _in_proj_kernel,
        grid=(n // tm,),
        in_specs=[
            pl.BlockSpec((tm, D_MODEL), row),
            pl.BlockSpec((1, D_MODEL), fixed),
            pl.BlockSpec((D_MODEL, in_cols), fixed),
            pl.BlockSpec((tm, 3 * LANES), tab),
        ],
        out_specs=[
            pl.BlockSpec((tm, ATT_WIDTH), row),
            pl.BlockSpec((tm, ATT_WIDTH), row),
            pl.BlockSpec((tm, ATT_WIDTH), row),
            pl.BlockSpec((tm, SSM_WIDTH), row),
            pl.BlockSpec((tm, 2 * D_MODEL), row),
        ],
        out_shape=[
            jax.ShapeDtypeStruct((n, ATT_WIDTH), BF16),
            jax.ShapeDtypeStruct((n, ATT_WIDTH), BF16),
            jax.ShapeDtypeStruct((n, ATT_WIDTH), BF16),
            jax.ShapeDtypeStruct((n, SSM_WIDTH), BF16),
            jax.ShapeDtypeStruct((n, 2 * D_MODEL), BF16),
        ],
        compiler_params=_params(("parallel",)),
        name="in_proj",
    )(x2, g_mix.reshape(1, D_MODEL), w_in, rope)


def _col_reduce(x, pair_op, reduce_op):
    r = x.shape[0]
    rb = min(ATTN_ROW_BLOCK, r)
    out = None
    for i in range(r // rb):
        blk = x[i * rb:(i + 1) * rb]
        n = rb
        while n > SUBLANES:
            n //= 2
            blk = pair_op(blk[:n], blk[n:])
        out = blk if out is None else pair_op(out, blk)
    return reduce_op(out, axis=0, keepdims=True)


def _attn_kernel(lam_ref, q_ref, k_ref, v_ref, g_ref, o_ref, vt_ref, r_ref, l_ref, acc_ref, p_ref, be_ref, fl_ref,
                 *, tq, tk, nk):
    qi = pl.program_id(2)

    @pl.when(qi == 0)
    def _():
        def tr(j, c):
            off = pl.multiple_of(j * tk, tk)
            vt_ref[:, pl.ds(off, tk)] = v_ref[0, pl.ds(off, tk), :].astype(F32).T.astype(BF16)
            return c
        lax.fori_loop(0, nk, tr, 0)

    qt = q_ref[0].astype(F32).T
    row = lax.broadcasted_iota(jnp.int32, qt.shape, 0)
    qm = jnp.concatenate([jnp.where(row < HEAD_DIM, qt, 0.0),
                          jnp.where(row >= HEAD_DIM, qt, 0.0)], axis=1).astype(BF16)

    def scores(j):
        off = pl.multiple_of(j * tk, tk)
        return jnp.dot(k_ref[0, pl.ds(off, tk), :], qm, preferred_element_type=F32)

    def values(j, slot):
        off = pl.multiple_of(j * tk, tk)
        pv = jnp.dot(vt_ref[:, pl.ds(off, tk)], p_ref[slot], preferred_element_type=F32)
        acc_ref[...] = be_ref[slot] * (acc_ref[...] + pv)

    rb = min(ATTN_ROW_BLOCK, tk)
    head = jnp.dot(k_ref[0, 0:rb, :], qm, preferred_element_type=F32)
    r_ref[...] = _col_reduce(head, jnp.maximum, jnp.max)
    l_ref[...] = jnp.zeros(l_ref.shape, F32)
    fl_ref[...] = jnp.zeros(fl_ref.shape, F32)
    acc_ref[...] = jnp.zeros(acc_ref.shape, F32)

    def step(j, slot, with_values=True):
        s = scores(j)
        ref = r_ref[...]
        p = jnp.exp2(s - ref)
        cm = _col_reduce(s, jnp.maximum, jnp.max)
        new = jnp.maximum(ref, cm)
        beta = jnp.exp2(ref - new)
        fl_ref[...] = jnp.maximum(fl_ref[...], cm - ref)
        l_ref[...] = beta * (l_ref[...] + _col_reduce(p, jnp.add, jnp.sum))
        r_ref[...] = new
        be_ref[slot] = beta
        p_ref[slot] = p.astype(BF16)
        if with_values:
            values(j - 1, 1 - slot)

    step(0, 0, with_values=False)
    if nk > 1:
        def body(i, c):
            step(2 * i + 1, 1)
            step(2 * i + 2, 0)
            return c
        lax.fori_loop(0, (nk - 2) // 2, body, 0)
        step(nk - 1, 1)
    values(nk - 1, (nk - 1) % 2)

    @pl.when(jnp.max(fl_ref[...]) > ATTN_OVERFLOW_MARGIN)
    def _():
        r_ref[...] = jnp.full(r_ref.shape, NEG_BIG, F32)
        l_ref[...] = jnp.zeros(l_ref.shape, F32)
        acc_ref[...] = jnp.zeros(acc_ref.shape, F32)

        def safe(j, c):
            s = scores(j)
            m_old = r_ref[...]
            m_new = jnp.maximum(m_old, _col_reduce(s, jnp.maximum, jnp.max))
            alpha = jnp.exp2(m_old - m_new)
            p = jnp.exp2(s - m_new)
            l_ref[...] = alpha * l_ref[...] + _col_reduce(p, jnp.add, jnp.sum)
            r_ref[...] = m_new
            off = pl.multiple_of(j * tk, tk)
            pv = jnp.dot(vt_ref[:, pl.ds(off, tk)], p.astype(BF16), preferred_element_type=F32)
            acc_ref[...] = alpha * acc_ref[...] + pv
            return c
        lax.fori_loop(0, nk, safe, 0)

    lam = lam_ref[0]
    o_all = acc_ref[...] * (1.0 / l_ref[...])
    o = o_all[:, :tq] - lam * o_all[:, tq:]
    o = o * lax.rsqrt(jnp.mean(o * o, axis=0, keepdims=True) + EPS) * g_ref[...]
    o_ref[0] = (o * (1.0 - LAM_INIT)).T.astype(BF16)


def _attention(q, k, v, lam, subln_g):
    bsz, seq, _ = q.shape
    tq_pref, tk_pref = ATTN_TILES_SHORT if seq <= ATTN_SHORT_SEQ else ATTN_TILES_LONG
    tq = _pick_tile(seq, tq_pref)
    tk = _pick_tile(seq, tk_pref)
    nk = seq // tk
    assert nk == 1 or nk % 2 == 0
    kern = functools.partial(_attn_kernel, tq=tq, tk=tk, nk=nk)
    return pl.pallas_call(
        kern,
        grid=(bsz, N_HEADS, seq // tq),
        in_specs=[
            pl.BlockSpec(memory_space=pltpu.SMEM),
            pl.BlockSpec((1, tq, LANES), lambda b, h, i: (b, i, h)),
            pl.BlockSpec((1, seq, LANES), lambda b, h, i: (b, 0, h)),
            pl.BlockSpec((1, seq, LANES), lambda b, h, i: (b, 0, h)),
            pl.BlockSpec((ATT_V_DIM, 1), lambda b, h, i: (0, 0)),
        ],
        out_specs=pl.BlockSpec((1, tq, LANES), lambda b, h, i: (b, i, h)),
        out_shape=jax.ShapeDtypeStruct((bsz, seq, ATT_WIDTH), BF16),
        scratch_shapes=[
            pltpu.VMEM((ATT_V_DIM, seq), BF16),
            pltpu.VMEM((1, 2 * tq), F32),
            pltpu.VMEM((1, 2 * tq), F32),
            pltpu.VMEM((ATT_V_DIM, 2 * tq), F32),
            pltpu.VMEM((2, tk, 2 * tq), BF16),
            pltpu.VMEM((2, 1, 2 * tq), F32),
            pltpu.VMEM((1, 2 * tq), F32),
        ],
        compiler_params=_params(("parallel", "parallel", "arbitrary")),
        name="diff_attn",
    )(lam.reshape(1), q, k, v, subln_g.reshape(ATT_V_DIM, 1))


def _ssm_weights(lam_re, lam_im, log_dt, b_re, b_im, c_re, c_im, d_skip):
    T, G, P, CH = SSM_CHUNK, SSM_GROUPS, SSM_STATE, SSM_CH
    n = jnp.arange(T + 1, dtype=F32)[:, None, None]
    pw_re, pw_im, bb_re, bb_im = [], [], [], []
    for dirn in range(2):
        dt = jnp.exp(log_dt[dirn])[:, None]
        lr, li = lam_re[dirn], lam_im[dirn]
        mag = jnp.exp(lr * dt)
        a_re, a_im = mag * jnp.cos(li * dt), mag * jnp.sin(li * dt)
        den = lr * lr + li * li
        n_re = a_re - 1.0
        q_re = (n_re * lr + a_im * li) / den
        q_im = (a_im * lr - n_re * li) / den
        bb_re.append(q_re[..., None] * b_re[dirn] - q_im[..., None] * b_im[dirn])
        bb_im.append(q_re[..., None] * b_im[dirn] + q_im[..., None] * b_re[dirn])
        magn = jnp.exp(n * (lr * dt)[None])
        pw_re.append(magn * jnp.cos(n * (li * dt)[None]))
        pw_im.append(magn * jnp.sin(n * (li * dt)[None]))

    def cmul(ar, ai, br, bi):
        return ar * br - ai * bi, ar * bi + ai * br

    kern = []
    for dirn in range(2):
        car, cai = cmul(c_re[dirn][None], c_im[dirn][None],
                        pw_re[dirn][:T, :, None, :], pw_im[dirn][:T, :, None, :])
        kern.append(jnp.einsum('dgcp,gpk->dgck', car, bb_re[dirn], precision=HIGHEST)
                    - jnp.einsum('dgcp,gpk->dgck', cai, bb_im[dirn], precision=HIGHEST))
    tj = jnp.arange(T)
    lag = tj[None, :] - tj[:, None]
    mf = jnp.where((lag >= 0)[:, :, None, None, None], kern[0][jnp.clip(lag, 0, T - 1)], 0.0)
    mb = jnp.where((lag <= 0)[:, :, None, None, None], kern[1][jnp.clip(-lag, 0, T - 1)], 0.0)
    dsk = (lag == 0)[:, :, None, None, None] * (jnp.eye(CH, dtype=F32)[None, None, None] * d_skip[None, None, :, :, None])
    m_all = (mf + mb + dsk).astype(BF16).transpose(2, 0, 4, 1, 3).reshape(G, T * CH, T * CH)

    wf_re, wf_im = cmul(pw_re[0][T - 1 - tj][:, :, :, None], pw_im[0][T - 1 - tj][:, :, :, None],
                        bb_re[0][None], bb_im[0][None])
    wb_re, wb_im = cmul(pw_re[1][tj][:, :, :, None], pw_im[1][tj][:, :, :, None], bb_re[1][None], bb_im[1][None])
    w_all = jnp.stack([wf_re, wf_im, wb_re, wb_im], axis=0).astype(BF16).transpose(2, 1, 4, 0, 3)
    w_all = w_all.reshape(G // 2, 2, T * CH, 4, P)
    eye2 = jnp.eye(2, dtype=BF16)
    w_pair = (w_all[:, :, :, :, None, :] * eye2[None, :, None, None, :, None]).reshape(G // 2, 2 * T * CH, 4 * 2 * P)

    zf_re, zf_im = cmul(c_re[0][None], c_im[0][None], pw_re[0][1:T + 1, :, None, :], pw_im[0][1:T + 1, :, None, :])
    zb_re, zb_im = cmul(c_re[1][None], c_im[1][None],
                        pw_re[1][T - tj][:, :, None, :], pw_im[1][T - tj][:, :, None, :])
    z_all = jnp.stack([zf_re, -zf_im, zb_re, -zb_im], axis=0).astype(BF16).transpose(2, 0, 4, 1, 3)
    z_all = z_all.reshape(G // 2, 2, 4, P, T * CH)
    z_pair = (z_all.transpose(0, 2, 1, 3, 4)[:, :, :, :, None, :] * eye2[None, None, :, None, :, None])
    z_pair = z_pair.reshape(G // 2, 4 * 2 * P, 2 * T * CH)

    a_tab = []
    for dirn in range(2):
        dt = jnp.exp(log_dt[dirn])[:, None]
        order = jnp.arange(SUBLANES) if dirn == 0 else SUBLANES - 1 - jnp.arange(SUBLANES)
        nn = jnp.concatenate([order, jnp.array([1, 2, 4, 8])]).astype(F32)[:, None, None] * T
        magn = jnp.exp(nn * (lam_re[dirn] * dt)[None])
        ang = nn * (lam_im[dirn] * dt)[None]
        tab = jnp.stack([magn * jnp.cos(ang), magn * jnp.sin(ang)], axis=0)
        tab = jnp.pad(tab, ((0, 0), (0, 2 * SUBLANES - tab.shape[1]), (0, 0), (0, 0)))
        a_tab.append(tab.reshape(2, 2 * SUBLANES, G // 2, 2 * P).transpose(2, 0, 1, 3))
    a_tab = jnp.stack(a_tab, axis=1)
    return m_all, w_pair, z_pair, a_tab


def _tile_scan(x_re, x_im, a_tab, reverse):
    row = lax.broadcasted_iota(jnp.int32, x_re.shape, 0)
    for lvl, d in enumerate((1, 2, 4)):
        ar = a_tab[0][SUBLANES + lvl:SUBLANES + lvl + 1, :]
        ai = a_tab[1][SUBLANES + lvl:SUBLANES + lvl + 1, :]
        if reverse:
            keep = row < SUBLANES - d
            shift = SUBLANES - d
        else:
            keep = row >= d
            shift = d
        sr = jnp.where(keep, pltpu.roll(x_re, shift, 0), 0.0)
        si = jnp.where(keep, pltpu.roll(x_im, shift, 0), 0.0)
        x_re, x_im = x_re + ar * sr - ai * si, x_im + ar * si + ai * sr
    return x_re, x_im


def _ssm_kernel(u_ref, w_ref, m_ref, z_ref, a_ref, y_ref, v_ref, s_ref, *, nc):
    pl_ = SSM_PAIR_LANES
    nt = nc // SUBLANES
    u0 = u_ref[0, 0]
    u1 = u_ref[0, 1]
    v_ref[...] = jnp.dot(jnp.concatenate([u0, u1], axis=1), w_ref[0], preferred_element_type=F32)

    row = lax.broadcasted_iota(jnp.int32, (SUBLANES, pl_), 0)

    def tile_step(i, carry):
        new = []
        for dirn in range(2):
            c_re, c_im = carry[2 * dirn], carry[2 * dirn + 1]
            a_tab = (a_ref[0, dirn, 0], a_ref[0, dirn, 1])
            t = i if dirn == 0 else nt - 1 - i
            r0 = pl.multiple_of(t * SUBLANES, SUBLANES)
            lo = 2 * dirn * pl_
            x_re, x_im = _tile_scan(v_ref[pl.ds(r0, SUBLANES), lo:lo + pl_],
                                    v_ref[pl.ds(r0, SUBLANES), lo + pl_:lo + 2 * pl_], a_tab, dirn == 1)
            if dirn == 0:
                keep, shift, last = row >= 1, 1, SUBLANES - 1
            else:
                keep, shift, last = row < SUBLANES - 1, SUBLANES - 1, 0
            pr, pi = a_tab[0][0:SUBLANES, :], a_tab[1][0:SUBLANES, :]
            s_ref[pl.ds(r0, SUBLANES), lo:lo + pl_] = (
                pr * c_re - pi * c_im + jnp.where(keep, pltpu.roll(x_re, shift, 0), 0.0))
            s_ref[pl.ds(r0, SUBLANES), lo + pl_:lo + 2 * pl_] = (
                pr * c_im + pi * c_re + jnp.where(keep, pltpu.roll(x_im, shift, 0), 0.0))
            a8r = a_tab[0][SUBLANES + 3:SUBLANES + 4, :]
            a8i = a_tab[1][SUBLANES + 3:SUBLANES + 4, :]
            new.append(a8r * c_re - a8i * c_im + x_re[last:last + 1, :])
            new.append(a8r * c_im + a8i * c_re + x_im[last:last + 1, :])
        return tuple(new)

    zero = jnp.zeros((1, pl_), F32)
    lax.fori_loop(0, nt, tile_step, (zero, zero, zero, zero))

    ys = jnp.dot(s_ref[...].astype(BF16), z_ref[0], preferred_element_type=F32)
    half = ys.shape[1] // 2
    y_ref[0, 0] = jax.nn.gelu(ys[:, :half] + jnp.dot(u0, m_ref[0], preferred_element_type=F32)).astype(BF16)
    y_ref[0, 1] = jax.nn.gelu(ys[:, half:] + jnp.dot(u1, m_ref[1], preferred_element_type=F32)).astype(BF16)


def _ssm(s_in, bsz, seq, weights):
    m_all, w_pair, z_pair, a_pair = weights
    T, G, CH = SSM_CHUNK, SSM_GROUPS, SSM_CH
    nc = seq // T
    tc = T * CH
    ug = s_in.reshape(bsz, nc, T, G, CH).transpose(0, 3, 1, 2, 4).reshape(bsz, G, nc, tc)
    kern = functools.partial(_ssm_kernel, nc=nc)
    yg = pl.pallas_call(
        kern,
        grid=(G // 2, bsz),
        in_specs=[
            pl.BlockSpec((1, 2, nc, tc), lambda g, b: (b, g, 0, 0)),
            pl.BlockSpec((1, 2 * tc, 4 * SSM_PAIR_LANES), lambda g, b: (g, 0, 0)),
            pl.BlockSpec((2, tc, tc), lambda g, b: (g, 0, 0)),
            pl.BlockSpec((1, 4 * SSM_PAIR_LANES, 2 * tc), lambda g, b: (g, 0, 0)),
            pl.BlockSpec((1, 2, 2, 2 * SUBLANES, SSM_PAIR_LANES), lambda g, b: (g, 0, 0, 0, 0)),
        ],
        out_specs=pl.BlockSpec((1, 2, nc, tc), lambda g, b: (b, g, 0, 0)),
        out_shape=jax.ShapeDtypeStruct((bsz, G, nc, tc), BF16),
        scratch_shapes=[
            pltpu.VMEM((nc, 4 * SSM_PAIR_LANES), F32),
            pltpu.VMEM((nc, 4 * SSM_PAIR_LANES), F32),
        ],
        compiler_params=_params(("parallel", "arbitrary")),
        name="s5_scan",
    )(ug, w_pair, m_all, z_pair, a_pair)
    return yg.reshape(bsz, G, nc, T, CH).transpose(0, 2, 3, 1, 4).reshape(bsz * seq, SSM_WIDTH)


def _mix_kernel(h_ref, o_ref, yg_ref, gl_ref, wglu_ref, bglu_ref, wa_ref, ws_ref, wo_ref, out_ref):
    yg = yg_ref[...]
    z = jnp.dot(yg, wglu_ref[...], preferred_element_type=F32) + bglu_ref[...]
    ysg = (yg.astype(F32) * _sigmoid(z)).astype(BF16)
    br_a = jnp.dot(o_ref[...], wa_ref[...], preferred_element_type=F32)
    br_s = jnp.dot(ysg, ws_ref[...], preferred_element_type=F32)
    gl = gl_ref[...].astype(F32)
    mix = _sigmoid(gl[:, :D_MODEL]) * br_a + _sigmoid(gl[:, D_MODEL:]) * br_s
    out_ref[...] = h_ref[...] + jnp.dot(mix.astype(BF16), wo_ref[...], preferred_element_type=F32)


def _mix(h, o, ys, gl, w_glu, b_glu, w_attn_br, w_ssm_br, w_out):
    n = h.shape[0]
    tm = _pick_tile(n, 512)
    row = lambda i: (i, 0)
    fixed = lambda i: (0, 0)
    return pl.pallas_call(
        _mix_kernel,
        grid=(n // tm,),
        in_specs=[
            pl.BlockSpec((tm, D_MODEL), row),
            pl.BlockSpec((tm, ATT_WIDTH), row),
            pl.BlockSpec((tm, SSM_WIDTH), row),
            pl.BlockSpec((tm, 2 * D_MODEL), row),
            pl.BlockSpec((SSM_WIDTH, SSM_WIDTH), fixed),
            pl.BlockSpec((1, SSM_WIDTH), fixed),
            pl.BlockSpec((ATT_WIDTH, D_MODEL), fixed),
            pl.BlockSpec((SSM_WIDTH, D_MODEL), fixed),
            pl.BlockSpec((D_MODEL, D_MODEL), fixed),
        ],
        out_specs=pl.BlockSpec((tm, D_MODEL), row),
        out_shape=jax.ShapeDtypeStruct((n, D_MODEL), F32),
        compiler_params=_params(("parallel",)),
        name="branch_mix",
    )(h, o, ys, gl, w_glu, b_glu.reshape(1, SSM_WIDTH), w_attn_br, w_ssm_br, w_out)


def _lane_max(x):
    return jnp.max(x, axis=-1, keepdims=True)


def _router(u, wr_ref, br_ref):
    logits = jnp.dot(u, wr_ref[...], preferred_element_type=F32) + br_ref[...]
    lane = lax.broadcasted_iota(jnp.int32, logits.shape, 1).astype(F32)
    far = float(4 * LANES)
    is_grp = (lane >= N_EXPERTS) & (lane < N_EXPERTS + N_GROUPS)
    lg = jnp.where(is_grp, logits, NEG_BIG)
    g_max = _lane_max(lg)
    p_sel = 1.0 / jnp.sum(jnp.where(is_grp, jnp.exp(lg - g_max), 0.0), axis=-1, keepdims=True)
    g_sel = jnp.min(jnp.where(lg == g_max, lane, far), axis=-1, keepdims=True) - N_EXPERTS
    g_lo = g_sel * EXPERTS_PER_GROUP
    in_grp = (lane >= g_lo) & (lane < g_lo + EXPERTS_PER_GROUP)
    l2 = jnp.where(in_grp, logits, NEG_BIG)
    m1 = _lane_max(l2)
    i1 = jnp.min(jnp.where(l2 == m1, lane, far), axis=-1, keepdims=True)
    l2b = jnp.where(lane == i1, NEG_BIG, l2)
    m2 = _lane_max(l2b)
    i2 = jnp.min(jnp.where(l2b == m2, lane, far), axis=-1, keepdims=True)
    e2 = jnp.exp(m2 - m1)
    w1 = 1.0 / (1.0 + e2)
    w2 = e2 * w1
    gates = p_sel * (jnp.where(lane == i1, w1, 0.0) + jnp.where(lane == i2, w2, 0.0))
    return gates, jnp.where(lane == g_sel, 1.0, 0.0)


def _moe_kernel(h_ref, g_ref, wr_ref, br_ref, tri_ref, w13_ref, w2_ref, out_ref,
                ts_ref, gs_ref, acc_ref, pt_ref, meta_ref, *, tm, win, eps):
    step = pl.program_id(1)

    @pl.when(step == 0)
    def _():
        u = _rms(h_ref[...], g_ref[...]).astype(BF16)
        gates, onehot = _router(u, wr_ref, br_ref)
        rank = jnp.dot(tri_ref[...], onehot.astype(BF16), preferred_element_type=F32)
        cnt = jnp.sum(onehot, axis=0, keepdims=True)
        off = pltpu.roll(cnt, 1, 1) + pltpu.roll(cnt, 2, 1) + pltpu.roll(cnt, 3, 1)
        pos = jnp.sum(onehot * (off + rank), axis=-1, keepdims=True)
        slot_lane = lax.broadcasted_iota(jnp.int32, (tm, tm), 1).astype(F32)
        pt_ref[...] = jnp.where(slot_lane == pos, 1.0, 0.0).astype(BF16)
        pos_row = jnp.broadcast_to(pos, (tm, LANES)).T[0:1, :]
        slot_sub = lax.broadcasted_iota(jnp.int32, (tm, tm), 0).astype(F32)
        perm = jnp.where(slot_sub == pos_row, 1.0, 0.0).astype(BF16)
        ts_ref[0:tm, :] = jnp.dot(perm, u, preferred_element_type=F32).astype(BF16)
        g_hi = gates.astype(BF16)
        g_lo = (gates - g_hi.astype(F32)).astype(BF16)
        gs_ref[0:tm, :] = (jnp.dot(perm, g_hi, preferred_element_type=F32)
                           + jnp.dot(perm, g_lo, preferred_element_type=F32))
        ts_ref[tm:tm + win, :] = jnp.zeros((win, D_MODEL), BF16)
        gs_ref[tm:tm + win, :] = jnp.zeros((win, LANES), F32)
        acc_ref[...] = jnp.zeros(acc_ref.shape, F32)
        off_i = off.astype(jnp.int32)
        end_i = (off + cnt).astype(jnp.int32)
        for g in range(N_GROUPS):
            meta_ref[0, g] = off_i[0, g]
            meta_ref[1, g] = end_i[0, g]

    e0 = step * eps
    grp = lax.shift_right_logical(e0, 3)
    start = lax.shift_left(lax.shift_right_logical(meta_ref[0, grp], 4), 4)
    span = meta_ref[1, grp] - start
    nwin = 1 + sum((span > k * win).astype(jnp.int32) for k in range(1, tm // win + 2))

    def window(i, c):
        r0 = pl.multiple_of(start + i * win, 16)
        x = ts_ref[pl.ds(r0, win), :]
        gsl = gs_ref[pl.ds(r0, win), :]
        lane = lax.broadcasted_iota(jnp.int32, gsl.shape, 1)
        y = None
        for k in range(eps):
            ab = jnp.dot(x, w13_ref[k], preferred_element_type=F32)
            a = ab[:, :EXPERT_FF]
            hdn = a * _sigmoid(a) * ab[:, EXPERT_FF:]
            ge = jnp.sum(jnp.where(lane == e0 + k, gsl, 0.0), axis=-1, keepdims=True)
            yk = jnp.dot((hdn * ge).astype(BF16), w2_ref[k], preferred_element_type=F32)
            y = yk if y is None else y + yk
        acc_ref[pl.ds(r0, win), :] += y
        return c
    lax.fori_loop(0, nwin, window, 0)

    @pl.when(step == pl.num_programs(1) - 1)
    def _():
        out_ref[...] = h_ref[...] + jnp.dot(pt_ref[...], acc_ref[0:tm, :].astype(BF16), preferred_element_type=F32)


def _moe(h, g_ffn, w_router, b_router, w13, w2, win=MOE_WINDOW, eps=MOE_EXPERTS_PER_STEP):
    n = h.shape[0]
    tm = _pick_tile(n, 1024)
    tri = jnp.tri(tm, tm, -1, dtype=BF16)
    row = lambda i, e: (i, 0)
    fixed = lambda i, e: (0, 0)
    kern = functools.partial(_moe_kernel, tm=tm, win=win, eps=eps)
    return pl.pallas_call(
        kern,
        grid=(n // tm, N_EXPERTS // eps),
        in_specs=[
            pl.BlockSpec((tm, D_MODEL), row),
            pl.BlockSpec((1, D_MODEL), fixed),
            pl.BlockSpec((D_MODEL, LANES), fixed),
            pl.BlockSpec((1, LANES), fixed),
            pl.BlockSpec((tm, tm), fixed),
            pl.BlockSpec((eps, D_MODEL, 2 * EXPERT_FF), lambda i, e: (e, 0, 0)),
            pl.BlockSpec((eps, EXPERT_FF, D_MODEL), lambda i, e: (e, 0, 0)),
        ],
        out_specs=pl.BlockSpec((tm, D_MODEL), row),
        out_shape=jax.ShapeDtypeStruct((n, D_MODEL), F32),
        scratch_shapes=[
            pltpu.VMEM((tm + win, D_MODEL), BF16),
            pltpu.VMEM((tm + win, LANES), F32),
            pltpu.VMEM((tm + win, D_MODEL), F32),
            pltpu.VMEM((tm, tm), BF16),
            pltpu.SMEM((2, N_GROUPS), jnp.int32),
        ],
        compiler_params=_params(("parallel", "arbitrary")),
        name="hier_moe",
    )(h, g_ffn.reshape(1, D_MODEL), w_router, b_router, tri, w13, w2)


def _ple_kernel(h_ref, p_ref, gp_ref, wg_ref, wp_ref, gf_ref, out_ref):
    h = h_ref[...]
    gate = _sigmoid(jnp.dot(_rms(h, gp_ref[...]).astype(BF16), wg_ref[...], preferred_element_type=F32))
    pe = jnp.dot(p_ref[...].astype(BF16), wp_ref[...], preferred_element_type=F32)
    out_ref[...] = _rms(h + gate * pe, gf_ref[...])


def _ple(h, p, g_ple, w_ple_gate, w_ple, g_final):
    n = h.shape[0]
    tm = _pick_tile(n, 512)
    row = lambda i: (i, 0)
    fixed = lambda i: (0, 0)
    return pl.pallas_call(
        _ple_kernel,
        grid=(n // tm,),
        in_specs=[
            pl.BlockSpec((tm, D_MODEL), row),
            pl.BlockSpec((tm, PLE_DIM), row),
            pl.BlockSpec((1, D_MODEL), fixed),
            pl.BlockSpec((D_MODEL, D_MODEL), fixed),
            pl.BlockSpec((PLE_DIM, D_MODEL), fixed),
            pl.BlockSpec((1, D_MODEL), fixed),
        ],
        out_specs=pl.BlockSpec((tm, D_MODEL), row),
        out_shape=jax.ShapeDtypeStruct((n, D_MODEL), F32),
        compiler_params=_params(("parallel",)),
        name="ple_final",
    )(h, p, g_ple.reshape(1, D_MODEL), w_ple_gate, w_ple, g_final.reshape(1, D_MODEL))


def _prep_weights(g_mix, w_in, lam_q1, lam_k1, lam_q2, lam_k2, subln_g,
                  ssm_lam_re, ssm_lam_im, ssm_log_dt, ssm_b_re, ssm_b_im, ssm_c_re, ssm_c_im, ssm_d,
                  w_glu, b_glu, w_attn_br, w_ssm_br, w_out, g_ffn, w_r1, b_r1, w_r2, b_r2,
                  w_e1, w_e3, w_e2, g_ple, w_ple_gate, w_ple, g_final):
    i = 0
    lam = (jnp.exp(jnp.sum(lam_q1[i] * lam_k1[i])) - jnp.exp(jnp.sum(lam_q2[i] * lam_k2[i])) + LAM_INIT)
    pad = LANES - N_EXPERTS - N_GROUPS
    w_router = jnp.concatenate([w_r2[i], w_r1[i], jnp.zeros((D_MODEL, pad), F32)], axis=1).astype(BF16)
    b_router = jnp.concatenate([b_r2[i], b_r1[i], jnp.zeros((pad,), F32)]).reshape(1, LANES)
    return dict(
        g_mix=g_mix[i], w_in=w_in[i].astype(BF16), lam=lam.astype(F32), subln_g=subln_g[i],
        ssm=_ssm_weights(ssm_lam_re[i], ssm_lam_im[i], ssm_log_dt[i], ssm_b_re[i], ssm_b_im[i],
                         ssm_c_re[i], ssm_c_im[i], ssm_d[i]),
        w_glu=w_glu[i].astype(BF16), b_glu=b_glu[i], w_attn_br=w_attn_br[i].astype(BF16),
        w_ssm_br=w_ssm_br[i].astype(BF16), w_out=w_out[i].astype(BF16), g_ffn=g_ffn[i],
        w_router=w_router, b_router=b_router,
        w13=jnp.concatenate([w_e1[i], w_e3[i]], axis=-1).astype(BF16), w2=w_e2[i].astype(BF16),
        g_ple=g_ple[i], w_ple_gate=w_ple_gate[i].astype(BF16), w_ple=w_ple[i].astype(BF16), g_final=g_final,
    )


def _trunk(x, p, w):
    bsz, seq, _ = x.shape
    n = bsz * seq
    x2 = x.reshape(n, D_MODEL)
    q, k, v, s_in, gl = _in_proj(x2, seq, w["g_mix"], w["w_in"], w["rope"])
    shp = (bsz, seq, ATT_WIDTH)
    o = _attention(q.reshape(shp), k.reshape(shp), v.reshape(shp), w["lam"], w["subln_g"])
    ys = _ssm(s_in, bsz, seq, w["ssm"])
    h = _mix(x2, o.reshape(n, ATT_WIDTH), ys, gl, w["w_glu"], w["b_glu"], w["w_attn_br"], w["w_ssm_br"], w["w_out"])
    h = _moe(h, w["g_ffn"], w["w_router"], w["b_router"], w["w13"], w["w2"])
    y = _ple(h, p[0].reshape(n, PLE_DIM), w["g_ple"], w["w_ple_gate"], w["w_ple"], w["g_final"])
    return y.reshape(bsz, seq, D_MODEL)


def kernel(x_prompt, x_sample, p_prompt, p_sample, g_mix, w_in, lam_q1, lam_k1, lam_q2, lam_k2, subln_g, ssm_lam_re, ssm_lam_im, ssm_log_dt, ssm_b_re, ssm_b_im, ssm_c_re, ssm_c_im, ssm_d, w_glu, b_glu, w_attn_br, w_ssm_br, w_out, g_ffn, w_r1, b_r1, w_r2, b_r2, w_e1, w_e3, w_e2, g_ple, w_ple_gate, w_ple, g_final):
    w = _prep_weights(g_mix, w_in, lam_q1, lam_k1, lam_q2, lam_k2, subln_g,
                      ssm_lam_re, ssm_lam_im, ssm_log_dt, ssm_b_re, ssm_b_im, ssm_c_re, ssm_c_im, ssm_d,
                      w_glu, b_glu, w_attn_br, w_ssm_br, w_out, g_ffn, w_r1, b_r1, w_r2, b_r2,
                      w_e1, w_e3, w_e2, g_ple, w_ple_gate, w_ple, g_final)
    w["rope"] = _rope_tables(max(x_prompt.shape[1], x_sample.shape[1]))
    return (_trunk(x_prompt, p_prompt, w), _trunk(x_sample, p_sample, w))
```

```python
import functools
import math

import jax
import jax.numpy as jnp
from jax import lax
from jax.experimental import pallas as pl
from jax.experimental.pallas import tpu as pltpu

F32 = jnp.float32
BF16 = jnp.bfloat16
HIGHEST = lax.Precision.HIGHEST

D_MODEL = 1024
N_HEADS = 8
HEAD_DIM = 64
ATT_V_DIM = 2 * HEAD_DIM
ATT_WIDTH = N_HEADS * ATT_V_DIM
ROT_DIM = HEAD_DIM // 4
ROPE_THETA = 500000.0
SSM_CH = 16
SSM_GROUPS = 32
SSM_WIDTH = SSM_GROUPS * SSM_CH
SSM_STATE = 64
N_GROUPS = 4
EXPERTS_PER_GROUP = 8
N_EXPERTS = N_GROUPS * EXPERTS_PER_GROUP
EXPERT_FF = 256
PLE_DIM = 256
EPS = 1e-6
LAM_INIT = 0.8 - 0.6 * math.exp(-0.3 * 0)

LANES = 128
SUBLANES = 8
SSM_CHUNK = 16
SSM_PAIR_LANES = 2 * SSM_STATE
SSM_PAIR_CH = 2 * SSM_CH
SSM_PIECES = 128 // SSM_PAIR_CH
VMEM_LIMIT = 56 * 1024 * 1024
NEG_BIG = -1e30
LOG2E = math.log2(math.e)
ATTN_ROW_BLOCK = 64
ATTN_OVERFLOW_MARGIN = 100.0
ATTN_SHORT_SEQ = 2048
ATTN_TILES_SHORT = (2048, 512)
ATTN_TILES_LONG = (1024, 1024)
MOE_WINDOW = 320
MOE_EXPERTS_PER_STEP = 4


def _params(sem):
    return pltpu.CompilerParams(dimension_semantics=sem, vmem_limit_bytes=VMEM_LIMIT)


def _rms(x, g):
    return x * lax.rsqrt(jnp.mean(x * x, axis=-1, keepdims=True) + EPS) * g


def _sigmoid(x):
    return 1.0 / (1.0 + jnp.exp(-x))


def _pick_tile(n, pref):
    t = min(n, pref)
    while n % t:
        t //= 2
    return t


def _lane_piece(rows):
    return lax.shift_right_logical(lax.broadcasted_iota(jnp.int32, (rows, LANES), 1), SSM_PAIR_CH.bit_length() - 1)


def _to_chunk_major(tok_ref, out_ref, rows):
    T = SSM_CHUNK
    piece_id = _lane_piece(rows)
    for k in range(T // SSM_PIECES):
        xs = [[tok_ref[col, pl.ds(SSM_PIECES * k + m, rows, stride=T), :] for col in range(SSM_WIDTH // LANES)]
              for m in range(SSM_PIECES)]
        for gp in range(SSM_GROUPS // 2):
            col, src = divmod(gp, SSM_PIECES)
            dest = None
            for m in range(SSM_PIECES):
                piece = xs[m][col]
                shift = ((m - src) * SSM_PAIR_CH) % LANES
                if shift:
                    piece = pltpu.roll(piece, shift, 1)
                dest = piece if dest is None else jnp.where(piece_id == m, piece, dest)
            out_ref[gp, :, k * LANES:(k + 1) * LANES] = dest.astype(out_ref.dtype)


def _to_token_major(chunk_ref, tok_ref, rows):
    T = SSM_CHUNK
    piece_id = _lane_piece(rows)
    for t in range(T):
        col, src = divmod(t, SSM_PIECES)
        for k in range(SSM_WIDTH // LANES):
            dest = None
            for m in range(SSM_PIECES):
                piece = chunk_ref[SSM_PIECES * k + m, :, col * LANES:(col + 1) * LANES].astype(F32)
                shift = ((m - src) * SSM_PAIR_CH) % LANES
                if shift:
                    piece = pltpu.roll(piece, shift, 1)
                dest = piece if dest is None else jnp.where(piece_id == m, piece, dest)
            tok_ref[k, pl.ds(t, rows, stride=T), :] = dest


def _in_proj_kernel(x_ref, g_ref, w_ref, rope_ref, q_ref, k_ref, v_ref, s_ref, gl_ref, tok_ref):
    u = _rms(x_ref[...], g_ref[...]).astype(BF16)
    cos = rope_ref[:, 0:LANES]
    sa = rope_ref[:, LANES:2 * LANES]
    sb = rope_ref[:, 2 * LANES:3 * LANES]
    for sec, out_ref, scale in ((0, q_ref, HEAD_DIM ** -0.5 * LOG2E), (1, k_ref, 1.0)):
        y = jnp.dot(u, w_ref[:, sec * ATT_WIDTH:(sec + 1) * ATT_WIDTH], preferred_element_type=F32)
        for h in range(N_HEADS):
            yh = y[:, h * LANES:(h + 1) * LANES]
            r = yh * cos + pltpu.roll(yh, LANES - ROT_DIM // 2, 1) * sa + pltpu.roll(yh, ROT_DIM // 2, 1) * sb
            out_ref[:, h * LANES:(h + 1) * LANES] = (r * scale).astype(BF16)
    c0 = 2 * ATT_WIDTH
    v_ref[...] = jnp.dot(u, w_ref[:, c0:c0 + ATT_WIDTH], preferred_element_type=F32).astype(BF16)
    c0 += ATT_WIDTH
    s_tok = jnp.dot(u, w_ref[:, c0:c0 + SSM_WIDTH], preferred_element_type=F32)
    for col in range(SSM_WIDTH // LANES):
        tok_ref[col] = s_tok[:, col * LANES:(col + 1) * LANES]
    _to_chunk_major(tok_ref, s_ref, tok_ref.shape[1] // SSM_CHUNK)
    c0 += SSM_WIDTH
    gl_ref[...] = jnp.dot(u, w_ref[:, c0:c0 + 2 * D_MODEL], preferred_element_type=F32).astype(BF16)


def _rope_tables(seq):
    half = ROT_DIM // 2
    inv = ROPE_THETA ** (-jnp.arange(0, ROT_DIM, 2, dtype=F32) / ROT_DIM)
    ang = jnp.arange(seq, dtype=F32)[:, None] * inv[None, :]
    cos, sin = jnp.cos(ang), jnp.sin(ang)
    one = jnp.ones((seq, HEAD_DIM - ROT_DIM), F32)
    zero = jnp.zeros((seq, HEAD_DIM - ROT_DIM), F32)
    zh = jnp.zeros((seq, half), F32)
    cos_t = [cos, cos, one]
    sa_t = [-sin, zh, zero]
    sb_t = [zh, sin, zero]
    return jnp.concatenate(cos_t * 2 + sa_t * 2 + sb_t * 2, axis=1)


def _in_proj(x2, seq, g_mix, w_in, rope):
    n = x2.shape[0]
    tm = _pick_tile(seq, 512)
    nl = seq // tm
    in_cols = w_in.shape[1]
    row = lambda i: (i, 0)
    fixed = lambda i: (0, 0)
    tab = lambda i: (i % nl, 0)
    return pl.pallas_call(
        _in_proj_kernel,
        grid=(n // tm,),
        in_specs=[
            pl.BlockSpec((tm, D_MODEL), row),
            pl.BlockSpec((1, D_MODEL), fixed),
            pl.BlockSpec((D_MODEL, in_cols), fixed),
            pl.BlockSpec((tm, 3 * LANES), tab),
        ],
        out_specs=[
            pl.BlockSpec((tm, ATT_WIDTH), row),
            pl.BlockSpec((tm, ATT_WIDTH), row),
            pl.BlockSpec((tm, ATT_WIDTH), row),
            pl.BlockSpec((SSM_GROUPS // 2, tm // SSM_CHUNK, SSM_CHUNK * SSM_PAIR_CH), lambda i: (0, i, 0)),
            pl.BlockSpec((tm, 2 * D_MODEL), row),
        ],
        out_shape=[
            jax.ShapeDtypeStruct((n, ATT_WIDTH), BF16),
            jax.ShapeDtypeStruct((n, ATT_WIDTH), BF16),
            jax.ShapeDtypeStruct((n, ATT_WIDTH), BF16),
            jax.ShapeDtypeStruct((SSM_GROUPS // 2, n // SSM_CHUNK, SSM_CHUNK * SSM_PAIR_CH), BF16),
            jax.ShapeDtypeStruct((n, 2 * D_MODEL), BF16),
        ],
        scratch_shapes=[pltpu.VMEM((SSM_WIDTH // LANES, tm, LANES), F32)],
        compiler_params=_params(("parallel",)),
        name="in_proj",
    )(x2, g_mix.reshape(1, D_MODEL), w_in, rope)


def _col_reduce(x, pair_op, reduce_op):
    r = x.shape[0]
    rb = min(ATTN_ROW_BLOCK, r)
    out = None
    for i in range(r // rb):
        blk = x[i * rb:(i + 1) * rb]
        n = rb
        while n > SUBLANES:
            n //= 2
            blk = pair_op(blk[:n], blk[n:])
        out = blk if out is None else pair_op(out, blk)
    return reduce_op(out, axis=0, keepdims=True)


def _attn_kernel(lam_ref, q_ref, k_ref, v_ref, g_ref, o_ref, vt_ref, r_ref, l_ref, acc_ref, p_ref, be_ref, fl_ref,
                 *, tq, tk, nk):
    qi = pl.program_id(2)

    @pl.when(qi == 0)
    def _():
        def tr(j, c):
            off = pl.multiple_of(j * tk, tk)
            vt_ref[:, pl.ds(off, tk)] = v_ref[0, pl.ds(off, tk), :].astype(F32).T.astype(BF16)
            return c
        lax.fori_loop(0, nk, tr, 0)

    qt = q_ref[0].astype(F32).T
    row = lax.broadcasted_iota(jnp.int32, qt.shape, 0)
    qm = jnp.concatenate([jnp.where(row < HEAD_DIM, qt, 0.0),
                          jnp.where(row >= HEAD_DIM, qt, 0.0)], axis=1).astype(BF16)

    def scores(j):
        off = pl.multiple_of(j * tk, tk)
        return jnp.dot(k_ref[0, pl.ds(off, tk), :], qm, preferred_element_type=F32)

    def values(j, slot):
        off = pl.multiple_of(j * tk, tk)
        pv = jnp.dot(vt_ref[:, pl.ds(off, tk)], p_ref[slot], preferred_element_type=F32)
        acc_ref[...] = be_ref[slot] * (acc_ref[...] + pv)

    rb = min(ATTN_ROW_BLOCK, tk)
    head = jnp.dot(k_ref[0, 0:rb, :], qm, preferred_element_type=F32)
    r_ref[...] = _col_reduce(head, jnp.maximum, jnp.max)
    l_ref[...] = jnp.zeros(l_ref.shape, F32)
    fl_ref[...] = jnp.zeros(fl_ref.shape, F32)
    acc_ref[...] = jnp.zeros(acc_ref.shape, F32)

    def step(j, slot, with_values=True):
        s = scores(j)
        ref = r_ref[...]
        p = jnp.exp2(s - ref)
        cm = _col_reduce(s, jnp.maximum, jnp.max)
        new = jnp.maximum(ref, cm)
        beta = jnp.exp2(ref - new)
        fl_ref[...] = jnp.maximum(fl_ref[...], cm - ref)
        l_ref[...] = beta * (l_ref[...] + _col_reduce(p, jnp.add, jnp.sum))
        r_ref[...] = new
        be_ref[slot] = beta
        p_ref[slot] = p.astype(BF16)
        if with_values:
            values(j - 1, 1 - slot)

    step(0, 0, with_values=False)
    if nk > 1:
        def body(i, c):
            step(2 * i + 1, 1)
            step(2 * i + 2, 0)
            return c
        lax.fori_loop(0, (nk - 2) // 2, body, 0)
        step(nk - 1, 1)
    values(nk - 1, (nk - 1) % 2)

    @pl.when(jnp.max(fl_ref[...]) > ATTN_OVERFLOW_MARGIN)
    def _():
        r_ref[...] = jnp.full(r_ref.shape, NEG_BIG, F32)
        l_ref[...] = jnp.zeros(l_ref.shape, F32)
        acc_ref[...] = jnp.zeros(acc_ref.shape, F32)

        def safe(j, c):
            s = scores(j)
            m_old = r_ref[...]
            m_new = jnp.maximum(m_old, _col_reduce(s, jnp.maximum, jnp.max))
            alpha = jnp.exp2(m_old - m_new)
            p = jnp.exp2(s - m_new)
            l_ref[...] = alpha * l_ref[...] + _col_reduce(p, jnp.add, jnp.sum)
            r_ref[...] = m_new
            off = pl.multiple_of(j * tk, tk)
            pv = jnp.dot(vt_ref[:, pl.ds(off, tk)], p.astype(BF16), preferred_element_type=F32)
            acc_ref[...] = alpha * acc_ref[...] + pv
            return c
        lax.fori_loop(0, nk, safe, 0)

    lam = lam_ref[0]
    o_all = acc_ref[...] * (1.0 / l_ref[...])
    o = o_all[:, :tq] - lam * o_all[:, tq:]
    o = o * lax.rsqrt(jnp.mean(o * o, axis=0, keepdims=True) + EPS) * g_ref[...]
    o_ref[0] = (o * (1.0 - LAM_INIT)).T.astype(BF16)


def _attention(q, k, v, lam, subln_g):
    bsz, seq, _ = q.shape
    tq_pref, tk_pref = ATTN_TILES_SHORT if seq <= ATTN_SHORT_SEQ else ATTN_TILES_LONG
    tq = _pick_tile(seq, tq_pref)
    tk = _pick_tile(seq, tk_pref)
    nk = seq // tk
    assert nk == 1 or nk % 2 == 0
    kern = functools.partial(_attn_kernel, tq=tq, tk=tk, nk=nk)
    return pl.pallas_call(
        kern,
        grid=(bsz, N_HEADS, seq // tq),
        in_specs=[
            pl.BlockSpec(memory_space=pltpu.SMEM),
            pl.BlockSpec((1, tq, LANES), lambda b, h, i: (b, i, h)),
            pl.BlockSpec((1, seq, LANES), lambda b, h, i: (b, 0, h)),
            pl.BlockSpec((1, seq, LANES), lambda b, h, i: (b, 0, h)),
            pl.BlockSpec((ATT_V_DIM, 1), lambda b, h, i: (0, 0)),
        ],
        out_specs=pl.BlockSpec((1, tq, LANES), lambda b, h, i: (b, i, h)),
        out_shape=jax.ShapeDtypeStruct((bsz, seq, ATT_WIDTH), BF16),
        scratch_shapes=[
            pltpu.VMEM((ATT_V_DIM, seq), BF16),
            pltpu.VMEM((1, 2 * tq), F32),
            pltpu.VMEM((1, 2 * tq), F32),
            pltpu.VMEM((ATT_V_DIM, 2 * tq), F32),
            pltpu.VMEM((2, tk, 2 * tq), BF16),
            pltpu.VMEM((2, 1, 2 * tq), F32),
            pltpu.VMEM((1, 2 * tq), F32),
        ],
        compiler_params=_params(("parallel", "parallel", "arbitrary")),
        name="diff_attn",
    )(lam.reshape(1), q, k, v, subln_g.reshape(ATT_V_DIM, 1))


def _ssm_weights(lam_re, lam_im, log_dt, b_re, b_im, c_re, c_im, d_skip):
    T, G, P, CH = SSM_CHUNK, SSM_GROUPS, SSM_STATE, SSM_CH
    n = jnp.arange(T + 1, dtype=F32)[:, None, None]
    pw_re, pw_im, bb_re, bb_im = [], [], [], []
    for dirn in range(2):
        dt = jnp.exp(log_dt[dirn])[:, None]
        lr, li = lam_re[dirn], lam_im[dirn]
        mag = jnp.exp(lr * dt)
        a_re, a_im = mag * jnp.cos(li * dt), mag * jnp.sin(li * dt)
        den = lr * lr + li * li
        n_re = a_re - 1.0
        q_re = (n_re * lr + a_im * li) / den
        q_im = (a_im * lr - n_re * li) / den
        bb_re.append(q_re[..., None] * b_re[dirn] - q_im[..., None] * b_im[dirn])
        bb_im.append(q_re[..., None] * b_im[dirn] + q_im[..., None] * b_re[dirn])
        magn = jnp.exp(n * (lr * dt)[None])
        pw_re.append(magn * jnp.cos(n * (li * dt)[None]))
        pw_im.append(magn * jnp.sin(n * (li * dt)[None]))

    def cmul(ar, ai, br, bi):
        return ar * br - ai * bi, ar * bi + ai * br

    kern = []
    for dirn in range(2):
        car, cai = cmul(c_re[dirn][None], c_im[dirn][None],
                        pw_re[dirn][:T, :, None, :], pw_im[dirn][:T, :, None, :])
        kern.append(jnp.einsum('dgcp,gpk->dgck', car, bb_re[dirn], precision=HIGHEST)
                    - jnp.einsum('dgcp,gpk->dgck', cai, bb_im[dirn], precision=HIGHEST))
    tj = jnp.arange(T)
    lag = tj[None, :] - tj[:, None]
    mf = jnp.where((lag >= 0)[:, :, None, None, None], kern[0][jnp.clip(lag, 0, T - 1)], 0.0)
    mb = jnp.where((lag <= 0)[:, :, None, None, None], kern[1][jnp.clip(-lag, 0, T - 1)], 0.0)
    dsk = (lag == 0)[:, :, None, None, None] * (jnp.eye(CH, dtype=F32)[None, None, None] * d_skip[None, None, :, :, None])
    eye2 = jnp.eye(2, dtype=BF16)
    m_all = (mf + mb + dsk).astype(BF16).reshape(T, T, G // 2, 2, CH, CH)
    m_pair = m_all.transpose(2, 0, 3, 5, 1, 4)[:, :, :, :, :, None, :] * eye2[None, None, :, None, None, :, None]
    m_pair = m_pair.reshape(G // 2, T * 2 * CH, T * 2 * CH)

    wf_re, wf_im = cmul(pw_re[0][T - 1 - tj][:, :, :, None], pw_im[0][T - 1 - tj][:, :, :, None],
                        bb_re[0][None], bb_im[0][None])
    wb_re, wb_im = cmul(pw_re[1][tj][:, :, :, None], pw_im[1][tj][:, :, :, None], bb_re[1][None], bb_im[1][None])
    w_all = jnp.stack([wf_re, wf_im, wb_re, wb_im], axis=0).astype(BF16).transpose(2, 1, 4, 0, 3)
    w_all = w_all.reshape(G // 2, 2, T, CH, 4, P).transpose(0, 2, 1, 3, 4, 5)
    w_pair = (w_all[:, :, :, :, :, None, :] * eye2[None, None, :, None, None, :, None]).reshape(G // 2, T * 2 * CH, 4 * 2 * P)

    zf_re, zf_im = cmul(c_re[0][None], c_im[0][None], pw_re[0][1:T + 1, :, None, :], pw_im[0][1:T + 1, :, None, :])
    zb_re, zb_im = cmul(c_re[1][None], c_im[1][None],
                        pw_re[1][T - tj][:, :, None, :], pw_im[1][T - tj][:, :, None, :])
    z_all = jnp.stack([zf_re, -zf_im, zb_re, -zb_im], axis=0).astype(BF16).transpose(2, 0, 4, 1, 3)
    z_all = z_all.reshape(G // 2, 2, 4, P, T, CH)
    z_pair = (z_all.transpose(0, 2, 1, 3, 4, 5)[:, :, :, :, :, None, :] * eye2[None, None, :, None, None, :, None])
    z_pair = z_pair.reshape(G // 2, 4 * 2 * P, T * 2 * CH)

    a_tab = []
    for dirn in range(2):
        dt = jnp.exp(log_dt[dirn])[:, None]
        order = jnp.arange(SUBLANES) if dirn == 0 else SUBLANES - 1 - jnp.arange(SUBLANES)
        nn = jnp.concatenate([order, jnp.array([1, 2, 4, 8])]).astype(F32)[:, None, None] * T
        magn = jnp.exp(nn * (lam_re[dirn] * dt)[None])
        ang = nn * (lam_im[dirn] * dt)[None]
        tab = jnp.stack([magn * jnp.cos(ang), magn * jnp.sin(ang)], axis=0)
        tab = jnp.pad(tab, ((0, 0), (0, 2 * SUBLANES - tab.shape[1]), (0, 0), (0, 0)))
        a_tab.append(tab.reshape(2, 2 * SUBLANES, G // 2, 2 * P).transpose(2, 0, 1, 3))
    a_tab = jnp.stack(a_tab, axis=1)
    return m_pair, w_pair, z_pair, a_tab


def _tile_scan(x_re, x_im, a_tab, reverse):
    row = lax.broadcasted_iota(jnp.int32, x_re.shape, 0)
    for lvl, d in enumerate((1, 2, 4)):
        ar = a_tab[0][SUBLANES + lvl:SUBLANES + lvl + 1, :]
        ai = a_tab[1][SUBLANES + lvl:SUBLANES + lvl + 1, :]
        if reverse:
            keep = row < SUBLANES - d
            shift = SUBLANES - d
        else:
            keep = row >= d
            shift = d
        sr = jnp.where(keep, pltpu.roll(x_re, shift, 0), 0.0)
        si = jnp.where(keep, pltpu.roll(x_im, shift, 0), 0.0)
        x_re, x_im = x_re + ar * sr - ai * si, x_im + ar * si + ai * sr
    return x_re, x_im


def _ssm_kernel(u_ref, w_ref, m_ref, z_ref, a_ref, y_ref, v_ref, s_ref, *, nc):
    pl_ = SSM_PAIR_LANES
    nt = nc // SUBLANES
    u = u_ref[0]
    v_ref[...] = jnp.dot(u, w_ref[0], preferred_element_type=F32)

    row = lax.broadcasted_iota(jnp.int32, (SUBLANES, pl_), 0)

    def tile_step(i, carry):
        new = []
        for dirn in range(2):
            c_re, c_im = carry[2 * dirn], carry[2 * dirn + 1]
            a_tab = (a_ref[0, dirn, 0], a_ref[0, dirn, 1])
            t = i if dirn == 0 else nt - 1 - i
            r0 = pl.multiple_of(t * SUBLANES, SUBLANES)
            lo = 2 * dirn * pl_
            x_re, x_im = _tile_scan(v_ref[pl.ds(r0, SUBLANES), lo:lo + pl_],
                                    v_ref[pl.ds(r0, SUBLANES), lo + pl_:lo + 2 * pl_], a_tab, dirn == 1)
            if dirn == 0:
                keep, shift, last = row >= 1, 1, SUBLANES - 1
            else:
                keep, shift, last = row < SUBLANES - 1, SUBLANES - 1, 0
            pr, pi = a_tab[0][0:SUBLANES, :], a_tab[1][0:SUBLANES, :]
            s_ref[pl.ds(r0, SUBLANES), lo:lo + pl_] = (
                pr * c_re - pi * c_im + jnp.where(keep, pltpu.roll(x_re, shift, 0), 0.0))
            s_ref[pl.ds(r0, SUBLANES), lo + pl_:lo + 2 * pl_] = (
                pr * c_im + pi * c_re + jnp.where(keep, pltpu.roll(x_im, shift, 0), 0.0))
            a8r = a_tab[0][SUBLANES + 3:SUBLANES + 4, :]
            a8i = a_tab[1][SUBLANES + 3:SUBLANES + 4, :]
            new.append(a8r * c_re - a8i * c_im + x_re[last:last + 1, :])
            new.append(a8r * c_im + a8i * c_re + x_im[last:last + 1, :])
        return tuple(new)

    zero = jnp.zeros((1, pl_), F32)
    lax.fori_loop(0, nt, tile_step, (zero, zero, zero, zero))

    ys = (jnp.dot(s_ref[...].astype(BF16), z_ref[0], preferred_element_type=F32)
          + jnp.dot(u, m_ref[0], preferred_element_type=F32))
    y_ref[0] = jax.nn.gelu(ys).astype(BF16)


def _ssm(ug, bsz, seq, weights):
    m_pair, w_pair, z_pair, a_tab = weights
    nc = seq // SSM_CHUNK
    tw = SSM_CHUNK * SSM_PAIR_CH
    kern = functools.partial(_ssm_kernel, nc=nc)
    return pl.pallas_call(
        kern,
        grid=(SSM_GROUPS // 2, bsz),
        in_specs=[
            pl.BlockSpec((1, nc, tw), lambda g, b: (g, b, 0)),
            pl.BlockSpec((1, tw, 4 * SSM_PAIR_LANES), lambda g, b: (g, 0, 0)),
            pl.BlockSpec((1, tw, tw), lambda g, b: (g, 0, 0)),
            pl.BlockSpec((1, 4 * SSM_PAIR_LANES, tw), lambda g, b: (g, 0, 0)),
            pl.BlockSpec((1, 2, 2, 2 * SUBLANES, SSM_PAIR_LANES), lambda g, b: (g, 0, 0, 0, 0)),
        ],
        out_specs=pl.BlockSpec((1, nc, tw), lambda g, b: (g, b, 0)),
        out_shape=jax.ShapeDtypeStruct(ug.shape, BF16),
        scratch_shapes=[
            pltpu.VMEM((nc, 4 * SSM_PAIR_LANES), F32),
            pltpu.VMEM((nc, 4 * SSM_PAIR_LANES), F32),
        ],
        compiler_params=_params(("parallel", "arbitrary")),
        name="s5_scan",
    )(ug, w_pair, m_pair, z_pair, a_tab)


def _mix_kernel(h_ref, o_ref, yg_ref, gl_ref, wglu_ref, bglu_ref, wa_ref, ws_ref, wo_ref, out_ref, tok_ref):
    _to_token_major(yg_ref, tok_ref, tok_ref.shape[1] // SSM_CHUNK)
    yg = jnp.concatenate([tok_ref[col] for col in range(SSM_WIDTH // LANES)], axis=1)
    z = jnp.dot(yg.astype(BF16), wglu_ref[...], preferred_element_type=F32) + bglu_ref[...]
    ysg = (yg * _sigmoid(z)).astype(BF16)
    br_a = jnp.dot(o_ref[...], wa_ref[...], preferred_element_type=F32)
    br_s = jnp.dot(ysg, ws_ref[...], preferred_element_type=F32)
    gl = gl_ref[...].astype(F32)
    mix = _sigmoid(gl[:, :D_MODEL]) * br_a + _sigmoid(gl[:, D_MODEL:]) * br_s
    out_ref[...] = h_ref[...] + jnp.dot(mix.astype(BF16), wo_ref[...], preferred_element_type=F32)


def _mix(h, o, ys, gl, w_glu, b_glu, w_attn_br, w_ssm_br, w_out):
    n = h.shape[0]
    tm = _pick_tile(n, 512)
    row = lambda i: (i, 0)
    fixed = lambda i: (0, 0)
    return pl.pallas_call(
        _mix_kernel,
        grid=(n // tm,),
        in_specs=[
            pl.BlockSpec((tm, D_MODEL), row),
            pl.BlockSpec((tm, ATT_WIDTH), row),
            pl.BlockSpec((SSM_GROUPS // 2, tm // SSM_CHUNK, SSM_CHUNK * SSM_PAIR_CH), lambda i: (0, i, 0)),
            pl.BlockSpec((tm, 2 * D_MODEL), row),
            pl.BlockSpec((SSM_WIDTH, SSM_WIDTH), fixed),
            pl.BlockSpec((1, SSM_WIDTH), fixed),
            pl.BlockSpec((ATT_WIDTH, D_MODEL), fixed),
            pl.BlockSpec((SSM_WIDTH, D_MODEL), fixed),
            pl.BlockSpec((D_MODEL, D_MODEL), fixed),
        ],
        out_specs=pl.BlockSpec((tm, D_MODEL), row),
        out_shape=jax.ShapeDtypeStruct((n, D_MODEL), F32),
        scratch_shapes=[pltpu.VMEM((SSM_WIDTH // LANES, tm, LANES), F32)],
        compiler_params=_params(("parallel",)),
        name="branch_mix",
    )(h, o, ys, gl, w_glu, b_glu.reshape(1, SSM_WIDTH), w_attn_br, w_ssm_br, w_out)


def _lane_max(x):
    return jnp.max(x, axis=-1, keepdims=True)


def _router(u, wr_ref, br_ref):
    logits = jnp.dot(u, wr_ref[...], preferred_element_type=F32) + br_ref[...]
    lane = lax.broadcasted_iota(jnp.int32, logits.shape, 1).astype(F32)
    far = float(4 * LANES)
    is_grp = (lane >= N_EXPERTS) & (lane < N_EXPERTS + N_GROUPS)
    lg = jnp.where(is_grp, logits, NEG_BIG)
    g_max = _lane_max(lg)
    p_sel = 1.0 / jnp.sum(jnp.where(is_grp, jnp.exp(lg - g_max), 0.0), axis=-1, keepdims=True)
    g_sel = jnp.min(jnp.where(lg == g_max, lane, far), axis=-1, keepdims=True) - N_EXPERTS
    g_lo = g_sel * EXPERTS_PER_GROUP
    in_grp = (lane >= g_lo) & (lane < g_lo + EXPERTS_PER_GROUP)
    l2 = jnp.where(in_grp, logits, NEG_BIG)
    m1 = _lane_max(l2)
    i1 = jnp.min(jnp.where(l2 == m1, lane, far), axis=-1, keepdims=True)
    l2b = jnp.where(lane == i1, NEG_BIG, l2)
    m2 = _lane_max(l2b)
    i2 = jnp.min(jnp.where(l2b == m2, lane, far), axis=-1, keepdims=True)
    e2 = jnp.exp(m2 - m1)
    w1 = 1.0 / (1.0 + e2)
    w2 = e2 * w1
    gates = p_sel * (jnp.where(lane == i1, w1, 0.0) + jnp.where(lane == i2, w2, 0.0))
    return gates, jnp.where(lane == g_sel, 1.0, 0.0)


def _moe_kernel(h_ref, g_ref, wr_ref, br_ref, tri_ref, w13_ref, w2_ref, out_ref,
                ts_ref, gs_ref, acc_ref, pt_ref, meta_ref, *, tm, win, eps):
    step = pl.program_id(1)

    @pl.when(step == 0)
    def _():
        u = _rms(h_ref[...], g_ref[...]).astype(BF16)
        gates, onehot = _router(u, wr_ref, br_ref)
        rank = jnp.dot(tri_ref[...], onehot.astype(BF16), preferred_element_type=F32)
        cnt = jnp.sum(onehot, axis=0, keepdims=True)
        off = pltpu.roll(cnt, 1, 1) + pltpu.roll(cnt, 2, 1) + pltpu.roll(cnt, 3, 1)
        pos = jnp.sum(onehot * (off + rank), axis=-1, keepdims=True)
        slot_lane = lax.broadcasted_iota(jnp.int32, (tm, tm), 1).astype(F32)
        pt_ref[...] = jnp.where(slot_lane == pos, 1.0, 0.0).astype(BF16)
        pos_row = jnp.broadcast_to(pos, (tm, LANES)).T[0:1, :]
        slot_sub = lax.broadcasted_iota(jnp.int32, (tm, tm), 0).astype(F32)
        perm = jnp.where(slot_sub == pos_row, 1.0, 0.0).astype(BF16)
        ts_ref[0:tm, :] = jnp.dot(perm, u, preferred_element_type=F32).astype(BF16)
        g_hi = gates.astype(BF16)
        g_lo = (gates - g_hi.astype(F32)).astype(BF16)
        gs_ref[0:tm, :] = (jnp.dot(perm, g_hi, preferred_element_type=F32)
                           + jnp.dot(perm, g_lo, preferred_element_type=F32))
        ts_ref[tm:tm + win, :] = jnp.zeros((win, D_MODEL), BF16)
        gs_ref[tm:tm + win, :] = jnp.zeros((win, LANES), F32)
        acc_ref[...] = jnp.zeros(acc_ref.shape, F32)
        off_i = off.astype(jnp.int32)
        end_i = (off + cnt).astype(jnp.int32)
        for g in range(N_GROUPS):
            meta_ref[0, g] = off_i[0, g]
            meta_ref[1, g] = end_i[0, g]

    e0 = step * eps
    grp = lax.shift_right_logical(e0, 3)
    start = lax.shift_left(lax.shift_right_logical(meta_ref[0, grp], 4), 4)
    span = meta_ref[1, grp] - start
    nwin = 1 + sum((span > k * win).astype(jnp.int32) for k in range(1, tm // win + 2))

    def window(i, c):
        r0 = pl.multiple_of(start + i * win, 16)
        x = ts_ref[pl.ds(r0, win), :]
        gsl = gs_ref[pl.ds(r0, win), :]
        lane = lax.broadcasted_iota(jnp.int32, gsl.shape, 1)
        y = None
        for k in range(eps):
            ab = jnp.dot(x, w13_ref[k], preferred_element_type=F32)
            a = ab[:, :EXPERT_FF]
            hdn = a * _sigmoid(a) * ab[:, EXPERT_FF:]
            ge = jnp.sum(jnp.where(lane == e0 + k, gsl, 0.0), axis=-1, keepdims=True)
            yk = jnp.dot((hdn * ge).astype(BF16), w2_ref[k], preferred_element_type=F32)
            y = yk if y is None else y + yk
        acc_ref[pl.ds(r0, win), :] += y
        return c
    lax.fori_loop(0, nwin, window, 0)

    @pl.when(step == pl.num_programs(1) - 1)
    def _():
        out_ref[...] = h_ref[...] + jnp.dot(pt_ref[...], acc_ref[0:tm, :].astype(BF16), preferred_element_type=F32)


def _moe(h, g_ffn, w_router, b_router, w13, w2, win=MOE_WINDOW, eps=MOE_EXPERTS_PER_STEP):
    n = h.shape[0]
    tm = _pick_tile(n, 1024)
    tri = jnp.tri(tm, tm, -1, dtype=BF16)
    row = lambda i, e: (i, 0)
    fixed = lambda i, e: (0, 0)
    kern = functools.partial(_moe_kernel, tm=tm, win=win, eps=eps)
    return pl.pallas_call(
        kern,
        grid=(n // tm, N_EXPERTS // eps),
        in_specs=[
            pl.BlockSpec((tm, D_MODEL), row),
            pl.BlockSpec((1, D_MODEL), fixed),
            pl.BlockSpec((D_MODEL, LANES), fixed),
            pl.BlockSpec((1, LANES), fixed),
            pl.BlockSpec((tm, tm), fixed),
            pl.BlockSpec((eps, D_MODEL, 2 * EXPERT_FF), lambda i, e: (e, 0, 0)),
            pl.BlockSpec((eps, EXPERT_FF, D_MODEL), lambda i, e: (e, 0, 0)),
        ],
        out_specs=pl.BlockSpec((tm, D_MODEL), row),
        out_shape=jax.ShapeDtypeStruct((n, D_MODEL), F32),
        scratch_shapes=[
            pltpu.VMEM((tm + win, D_MODEL), BF16),
            pltpu.VMEM((tm + win, LANES), F32),
            pltpu.VMEM((tm + win, D_MODEL), F32),
            pltpu.VMEM((tm, tm), BF16),
            pltpu.SMEM((2, N_GROUPS), jnp.int32),
        ],
        compiler_params=_params(("parallel", "arbitrary")),
        name="hier_moe",
    )(h, g_ffn.reshape(1, D_MODEL), w_router, b_router, tri, w13, w2)


def _ple_kernel(h_ref, p_ref, gp_ref, wg_ref, wp_ref, gf_ref, out_ref):
    h = h_ref[...]
    gate = _sigmoid(jnp.dot(_rms(h, gp_ref[...]).astype(BF16), wg_ref[...], preferred_element_type=F32))
    pe = jnp.dot(p_ref[...].astype(BF16), wp_ref[...], preferred_element_type=F32)
    out_ref[...] = _rms(h + gate * pe, gf_ref[...])


def _ple(h, p, g_ple, w_ple_gate, w_ple, g_final):
    n = h.shape[0]
    tm = _pick_tile(n, 512)
    row = lambda i: (i, 0)
    fixed = lambda i: (0, 0)
    return pl.pallas_call(
        _ple_kernel,
        grid=(n // tm,),
        in_specs=[
            pl.BlockSpec((tm, D_MODEL), row),
            pl.BlockSpec((tm, PLE_DIM), row),
            pl.BlockSpec((1, D_MODEL), fixed),
            pl.BlockSpec((D_MODEL, D_MODEL), fixed),
            pl.BlockSpec((PLE_DIM, D_MODEL), fixed),
            pl.BlockSpec((1, D_MODEL), fixed),
        ],
        out_specs=pl.BlockSpec((tm, D_MODEL), row),
        out_shape=jax.ShapeDtypeStruct((n, D_MODEL), F32),
        compiler_params=_params(("parallel",)),
        name="ple_final",
    )(h, p, g_ple.reshape(1, D_MODEL), w_ple_gate, w_ple, g_final.reshape(1, D_MODEL))


def _prep_weights(g_mix, w_in, lam_q1, lam_k1, lam_q2, lam_k2, subln_g,
                  ssm_lam_re, ssm_lam_im, ssm_log_dt, ssm_b_re, ssm_b_im, ssm_c_re, ssm_c_im, ssm_d,
                  w_glu, b_glu, w_attn_br, w_ssm_br, w_out, g_ffn, w_r1, b_r1, w_r2, b_r2,
                  w_e1, w_e3, w_e2, g_ple, w_ple_gate, w_ple, g_final):
    i = 0
    lam = (jnp.exp(jnp.sum(lam_q1[i] * lam_k1[i])) - jnp.exp(jnp.sum(lam_q2[i] * lam_k2[i])) + LAM_INIT)
    pad = LANES - N_EXPERTS - N_GROUPS
    w_router = jnp.concatenate([w_r2[i], w_r1[i], jnp.zeros((D_MODEL, pad), F32)], axis=1).astype(BF16)
    b_router = jnp.concatenate([b_r2[i], b_r1[i], jnp.zeros((pad,), F32)]).reshape(1, LANES)
    return dict(
        g_mix=g_mix[i], w_in=w_in[i].astype(BF16), lam=lam.astype(F32), subln_g=subln_g[i],
        ssm=_ssm_weights(ssm_lam_re[i], ssm_lam_im[i], ssm_log_dt[i], ssm_b_re[i], ssm_b_im[i],
                         ssm_c_re[i], ssm_c_im[i], ssm_d[i]),
        w_glu=w_glu[i].astype(BF16), b_glu=b_glu[i], w_attn_br=w_attn_br[i].astype(BF16),
        w_ssm_br=w_ssm_br[i].astype(BF16), w_out=w_out[i].astype(BF16), g_ffn=g_ffn[i],
        w_router=w_router, b_router=b_router,
        w13=jnp.concatenate([w_e1[i], w_e3[i]], axis=-1).astype(BF16), w2=w_e2[i].astype(BF16),
        g_ple=g_ple[i], w_ple_gate=w_ple_gate[i].astype(BF16), w_ple=w_ple[i].astype(BF16), g_final=g_final,
    )


def _trunk(x, p, w):
    bsz, seq, _ = x.shape
    n = bsz * seq
    x2 = x.reshape(n, D_MODEL)
    q, k, v, s_in, gl = _in_proj(x2, seq, w["g_mix"], w["w_in"], w["rope"])
    shp = (bsz, seq, ATT_WIDTH)
    o = _attention(q.reshape(shp), k.reshape(shp), v.reshape(shp), w["lam"], w["subln_g"])
    ys = _ssm(s_in, bsz, seq, w["ssm"])
    h = _mix(x2, o.reshape(n, ATT_WIDTH), ys, gl, w["w_glu"], w["b_glu"], w["w_attn_br"], w["w_ssm_br"], w["w_out"])
    h = _moe(h, w["g_ffn"], w["w_router"], w["b_router"], w["w13"], w["w2"])
    y = _ple(h, p[0].reshape(n, PLE_DIM), w["g_ple"], w["w_ple_gate"], w["w_ple"], w["g_final"])
    return y.reshape(bsz, seq, D_MODEL)


def kernel(x_prompt, x_sample, p_prompt, p_sample, g_mix, w_in, lam_q1, lam_k1, lam_q2, lam_k2, subln_g, ssm_lam_re, ssm_lam_im, ssm_log_dt, ssm_b_re, ssm_b_im, ssm_c_re, ssm_c_im, ssm_d, w_glu, b_glu, w_attn_br, w_ssm_br, w_out, g_ffn, w_r1, b_r1, w_r2, b_r2, w_e1, w_e3, w_e2, g_ple, w_ple_gate, w_ple, g_final):
    w = _prep_weights(g_mix, w_in, lam_q1, lam_k1, lam_q2, lam_k2, subln_g,
                      ssm_lam_re, ssm_lam_im, ssm_log_dt, ssm_b_re, ssm_b_im, ssm_c_re, ssm_c_im, ssm_d,
                      w_glu, b_glu, w_attn_br, w_ssm_br, w_out, g_ffn, w_r1, b_r1, w_r2, b_r2,
                      w_e1, w_e3, w_e2, g_ple, w_ple_gate, w_ple, g_final)
    w["rope"] = _rope_tables(max(x_prompt.shape[1], x_sample.shape[1]))
    return (_trunk(x_prompt, p_prompt, w), _trunk(x_sample, p_sample, w))
```

```python
import functools
import math

import jax
import jax.numpy as jnp
from jax import lax
from jax.experimental import pallas as pl
from jax.experimental.pallas import tpu as pltpu

F32 = jnp.float32
BF16 = jnp.bfloat16
HIGHEST = lax.Precision.HIGHEST

D_MODEL = 1024
N_HEADS = 8
HEAD_DIM = 64
ATT_V_DIM = 2 * HEAD_DIM
ATT_WIDTH = N_HEADS * ATT_V_DIM
ROT_DIM = HEAD_DIM // 4
ROPE_THETA = 500000.0
SSM_CH = 16
SSM_GROUPS = 32
SSM_WIDTH = SSM_GROUPS * SSM_CH
SSM_STATE = 64
N_GROUPS = 4
EXPERTS_PER_GROUP = 8
N_EXPERTS = N_GROUPS * EXPERTS_PER_GROUP
EXPERT_FF = 256
PLE_DIM = 256
EPS = 1e-6
LAM_INIT = 0.8 - 0.6 * math.exp(-0.3 * 0)

LANES = 128
SUBLANES = 8
SSM_CHUNK = 16
SSM_PAIR_LANES = 2 * SSM_STATE
SSM_PAIR_CH = 2 * SSM_CH
SSM_PIECES = 128 // SSM_PAIR_CH
VMEM_LIMIT = 56 * 1024 * 1024
NEG_BIG = -1e30
LOG2E = math.log2(math.e)
ATTN_ROW_BLOCK = 16
ATTN_OVERFLOW_MARGIN = 100.0
ATTN_SHORT_SEQ = 2048
ATTN_TILES_SHORT = (2048, 512)
ATTN_TILES_LONG = (1024, 1024)
MOE_WINDOW = 320
MOE_EXPERTS_PER_STEP = 4


def _params(sem):
    return pltpu.CompilerParams(dimension_semantics=sem, vmem_limit_bytes=VMEM_LIMIT)


def _rms(x, g):
    return x * lax.rsqrt(jnp.mean(x * x, axis=-1, keepdims=True) + EPS) * g


def _sigmoid(x):
    return 1.0 / (1.0 + jnp.exp(-x))


def _pick_tile(n, pref):
    t = min(n, pref)
    while n % t:
        t //= 2
    return t


def _lane_piece(rows):
    return lax.shift_right_logical(lax.broadcasted_iota(jnp.int32, (rows, LANES), 1), SSM_PAIR_CH.bit_length() - 1)


def _to_chunk_major(tok_ref, out_ref, rows):
    T = SSM_CHUNK
    piece_id = _lane_piece(rows)
    for k in range(T // SSM_PIECES):
        xs = [[tok_ref[col, pl.ds(SSM_PIECES * k + m, rows, stride=T), :] for col in range(SSM_WIDTH // LANES)]
              for m in range(SSM_PIECES)]
        for gp in range(SSM_GROUPS // 2):
            col, src = divmod(gp, SSM_PIECES)
            dest = None
            for m in range(SSM_PIECES):
                piece = xs[m][col]
                shift = ((m - src) * SSM_PAIR_CH) % LANES
                if shift:
                    piece = pltpu.roll(piece, shift, 1)
                dest = piece if dest is None else jnp.where(piece_id == m, piece, dest)
            out_ref[gp, :, k * LANES:(k + 1) * LANES] = dest.astype(out_ref.dtype)


def _to_token_major(chunk_ref, tok_ref, rows):
    T = SSM_CHUNK
    piece_id = _lane_piece(rows)
    for t in range(T):
        col, src = divmod(t, SSM_PIECES)
        for k in range(SSM_WIDTH // LANES):
            dest = None
            for m in range(SSM_PIECES):
                piece = chunk_ref[SSM_PIECES * k + m, :, col * LANES:(col + 1) * LANES].astype(F32)
                shift = ((m - src) * SSM_PAIR_CH) % LANES
                if shift:
                    piece = pltpu.roll(piece, shift, 1)
                dest = piece if dest is None else jnp.where(piece_id == m, piece, dest)
            tok_ref[k, pl.ds(t, rows, stride=T), :] = dest


def _in_proj_kernel(x_ref, g_ref, w_ref, rope_ref, q_ref, k_ref, v_ref, s_ref, gl_ref, tok_ref):
    u = _rms(x_ref[...], g_ref[...]).astype(BF16)
    cos = rope_ref[:, 0:LANES]
    sa = rope_ref[:, LANES:2 * LANES]
    sb = rope_ref[:, 2 * LANES:3 * LANES]
    for sec, out_ref, scale in ((0, q_ref, HEAD_DIM ** -0.5 * LOG2E), (1, k_ref, 1.0)):
        y = jnp.dot(u, w_ref[:, sec * ATT_WIDTH:(sec + 1) * ATT_WIDTH], preferred_element_type=F32)
        for h in range(N_HEADS):
            yh = y[:, h * LANES:(h + 1) * LANES]
            r = yh * cos + pltpu.roll(yh, LANES - ROT_DIM // 2, 1) * sa + pltpu.roll(yh, ROT_DIM // 2, 1) * sb
            out_ref[:, h * LANES:(h + 1) * LANES] = (r * scale).astype(BF16)
    c0 = 2 * ATT_WIDTH
    v_ref[...] = jnp.dot(u, w_ref[:, c0:c0 + ATT_WIDTH], preferred_element_type=F32).astype(BF16)
    c0 += ATT_WIDTH
    s_tok = jnp.dot(u, w_ref[:, c0:c0 + SSM_WIDTH], preferred_element_type=F32)
    for col in range(SSM_WIDTH // LANES):
        tok_ref[col] = s_tok[:, col * LANES:(col + 1) * LANES]
    _to_chunk_major(tok_ref, s_ref, tok_ref.shape[1] // SSM_CHUNK)
    c0 += SSM_WIDTH
    gl_ref[...] = jnp.dot(u, w_ref[:, c0:c0 + 2 * D_MODEL], preferred_element_type=F32).astype(BF16)


def _rope_tables(seq):
    half = ROT_DIM // 2
    inv = ROPE_THETA ** (-jnp.arange(0, ROT_DIM, 2, dtype=F32) / ROT_DIM)
    ang = jnp.arange(seq, dtype=F32)[:, None] * inv[None, :]
    cos, sin = jnp.cos(ang), jnp.sin(ang)
    one = jnp.ones((seq, HEAD_DIM - ROT_DIM), F32)
    zero = jnp.zeros((seq, HEAD_DIM - ROT_DIM), F32)
    zh = jnp.zeros((seq, half), F32)
    cos_t = [cos, cos, one]
    sa_t = [-sin, zh, zero]
    sb_t = [zh, sin, zero]
    return jnp.concatenate(cos_t * 2 + sa_t * 2 + sb_t * 2, axis=1)


def _in_proj(x2, seq, g_mix, w_in, rope):
    n = x2.shape[0]
    tm = _pick_tile(seq, 512)
    nl = seq // tm
    in_cols = w_in.shape[1]
    row = lambda i: (i, 0)
    fixed = lambda i: (0, 0)
    tab = lambda i: (i % nl, 0)
    return pl.pallas_call(
        _in_proj_kernel,
        grid=(n // tm,),
        in_specs=[
            pl.BlockSpec((tm, D_MODEL), row),
            pl.BlockSpec((1, D_MODEL), fixed),
            pl.BlockSpec((D_MODEL, in_cols), fixed),
            pl.BlockSpec((tm, 3 * LANES), tab),
        ],
        out_specs=[
            pl.BlockSpec((tm, ATT_WIDTH), row),
            pl.BlockSpec((tm, ATT_WIDTH), row),
            pl.BlockSpec((tm, ATT_WIDTH), row),
            pl.BlockSpec((SSM_GROUPS // 2, tm // SSM_CHUNK, SSM_CHUNK * SSM_PAIR_CH), lambda i: (0, i, 0)),
            pl.BlockSpec((tm, 2 * D_MODEL), row),
        ],
        out_shape=[
            jax.ShapeDtypeStruct((n, ATT_WIDTH), BF16),
            jax.ShapeDtypeStruct((n, ATT_WIDTH), BF16),
            jax.ShapeDtypeStruct((n, ATT_WIDTH), BF16),
            jax.ShapeDtypeStruct((SSM_GROUPS // 2, n // SSM_CHUNK, SSM_CHUNK * SSM_PAIR_CH), BF16),
            jax.ShapeDtypeStruct((n, 2 * D_MODEL), BF16),
        ],
        scratch_shapes=[pltpu.VMEM((SSM_WIDTH // LANES, tm, LANES), F32)],
        compiler_params=_params(("parallel",)),
        name="in_proj",
    )(x2, g_mix.reshape(1, D_MODEL), w_in, rope)


def _col_reduce(x, pair_op, reduce_op):
    r = x.shape[0]
    rb = min(ATTN_ROW_BLOCK, r)
    out = None
    for i in range(r // rb):
        blk = x[i * rb:(i + 1) * rb]
        n = rb
        while n > SUBLANES:
            n //= 2
            blk = pair_op(blk[:n], blk[n:])
        out = blk if out is None else pair_op(out, blk)
    return reduce_op(out, axis=0, keepdims=True)


def _attn_kernel(lam_ref, q_ref, k_ref, v_ref, g_ref, o_ref, vt_ref, r_ref, l_ref, acc_ref, p_ref, be_ref, fl_ref,
                 *, tq, tk, nk):
    qi = pl.program_id(2)

    @pl.when(qi == 0)
    def _():
        def tr(j, c):
            off = pl.multiple_of(j * tk, tk)
            vt_ref[:, pl.ds(off, tk)] = v_ref[0, pl.ds(off, tk), :].astype(F32).T.astype(BF16)
            return c
        lax.fori_loop(0, nk, tr, 0)

    qt = q_ref[0].astype(F32).T
    row = lax.broadcasted_iota(jnp.int32, qt.shape, 0)
    qm = jnp.concatenate([jnp.where(row < HEAD_DIM, qt, 0.0),
                          jnp.where(row >= HEAD_DIM, qt, 0.0)], axis=1).astype(BF16)

    def scores(j):
        off = pl.multiple_of(j * tk, tk)
        return jnp.dot(k_ref[0, pl.ds(off, tk), :], qm, preferred_element_type=F32)

    def values(j, slot):
        off = pl.multiple_of(j * tk, tk)
        pv = jnp.dot(vt_ref[:, pl.ds(off, tk)], p_ref[slot], preferred_element_type=F32)
        acc_ref[...] = be_ref[slot] * (acc_ref[...] + pv)

    rb = min(ATTN_ROW_BLOCK, tk)
    head = jnp.dot(k_ref[0, 0:rb, :], qm, preferred_element_type=F32)
    r_ref[...] = _col_reduce(head, jnp.maximum, jnp.max)
    l_ref[...] = jnp.zeros(l_ref.shape, F32)
    fl_ref[...] = jnp.zeros(fl_ref.shape, F32)
    acc_ref[...] = jnp.zeros(acc_ref.shape, F32)

    def step(j, slot, with_values=True):
        s = scores(j)
        ref = r_ref[...]
        p = jnp.exp2(s - ref)
        cm = _col_reduce(s, jnp.maximum, jnp.max)
        new = jnp.maximum(ref, cm)
        beta = jnp.exp2(ref - new)
        fl_ref[...] = jnp.maximum(fl_ref[...], cm - ref)
        l_ref[...] = beta * (l_ref[...] + _col_reduce(p, jnp.add, jnp.sum))
        r_ref[...] = new
        be_ref[slot] = beta
        p_ref[slot] = p.astype(BF16)
        if with_values:
            values(j - 1, 1 - slot)

    step(0, 0, with_values=False)
    if nk > 1:
        def body(i, c):
            step(2 * i + 1, 1)
            step(2 * i + 2, 0)
            return c
        lax.fori_loop(0, (nk - 2) // 2, body, 0)
        step(nk - 1, 1)
    values(nk - 1, (nk - 1) % 2)

    @pl.when(jnp.max(fl_ref[...]) > ATTN_OVERFLOW_MARGIN)
    def _():
        r_ref[...] = jnp.full(r_ref.shape, NEG_BIG, F32)
        l_ref[...] = jnp.zeros(l_ref.shape, F32)
        acc_ref[...] = jnp.zeros(acc_ref.shape, F32)

        def safe(j, c):
            s = scores(j)
            m_old = r_ref[...]
            m_new = jnp.maximum(m_old, _col_reduce(s, jnp.maximum, jnp.max))
            alpha = jnp.exp2(m_old - m_new)
            p = jnp.exp2(s - m_new)
            l_ref[...] = alpha * l_ref[...] + _col_reduce(p, jnp.add, jnp.sum)
            r_ref[...] = m_new
            off = pl.multiple_of(j * tk, tk)
            pv = jnp.dot(vt_ref[:, pl.ds(off, tk)], p.astype(BF16), preferred_element_type=F32)
            acc_ref[...] = alpha * acc_ref[...] + pv
            return c
        lax.fori_loop(0, nk, safe, 0)

    lam = lam_ref[0]
    o_all = acc_ref[...] * (1.0 / l_ref[...])
    o = o_all[:, :tq] - lam * o_all[:, tq:]
    o = o * lax.rsqrt(jnp.mean(o * o, axis=0, keepdims=True) + EPS) * g_ref[...]
    o_ref[0] = (o * (1.0 - LAM_INIT)).T.astype(BF16)


def _attention(q, k, v, lam, subln_g):
    bsz, seq, _ = q.shape
    tq_pref, tk_pref = ATTN_TILES_SHORT if seq <= ATTN_SHORT_SEQ else ATTN_TILES_LONG
    tq = _pick_tile(seq, tq_pref)
    tk = _pick_tile(seq, tk_pref)
    nk = seq // tk
    assert nk == 1 or nk % 2 == 0
    kern = functools.partial(_attn_kernel, tq=tq, tk=tk, nk=nk)
    return pl.pallas_call(
        kern,
        grid=(bsz, N_HEADS, seq // tq),
        in_specs=[
            pl.BlockSpec(memory_space=pltpu.SMEM),
            pl.BlockSpec((1, tq, LANES), lambda b, h, i: (b, i, h)),
            pl.BlockSpec((1, seq, LANES), lambda b, h, i: (b, 0, h)),
            pl.BlockSpec((1, seq, LANES), lambda b, h, i: (b, 0, h)),
            pl.BlockSpec((ATT_V_DIM, 1), lambda b, h, i: (0, 0)),
        ],
        out_specs=pl.BlockSpec((1, tq, LANES), lambda b, h, i: (b, i, h)),
        out_shape=jax.ShapeDtypeStruct((bsz, seq, ATT_WIDTH), BF16),
        scratch_shapes=[
            pltpu.VMEM((ATT_V_DIM, seq), BF16),
            pltpu.VMEM((1, 2 * tq), F32),
            pltpu.VMEM((1, 2 * tq), F32),
            pltpu.VMEM((ATT_V_DIM, 2 * tq), F32),
            pltpu.VMEM((2, tk, 2 * tq), BF16),
            pltpu.VMEM((2, 1, 2 * tq), F32),
            pltpu.VMEM((1, 2 * tq), F32),
        ],
        compiler_params=_params(("parallel", "parallel", "arbitrary")),
        name="diff_attn",
    )(lam.reshape(1), q, k, v, subln_g.reshape(ATT_V_DIM, 1))


def _ssm_weights(lam_re, lam_im, log_dt, b_re, b_im, c_re, c_im, d_skip):
    T, G, P, CH = SSM_CHUNK, SSM_GROUPS, SSM_STATE, SSM_CH
    n = jnp.arange(T + 1, dtype=F32)[:, None, None]
    pw_re, pw_im, bb_re, bb_im = [], [], [], []
    for dirn in range(2):
        dt = jnp.exp(log_dt[dirn])[:, None]
        lr, li = lam_re[dirn], lam_im[dirn]
        mag = jnp.exp(lr * dt)
        a_re, a_im = mag * jnp.cos(li * dt), mag * jnp.sin(li * dt)
        den = lr * lr + li * li
        n_re = a_re - 1.0
        q_re = (n_re * lr + a_im * li) / den
        q_im = (a_im * lr - n_re * li) / den
        bb_re.append(q_re[..., None] * b_re[dirn] - q_im[..., None] * b_im[dirn])
        bb_im.append(q_re[..., None] * b_im[dirn] + q_im[..., None] * b_re[dirn])
        magn = jnp.exp(n * (lr * dt)[None])
        pw_re.append(magn * jnp.cos(n * (li * dt)[None]))
        pw_im.append(magn * jnp.sin(n * (li * dt)[None]))

    def cmul(ar, ai, br, bi):
        return ar * br - ai * bi, ar * bi + ai * br

    kern = []
    for dirn in range(2):
        car, cai = cmul(c_re[dirn][None], c_im[dirn][None],
                        pw_re[dirn][:T, :, None, :], pw_im[dirn][:T, :, None, :])
        kern.append(jnp.einsum('dgcp,gpk->dgck', car, bb_re[dirn], precision=HIGHEST)
                    - jnp.einsum('dgcp,gpk->dgck', cai, bb_im[dirn], precision=HIGHEST))
    tj = jnp.arange(T)
    lag = tj[None, :] - tj[:, None]
    k0 = kern[0][0] + kern[1][0] + jnp.eye(CH, dtype=F32)[None] * d_skip[:, :, None]
    k_lag = jnp.concatenate([kern[1][:0:-1], k0[None], kern[0][1:]], axis=0).astype(BF16)
    eye2 = jnp.eye(2, dtype=BF16)
    k_strip = k_lag.reshape(2 * T - 1, G // 2, 2, CH, CH).transpose(1, 2, 4, 0, 3)
    k_strip = k_strip[:, :, :, :, None, :] * eye2[None, :, None, None, :, None]
    k_strip = k_strip.reshape(G // 2, 2 * CH, (2 * T - 1) * 2 * CH).astype(F32)
    k_strip = jnp.pad(k_strip, ((0, 0), (0, 0), (0, 2 * CH)))

    def w_dir(dirn, steps):
        pr = pw_re[dirn][steps].transpose(1, 0, 2)[:, :, None, :]
        pi = pw_im[dirn][steps].transpose(1, 0, 2)[:, :, None, :]
        return cmul(pr, pi, bb_re[dirn].transpose(0, 2, 1)[:, None], bb_im[dirn].transpose(0, 2, 1)[:, None])
    wf_re, wf_im = w_dir(0, T - 1 - tj)
    wb_re, wb_im = w_dir(1, tj)
    w_all = jnp.stack([wf_re, wf_im, wb_re, wb_im], axis=3).astype(BF16)
    w_all = w_all.reshape(G // 2, 2, T, CH, 4, P).transpose(0, 2, 1, 3, 4, 5)
    w_pair = (w_all[:, :, :, :, :, None, :] * eye2[None, None, :, None, None, :, None]).reshape(G // 2, T * 2 * CH, 4 * 2 * P)

    def z_dir(dirn, steps):
        pr = pw_re[dirn][steps].transpose(1, 2, 0)[:, :, :, None]
        pi = pw_im[dirn][steps].transpose(1, 2, 0)[:, :, :, None]
        return cmul(c_re[dirn].transpose(0, 2, 1)[:, :, None, :], c_im[dirn].transpose(0, 2, 1)[:, :, None, :], pr, pi)
    zf_re, zf_im = z_dir(0, 1 + tj)
    zb_re, zb_im = z_dir(1, T - tj)
    z_all = jnp.stack([zf_re, -zf_im, zb_re, -zb_im], axis=1).astype(BF16)
    z_all = z_all.reshape(G // 2, 2, 4, P, T, CH)
    z_pair = (z_all.transpose(0, 2, 1, 3, 4, 5)[:, :, :, :, :, None, :] * eye2[None, None, :, None, None, :, None])
    z_pair = z_pair.reshape(G // 2, 4 * 2 * P, T * 2 * CH)

    a_tab = []
    for dirn in range(2):
        dt = jnp.exp(log_dt[dirn])[:, None]
        order = jnp.arange(SUBLANES) if dirn == 0 else SUBLANES - 1 - jnp.arange(SUBLANES)
        nn = jnp.concatenate([order, jnp.array([1, 2, 4, 8])]).astype(F32)[:, None, None] * T
        magn = jnp.exp(nn * (lam_re[dirn] * dt)[None])
        ang = nn * (lam_im[dirn] * dt)[None]
        tab = jnp.stack([magn * jnp.cos(ang), magn * jnp.sin(ang)], axis=0)
        tab = jnp.pad(tab, ((0, 0), (0, 2 * SUBLANES - tab.shape[1]), (0, 0), (0, 0)))
        a_tab.append(tab.reshape(2, 2 * SUBLANES, G // 2, 2 * P).transpose(2, 0, 1, 3))
    a_tab = jnp.stack(a_tab, axis=1)
    return k_strip, w_pair, z_pair, a_tab


def _tile_scan(x_re, x_im, a_tab, reverse):
    row = lax.broadcasted_iota(jnp.int32, x_re.shape, 0)
    for lvl, d in enumerate((1, 2, 4)):
        ar = a_tab[0][SUBLANES + lvl:SUBLANES + lvl + 1, :]
        ai = a_tab[1][SUBLANES + lvl:SUBLANES + lvl + 1, :]
        if reverse:
            keep = row < SUBLANES - d
            shift = SUBLANES - d
        else:
            keep = row >= d
            shift = d
        sr = jnp.where(keep, pltpu.roll(x_re, shift, 0), 0.0)
        si = jnp.where(keep, pltpu.roll(x_im, shift, 0), 0.0)
        x_re, x_im = x_re + ar * sr - ai * si, x_im + ar * si + ai * sr
    return x_re, x_im


def _ssm_kernel(u_ref, w_ref, k_ref, z_ref, a_ref, y_ref, v_ref, s_ref, *, nc):
    pl_ = SSM_PAIR_LANES
    nt = nc // SUBLANES
    u = u_ref[0]
    v_ref[...] = jnp.dot(u, w_ref[0], preferred_element_type=F32)

    row = lax.broadcasted_iota(jnp.int32, (SUBLANES, pl_), 0)

    def tile_step(i, carry):
        new = []
        for dirn in range(2):
            c_re, c_im = carry[2 * dirn], carry[2 * dirn + 1]
            a_tab = (a_ref[0, dirn, 0], a_ref[0, dirn, 1])
            t = i if dirn == 0 else nt - 1 - i
            r0 = pl.multiple_of(t * SUBLANES, SUBLANES)
            lo = 2 * dirn * pl_
            x_re, x_im = _tile_scan(v_ref[pl.ds(r0, SUBLANES), lo:lo + pl_],
                                    v_ref[pl.ds(r0, SUBLANES), lo + pl_:lo + 2 * pl_], a_tab, dirn == 1)
            if dirn == 0:
                keep, shift, last = row >= 1, 1, SUBLANES - 1
            else:
                keep, shift, last = row < SUBLANES - 1, SUBLANES - 1, 0
            pr, pi = a_tab[0][0:SUBLANES, :], a_tab[1][0:SUBLANES, :]
            s_ref[pl.ds(r0, SUBLANES), lo:lo + pl_] = (
                pr * c_re - pi * c_im + jnp.where(keep, pltpu.roll(x_re, shift, 0), 0.0))
            s_ref[pl.ds(r0, SUBLANES), lo + pl_:lo + 2 * pl_] = (
                pr * c_im + pi * c_re + jnp.where(keep, pltpu.roll(x_im, shift, 0), 0.0))
            a8r = a_tab[0][SUBLANES + 3:SUBLANES + 4, :]
            a8i = a_tab[1][SUBLANES + 3:SUBLANES + 4, :]
            new.append(a8r * c_re - a8i * c_im + x_re[last:last + 1, :])
            new.append(a8r * c_im + a8i * c_re + x_im[last:last + 1, :])
        return tuple(new)

    zero = jnp.zeros((1, pl_), F32)
    lax.fori_loop(0, nt, tile_step, (zero, zero, zero, zero))

    strip = k_ref[0]
    tw = SSM_CHUNK * SSM_PAIR_CH
    m_pair = jnp.concatenate([strip[:, (SSM_CHUNK - 1 - j) * SSM_PAIR_CH:(SSM_CHUNK - 1 - j) * SSM_PAIR_CH + tw]
                              for j in range(SSM_CHUNK)], axis=0).astype(BF16)
    ys = (jnp.dot(s_ref[...].astype(BF16), z_ref[0], preferred_element_type=F32)
          + jnp.dot(u, m_pair, preferred_element_type=F32))
    y_ref[0] = jax.nn.gelu(ys).astype(BF16)


def _ssm(ug, bsz, seq, weights):
    k_strip, w_pair, z_pair, a_tab = weights
    nc = seq // SSM_CHUNK
    tw = SSM_CHUNK * SSM_PAIR_CH
    kern = functools.partial(_ssm_kernel, nc=nc)
    return pl.pallas_call(
        kern,
        grid=(SSM_GROUPS // 2, bsz),
        in_specs=[
            pl.BlockSpec((1, nc, tw), lambda g, b: (g, b, 0)),
            pl.BlockSpec((1, tw, 4 * SSM_PAIR_LANES), lambda g, b: (g, 0, 0)),
            pl.BlockSpec((1, SSM_PAIR_CH, 2 * tw), lambda g, b: (g, 0, 0)),
            pl.BlockSpec((1, 4 * SSM_PAIR_LANES, tw), lambda g, b: (g, 0, 0)),
            pl.BlockSpec((1, 2, 2, 2 * SUBLANES, SSM_PAIR_LANES), lambda g, b: (g, 0, 0, 0, 0)),
        ],
        out_specs=pl.BlockSpec((1, nc, tw), lambda g, b: (g, b, 0)),
        out_shape=jax.ShapeDtypeStruct(ug.shape, BF16),
        scratch_shapes=[
            pltpu.VMEM((nc, 4 * SSM_PAIR_LANES), F32),
            pltpu.VMEM((nc, 4 * SSM_PAIR_LANES), F32),
        ],
        compiler_params=_params(("parallel", "arbitrary")),
        name="s5_scan",
    )(ug, w_pair, k_strip, z_pair, a_tab)


def _mix_kernel(h_ref, o_ref, yg_ref, gl_ref, wglu_ref, bglu_ref, wa_ref, ws_ref, wo_ref, out_ref, tok_ref):
    _to_token_major(yg_ref, tok_ref, tok_ref.shape[1] // SSM_CHUNK)
    yg = jnp.concatenate([tok_ref[col] for col in range(SSM_WIDTH // LANES)], axis=1)
    z = jnp.dot(yg.astype(BF16), wglu_ref[...], preferred_element_type=F32) + bglu_ref[...]
    ysg = (yg * _sigmoid(z)).astype(BF16)
    br_a = jnp.dot(o_ref[...], wa_ref[...], preferred_element_type=F32)
    br_s = jnp.dot(ysg, ws_ref[...], preferred_element_type=F32)
    gl = gl_ref[...].astype(F32)
    mix = _sigmoid(gl[:, :D_MODEL]) * br_a + _sigmoid(gl[:, D_MODEL:]) * br_s
    out_ref[...] = h_ref[...] + jnp.dot(mix.astype(BF16), wo_ref[...], preferred_element_type=F32)


def _mix(h, o, ys, gl, w_glu, b_glu, w_attn_br, w_ssm_br, w_out):
    n = h.shape[0]
    tm = _pick_tile(n, 512)
    row = lambda i: (i, 0)
    fixed = lambda i: (0, 0)
    return pl.pallas_call(
        _mix_kernel,
        grid=(n // tm,),
        in_specs=[
            pl.BlockSpec((tm, D_MODEL), row),
            pl.BlockSpec((tm, ATT_WIDTH), row),
            pl.BlockSpec((SSM_GROUPS // 2, tm // SSM_CHUNK, SSM_CHUNK * SSM_PAIR_CH), lambda i: (0, i, 0)),
            pl.BlockSpec((tm, 2 * D_MODEL), row),
            pl.BlockSpec((SSM_WIDTH, SSM_WIDTH), fixed),
            pl.BlockSpec((1, SSM_WIDTH), fixed),
            pl.BlockSpec((ATT_WIDTH, D_MODEL), fixed),
            pl.BlockSpec((SSM_WIDTH, D_MODEL), fixed),
            pl.BlockSpec((D_MODEL, D_MODEL), fixed),
        ],
        out_specs=pl.BlockSpec((tm, D_MODEL), row),
        out_shape=jax.ShapeDtypeStruct((n, D_MODEL), F32),
        scratch_shapes=[pltpu.VMEM((SSM_WIDTH // LANES, tm, LANES), F32)],
        compiler_params=_params(("parallel",)),
        name="branch_mix",
    )(h, o, ys, gl, w_glu, b_glu.reshape(1, SSM_WIDTH), w_attn_br, w_ssm_br, w_out)


def _lane_max(x):
    return jnp.max(x, axis=-1, keepdims=True)


def _router(u, wr_ref, br_ref):
    logits = jnp.dot(u, wr_ref[...], preferred_element_type=F32) + br_ref[...]
    lane = lax.broadcasted_iota(jnp.int32, logits.shape, 1).astype(F32)
    far = float(4 * LANES)
    is_grp = (lane >= N_EXPERTS) & (lane < N_EXPERTS + N_GROUPS)
    lg = jnp.where(is_grp, logits, NEG_BIG)
    g_max = _lane_max(lg)
    p_sel = 1.0 / jnp.sum(jnp.where(is_grp, jnp.exp(lg - g_max), 0.0), axis=-1, keepdims=True)
    g_sel = jnp.min(jnp.where(lg == g_max, lane, far), axis=-1, keepdims=True) - N_EXPERTS
    g_lo = g_sel * EXPERTS_PER_GROUP
    in_grp = (lane >= g_lo) & (lane < g_lo + EXPERTS_PER_GROUP)
    l2 = jnp.where(in_grp, logits, NEG_BIG)
    m1 = _lane_max(l2)
    i1 = jnp.min(jnp.where(l2 == m1, lane, far), axis=-1, keepdims=True)
    l2b = jnp.where(lane == i1, NEG_BIG, l2)
    m2 = _lane_max(l2b)
    i2 = jnp.min(jnp.where(l2b == m2, lane, far), axis=-1, keepdims=True)
    e2 = jnp.exp(m2 - m1)
    w1 = 1.0 / (1.0 + e2)
    w2 = e2 * w1
    gates = p_sel * (jnp.where(lane == i1, w1, 0.0) + jnp.where(lane == i2, w2, 0.0))
    return gates, jnp.where(lane == g_sel, 1.0, 0.0)


def _moe_kernel(h_ref, g_ref, wr_ref, br_ref, tri_ref, w13_ref, w2_ref, out_ref,
                ts_ref, gs_ref, acc_ref, pt_ref, meta_ref, *, tm, win, eps):
    step = pl.program_id(1)

    @pl.when(step == 0)
    def _():
        u = _rms(h_ref[...], g_ref[...]).astype(BF16)
        gates, onehot = _router(u, wr_ref, br_ref)
        rank = jnp.dot(tri_ref[...], onehot.astype(BF16), preferred_element_type=F32)
        cnt = jnp.sum(onehot, axis=0, keepdims=True)
        off = pltpu.roll(cnt, 1, 1) + pltpu.roll(cnt, 2, 1) + pltpu.roll(cnt, 3, 1)
        pos = jnp.sum(onehot * (off + rank), axis=-1, keepdims=True)
        slot_lane = lax.broadcasted_iota(jnp.int32, (tm, tm), 1).astype(F32)
        pt_ref[...] = jnp.where(slot_lane == pos, 1.0, 0.0).astype(BF16)
        pos_row = jnp.broadcast_to(pos, (tm, LANES)).T[0:1, :]
        slot_sub = lax.broadcasted_iota(jnp.int32, (tm, tm), 0).astype(F32)
        perm = jnp.where(slot_sub == pos_row, 1.0, 0.0).astype(BF16)
        ts_ref[0:tm, :] = jnp.dot(perm, u, preferred_element_type=F32).astype(BF16)
        g_hi = gates.astype(BF16)
        g_lo = (gates - g_hi.astype(F32)).astype(BF16)
        gs_ref[0:tm, :] = (jnp.dot(perm, g_hi, preferred_element_type=F32)
                           + jnp.dot(perm, g_lo, preferred_element_type=F32))
        ts_ref[tm:tm + win, :] = jnp.zeros((win, D_MODEL), BF16)
        gs_ref[tm:tm + win, :] = jnp.zeros((win, LANES), F32)
        acc_ref[...] = jnp.zeros(acc_ref.shape, F32)
        off_i = off.astype(jnp.int32)
        end_i = (off + cnt).astype(jnp.int32)
        for g in range(N_GROUPS):
            meta_ref[0, g] = off_i[0, g]
            meta_ref[1, g] = end_i[0, g]

    e0 = step * eps
    grp = lax.shift_right_logical(e0, 3)
    start = lax.shift_left(lax.shift_right_logical(meta_ref[0, grp], 4), 4)
    span = meta_ref[1, grp] - start
    nwin = 1 + sum((span > k * win).astype(jnp.int32) for k in range(1, tm // win + 2))

    def window(i, c):
        r0 = pl.multiple_of(start + i * win, 16)
        x = ts_ref[pl.ds(r0, win), :]
        gsl = gs_ref[pl.ds(r0, win), :]
        lane = lax.broadcasted_iota(jnp.int32, gsl.shape, 1)
        y = None
        for k in range(eps):
            ab = jnp.dot(x, w13_ref[k], preferred_element_type=F32)
            a = ab[:, :EXPERT_FF]
            hdn = a * _sigmoid(a) * ab[:, EXPERT_FF:]
            ge = jnp.sum(jnp.where(lane == e0 + k, gsl, 0.0), axis=-1, keepdims=True)
            yk = jnp.dot((hdn * ge).astype(BF16), w2_ref[k], preferred_element_type=F32)
            y = yk if y is None else y + yk
        acc_ref[pl.ds(r0, win), :] += y
        return c
    lax.fori_loop(0, nwin, window, 0)

    @pl.when(step == pl.num_programs(1) - 1)
    def _():
        out_ref[...] = h_ref[...] + jnp.dot(pt_ref[...], acc_ref[0:tm, :].astype(BF16), preferred_element_type=F32)


def _moe(h, g_ffn, w_router, b_router, w13, w2, win=MOE_WINDOW, eps=MOE_EXPERTS_PER_STEP):
    n = h.shape[0]
    tm = _pick_tile(n, 1024)
    tri = jnp.tri(tm, tm, -1, dtype=BF16)
    row = lambda i, e: (i, 0)
    fixed = lambda i, e: (0, 0)
    kern = functools.partial(_moe_kernel, tm=tm, win=win, eps=eps)
    return pl.pallas_call(
        kern,
        grid=(n // tm, N_EXPERTS // eps),
        in_specs=[
            pl.BlockSpec((tm, D_MODEL), row),
            pl.BlockSpec((1, D_MODEL), fixed),
            pl.BlockSpec((D_MODEL, LANES), fixed),
            pl.BlockSpec((1, LANES), fixed),
            pl.BlockSpec((tm, tm), fixed),
            pl.BlockSpec((eps, D_MODEL, 2 * EXPERT_FF), lambda i, e: (e, 0, 0)),
            pl.BlockSpec((eps, EXPERT_FF, D_MODEL), lambda i, e: (e, 0, 0)),
        ],
        out_specs=pl.BlockSpec((tm, D_MODEL), row),
        out_shape=jax.ShapeDtypeStruct((n, D_MODEL), F32),
        scratch_shapes=[
            pltpu.VMEM((tm + win, D_MODEL), BF16),
            pltpu.VMEM((tm + win, LANES), F32),
            pltpu.VMEM((tm + win, D_MODEL), F32),
            pltpu.VMEM((tm, tm), BF16),
            pltpu.SMEM((2, N_GROUPS), jnp.int32),
        ],
        compiler_params=_params(("parallel", "arbitrary")),
        name="hier_moe",
    )(h, g_ffn.reshape(1, D_MODEL), w_router, b_router, tri, w13, w2)


def _ple_kernel(h_ref, p_ref, gp_ref, wg_ref, wp_ref, gf_ref, out_ref):
    h = h_ref[...]
    gate = _sigmoid(jnp.dot(_rms(h, gp_ref[...]).astype(BF16), wg_ref[...], preferred_element_type=F32))
    pe = jnp.dot(p_ref[...].astype(BF16), wp_ref[...], preferred_element_type=F32)
    out_ref[...] = _rms(h + gate * pe, gf_ref[...])


def _ple(h, p, g_ple, w_ple_gate, w_ple, g_final):
    n = h.shape[0]
    tm = _pick_tile(n, 512)
    row = lambda i: (i, 0)
    fixed = lambda i: (0, 0)
    return pl.pallas_call(
        _ple_kernel,
        grid=(n // tm,),
        in_specs=[
            pl.BlockSpec((tm, D_MODEL), row),
            pl.BlockSpec((tm, PLE_DIM), row),
            pl.BlockSpec((1, D_MODEL), fixed),
            pl.BlockSpec((D_MODEL, D_MODEL), fixed),
            pl.BlockSpec((PLE_DIM, D_MODEL), fixed),
            pl.BlockSpec((1, D_MODEL), fixed),
        ],
        out_specs=pl.BlockSpec((tm, D_MODEL), row),
        out_shape=jax.ShapeDtypeStruct((n, D_MODEL), F32),
        compiler_params=_params(("parallel",)),
        name="ple_final",
    )(h, p, g_ple.reshape(1, D_MODEL), w_ple_gate, w_ple, g_final.reshape(1, D_MODEL))


def _prep_weights(g_mix, w_in, lam_q1, lam_k1, lam_q2, lam_k2, subln_g,
                  ssm_lam_re, ssm_lam_im, ssm_log_dt, ssm_b_re, ssm_b_im, ssm_c_re, ssm_c_im, ssm_d,
                  w_glu, b_glu, w_attn_br, w_ssm_br, w_out, g_ffn, w_r1, b_r1, w_r2, b_r2,
                  w_e1, w_e3, w_e2, g_ple, w_ple_gate, w_ple, g_final):
    i = 0
    lam = (jnp.exp(jnp.sum(lam_q1[i] * lam_k1[i])) - jnp.exp(jnp.sum(lam_q2[i] * lam_k2[i])) + LAM_INIT)
    pad = LANES - N_EXPERTS - N_GROUPS
    w_router = jnp.concatenate([w_r2[i], w_r1[i], jnp.zeros((D_MODEL, pad), F32)], axis=1).astype(BF16)
    b_router = jnp.concatenate([b_r2[i], b_r1[i], jnp.zeros((pad,), F32)]).reshape(1, LANES)
    return dict(
        g_mix=g_mix[i], w_in=w_in[i].astype(BF16), lam=lam.astype(F32), subln_g=subln_g[i],
        ssm=_ssm_weights(ssm_lam_re[i], ssm_lam_im[i], ssm_log_dt[i], ssm_b_re[i], ssm_b_im[i],
                         ssm_c_re[i], ssm_c_im[i], ssm_d[i]),
        w_glu=w_glu[i].astype(BF16), b_glu=b_glu[i], w_attn_br=w_attn_br[i].astype(BF16),
        w_ssm_br=w_ssm_br[i].astype(BF16), w_out=w_out[i].astype(BF16), g_ffn=g_ffn[i],
        w_router=w_router, b_router=b_router,
        w13=jnp.concatenate([w_e1[i], w_e3[i]], axis=-1).astype(BF16), w2=w_e2[i].astype(BF16),
        g_ple=g_ple[i], w_ple_gate=w_ple_gate[i].astype(BF16), w_ple=w_ple[i].astype(BF16), g_final=g_final,
    )


def _trunk(x, p, w):
    bsz, seq, _ = x.shape
    n = bsz * seq
    x2 = x.reshape(n, D_MODEL)
    q, k, v, s_in, gl = _in_proj(x2, seq, w["g_mix"], w["w_in"], w["rope"])
    shp = (bsz, seq, ATT_WIDTH)
    o = _attention(q.reshape(shp), k.reshape(shp), v.reshape(shp), w["lam"], w["subln_g"])
    ys = _ssm(s_in, bsz, seq, w["ssm"])
    h = _mix(x2, o.reshape(n, ATT_WIDTH), ys, gl, w["w_glu"], w["b_glu"], w["w_attn_br"], w["w_ssm_br"], w["w_out"])
    h = _moe(h, w["g_ffn"], w["w_router"], w["b_router"], w["w13"], w["w2"])
    y = _ple(h, p[0].reshape(n, PLE_DIM), w["g_ple"], w["w_ple_gate"], w["w_ple"], w["g_final"])
    return y.reshape(bsz, seq, D_MODEL)


def kernel(x_prompt, x_sample, p_prompt, p_sample, g_mix, w_in, lam_q1, lam_k1, lam_q2, lam_k2, subln_g, ssm_lam_re, ssm_lam_im, ssm_log_dt, ssm_b_re, ssm_b_im, ssm_c_re, ssm_c_im, ssm_d, w_glu, b_glu, w_attn_br, w_ssm_br, w_out, g_ffn, w_r1, b_r1, w_r2, b_r2, w_e1, w_e3, w_e2, g_ple, w_ple_gate, w_ple, g_final):
    w = _prep_weights(g_mix, w_in, lam_q1, lam_k1, lam_q2, lam_k2, subln_g,
                      ssm_lam_re, ssm_lam_im, ssm_log_dt, ssm_b_re, ssm_b_im, ssm_c_re, ssm_c_im, ssm_d,
                      w_glu, b_glu, w_attn_br, w_ssm_br, w_out, g_ffn, w_r1, b_r1, w_r2, b_r2,
                      w_e1, w_e3, w_e2, g_ple, w_ple_gate, w_ple, g_final)
    w["rope"] = _rope_tables(max(x_prompt.shape[1], x_sample.shape[1]))
    return (_trunk(x_prompt, p_prompt, w), _trunk(x_sample, p_sample, w))
```

```python
import functools
import math

import jax
import jax.numpy as jnp
from jax import lax
from jax.experimental import pallas as pl
from jax.experimental.pallas import tpu as pltpu

F32 = jnp.float32
BF16 = jnp.bfloat16
HIGHEST = lax.Precision.HIGHEST

D_MODEL = 1024
N_HEADS = 8
HEAD_DIM = 64
ATT_V_DIM = 2 * HEAD_DIM
ATT_WIDTH = N_HEADS * ATT_V_DIM
ROT_DIM = HEAD_DIM // 4
ROPE_THETA = 500000.0
SSM_CH = 16
SSM_GROUPS = 32
SSM_WIDTH = SSM_GROUPS * SSM_CH
SSM_STATE = 64
N_GROUPS = 4
EXPERTS_PER_GROUP = 8
N_EXPERTS = N_GROUPS * EXPERTS_PER_GROUP
EXPERT_FF = 256
PLE_DIM = 256
EPS = 1e-6
LAM_INIT = 0.8 - 0.6 * math.exp(-0.3 * 0)

LANES = 128
SUBLANES = 8
SSM_CHUNK = 16
SSM_PAIR_LANES = 2 * SSM_STATE
SSM_PAIR_CH = 2 * SSM_CH
SSM_PIECES = 128 // SSM_PAIR_CH
VMEM_LIMIT = 56 * 1024 * 1024
NEG_BIG = -1e30
LOG2E = math.log2(math.e)
ATTN_ROW_BLOCK = 16
ATTN_OVERFLOW_MARGIN = 100.0
ATTN_SHORT_SEQ = 2048
ATTN_TILES_SHORT = (2048, 1024)
ATTN_TILES_LONG = (1024, 1024)
MOE_WINDOW = 320
MOE_EXPERTS_PER_STEP = 4


def _params(sem):
    return pltpu.CompilerParams(dimension_semantics=sem, vmem_limit_bytes=VMEM_LIMIT)


def _rms(x, g):
    return x * lax.rsqrt(jnp.mean(x * x, axis=-1, keepdims=True) + EPS) * g


def _sigmoid(x):
    return 1.0 / (1.0 + jnp.exp(-x))


def _pick_tile(n, pref):
    t = min(n, pref)
    while n % t:
        t //= 2
    return t


def _lane_piece(rows):
    return lax.shift_right_logical(lax.broadcasted_iota(jnp.int32, (rows, LANES), 1), SSM_PAIR_CH.bit_length() - 1)


def _to_chunk_major(tok_ref, out_ref, rows):
    T = SSM_CHUNK
    piece_id = _lane_piece(rows)
    for k in range(T // SSM_PIECES):
        xs = [[tok_ref[col, pl.ds(SSM_PIECES * k + m, rows, stride=T), :] for col in range(SSM_WIDTH // LANES)]
              for m in range(SSM_PIECES)]
        for gp in range(SSM_GROUPS // 2):
            col, src = divmod(gp, SSM_PIECES)
            dest = None
            for m in range(SSM_PIECES):
                piece = xs[m][col]
                shift = ((m - src) * SSM_PAIR_CH) % LANES
                if shift:
                    piece = pltpu.roll(piece, shift, 1)
                dest = piece if dest is None else jnp.where(piece_id == m, piece, dest)
            out_ref[gp, :, k * LANES:(k + 1) * LANES] = dest.astype(out_ref.dtype)


def _to_token_major(chunk_ref, tok_ref, rows):
    T = SSM_CHUNK
    piece_id = _lane_piece(rows)
    for t in range(T):
        col, src = divmod(t, SSM_PIECES)
        for k in range(SSM_WIDTH // LANES):
            dest = None
            for m in range(SSM_PIECES):
                piece = chunk_ref[SSM_PIECES * k + m, :, col * LANES:(col + 1) * LANES].astype(F32)
                shift = ((m - src) * SSM_PAIR_CH) % LANES
                if shift:
                    piece = pltpu.roll(piece, shift, 1)
                dest = piece if dest is None else jnp.where(piece_id == m, piece, dest)
            tok_ref[k, pl.ds(t, rows, stride=T), :] = dest


def _in_proj_kernel(x_ref, g_ref, w_ref, rope_ref, q_ref, k_ref, v_ref, s_ref, gl_ref, tok_ref):
    u = _rms(x_ref[...], g_ref[...]).astype(BF16)
    cos = rope_ref[:, 0:LANES]
    sa = rope_ref[:, LANES:2 * LANES]
    sb = rope_ref[:, 2 * LANES:3 * LANES]
    for sec, out_ref, scale in ((0, q_ref, HEAD_DIM ** -0.5 * LOG2E), (1, k_ref, 1.0)):
        y = jnp.dot(u, w_ref[:, sec * ATT_WIDTH:(sec + 1) * ATT_WIDTH], preferred_element_type=F32)
        for h in range(N_HEADS):
            yh = y[:, h * LANES:(h + 1) * LANES]
            r = yh * cos + pltpu.roll(yh, LANES - ROT_DIM // 2, 1) * sa + pltpu.roll(yh, ROT_DIM // 2, 1) * sb
            out_ref[:, h * LANES:(h + 1) * LANES] = (r * scale).astype(BF16)
    c0 = 2 * ATT_WIDTH
    v_ref[...] = jnp.dot(u, w_ref[:, c0:c0 + ATT_WIDTH], preferred_element_type=F32).astype(BF16)
    c0 += ATT_WIDTH
    s_tok = jnp.dot(u, w_ref[:, c0:c0 + SSM_WIDTH], preferred_element_type=F32)
    for col in range(SSM_WIDTH // LANES):
        tok_ref[col] = s_tok[:, col * LANES:(col + 1) * LANES]
    _to_chunk_major(tok_ref, s_ref, tok_ref.shape[1] // SSM_CHUNK)
    c0 += SSM_WIDTH
    gl_ref[...] = jnp.dot(u, w_ref[:, c0:c0 + 2 * D_MODEL], preferred_element_type=F32).astype(BF16)


def _rope_tables(seq):
    half = ROT_DIM // 2
    inv = ROPE_THETA ** (-jnp.arange(0, ROT_DIM, 2, dtype=F32) / ROT_DIM)
    ang = jnp.arange(seq, dtype=F32)[:, None] * inv[None, :]
    cos, sin = jnp.cos(ang), jnp.sin(ang)
    one = jnp.ones((seq, HEAD_DIM - ROT_DIM), F32)
    zero = jnp.zeros((seq, HEAD_DIM - ROT_DIM), F32)
    zh = jnp.zeros((seq, half), F32)
    cos_t = [cos, cos, one]
    sa_t = [-sin, zh, zero]
    sb_t = [zh, sin, zero]
    return jnp.concatenate(cos_t * 2 + sa_t * 2 + sb_t * 2, axis=1)


def _in_proj(x2, seq, g_mix, w_in, rope):
    n = x2.shape[0]
    tm = _pick_tile(seq, 512)
    nl = seq // tm
    in_cols = w_in.shape[1]
    row = lambda i: (i, 0)
    fixed = lambda i: (0, 0)
    tab = lambda i: (i % nl, 0)
    return pl.pallas_call(
        _in_proj_kernel,
        grid=(n // tm,),
        in_specs=[
            pl.BlockSpec((tm, D_MODEL), row),
            pl.BlockSpec((1, D_MODEL), fixed),
            pl.BlockSpec((D_MODEL, in_cols), fixed),
            pl.BlockSpec((tm, 3 * LANES), tab),
        ],
        out_specs=[
            pl.BlockSpec((tm, ATT_WIDTH), row),
            pl.BlockSpec((tm, ATT_WIDTH), row),
            pl.BlockSpec((tm, ATT_WIDTH), row),
            pl.BlockSpec((SSM_GROUPS // 2, tm // SSM_CHUNK, SSM_CHUNK * SSM_PAIR_CH), lambda i: (0, i, 0)),
            pl.BlockSpec((tm, 2 * D_MODEL), row),
        ],
        out_shape=[
            jax.ShapeDtypeStruct((n, ATT_WIDTH), BF16),
            jax.ShapeDtypeStruct((n, ATT_WIDTH), BF16),
            jax.ShapeDtypeStruct((n, ATT_WIDTH), BF16),
            jax.ShapeDtypeStruct((SSM_GROUPS // 2, n // SSM_CHUNK, SSM_CHUNK * SSM_PAIR_CH), BF16),
            jax.ShapeDtypeStruct((n, 2 * D_MODEL), BF16),
        ],
        scratch_shapes=[pltpu.VMEM((SSM_WIDTH // LANES, tm, LANES), F32)],
        compiler_params=_params(("parallel",)),
        name="in_proj",
    )(x2, g_mix.reshape(1, D_MODEL), w_in, rope)


def _col_reduce(x, pair_op, reduce_op):
    r = x.shape[0]
    rb = min(ATTN_ROW_BLOCK, r)
    out = None
    for i in range(r // rb):
        blk = x[i * rb:(i + 1) * rb]
        n = rb
        while n > SUBLANES:
            n //= 2
            blk = pair_op(blk[:n], blk[n:])
        out = blk if out is None else pair_op(out, blk)
    return reduce_op(out, axis=0, keepdims=True)


def _attn_kernel(lam_ref, q_ref, k_ref, v_ref, g_ref, o_ref, vt_ref, r_ref, l_ref, acc_ref, p_ref, be_ref, fl_ref,
                 *, tq, tk, nk):
    qi = pl.program_id(2)

    @pl.when(qi == 0)
    def _():
        def tr(j, c):
            off = pl.multiple_of(j * tk, tk)
            vt_ref[:, pl.ds(off, tk)] = v_ref[0, pl.ds(off, tk), :].astype(F32).T.astype(BF16)
            return c
        lax.fori_loop(0, nk, tr, 0)

    qt = q_ref[0].astype(F32).T
    row = lax.broadcasted_iota(jnp.int32, qt.shape, 0)
    qm = jnp.concatenate([jnp.where(row < HEAD_DIM, qt, 0.0),
                          jnp.where(row >= HEAD_DIM, qt, 0.0)], axis=1).astype(BF16)

    def scores(j):
        off = pl.multiple_of(j * tk, tk)
        return jnp.dot(k_ref[0, pl.ds(off, tk), :], qm, preferred_element_type=F32)

    def values(j, slot):
        off = pl.multiple_of(j * tk, tk)
        pv = jnp.dot(vt_ref[:, pl.ds(off, tk)], p_ref[slot], preferred_element_type=F32)
        acc_ref[...] = be_ref[slot] * (acc_ref[...] + pv)

    rb = min(ATTN_ROW_BLOCK, tk)
    head = jnp.dot(k_ref[0, 0:rb, :], qm, preferred_element_type=F32)
    r_ref[...] = _col_reduce(head, jnp.maximum, jnp.max)
    l_ref[...] = jnp.zeros(l_ref.shape, F32)
    fl_ref[...] = jnp.zeros(fl_ref.shape, F32)
    acc_ref[...] = jnp.zeros(acc_ref.shape, F32)

    def step(j, slot, with_values=True):
        s = scores(j)
        ref = r_ref[...]
        p = jnp.exp2(s - ref)
        cm = _col_reduce(s, jnp.maximum, jnp.max)
        new = jnp.maximum(ref, cm)
        beta = jnp.exp2(ref - new)
        fl_ref[...] = jnp.maximum(fl_ref[...], cm - ref)
        l_ref[...] = beta * (l_ref[...] + _col_reduce(p, jnp.add, jnp.sum))
        r_ref[...] = new
        be_ref[slot] = beta
        p_ref[slot] = p.astype(BF16)
        if with_values:
            values(j - 1, 1 - slot)

    step(0, 0, with_values=False)
    if nk > 1:
        def body(i, c):
            step(2 * i + 1, 1)
            step(2 * i + 2, 0)
            return c
        lax.fori_loop(0, (nk - 2) // 2, body, 0)
        step(nk - 1, 1)
    values(nk - 1, (nk - 1) % 2)

    @pl.when(jnp.max(fl_ref[...]) > ATTN_OVERFLOW_MARGIN)
    def _():
        r_ref[...] = jnp.full(r_ref.shape, NEG_BIG, F32)
        l_ref[...] = jnp.zeros(l_ref.shape, F32)
        acc_ref[...] = jnp.zeros(acc_ref.shape, F32)

        def safe(j, c):
            s = scores(j)
            m_old = r_ref[...]
            m_new = jnp.maximum(m_old, _col_reduce(s, jnp.maximum, jnp.max))
            alpha = jnp.exp2(m_old - m_new)
            p = jnp.exp2(s - m_new)
            l_ref[...] = alpha * l_ref[...] + _col_reduce(p, jnp.add, jnp.sum)
            r_ref[...] = m_new
            off = pl.multiple_of(j * tk, tk)
            pv = jnp.dot(vt_ref[:, pl.ds(off, tk)], p.astype(BF16), preferred_element_type=F32)
            acc_ref[...] = alpha * acc_ref[...] + pv
            return c
        lax.fori_loop(0, nk, safe, 0)

    lam = lam_ref[0]
    o_all = acc_ref[...] * (1.0 / l_ref[...])
    o = o_all[:, :tq] - lam * o_all[:, tq:]
    o = o * lax.rsqrt(jnp.mean(o * o, axis=0, keepdims=True) + EPS) * g_ref[...]
    o_ref[0] = (o * (1.0 - LAM_INIT)).T.astype(BF16)


def _attention(q, k, v, lam, subln_g):
    bsz, seq, _ = q.shape
    tq_pref, tk_pref = ATTN_TILES_SHORT if seq <= ATTN_SHORT_SEQ else ATTN_TILES_LONG
    tq = _pick_tile(seq, tq_pref)
    tk = _pick_tile(seq, tk_pref)
    nk = seq // tk
    assert nk == 1 or nk % 2 == 0
    kern = functools.partial(_attn_kernel, tq=tq, tk=tk, nk=nk)
    return pl.pallas_call(
        kern,
        grid=(bsz, N_HEADS, seq // tq),
        in_specs=[
            pl.BlockSpec(memory_space=pltpu.SMEM),
            pl.BlockSpec((1, tq, LANES), lambda b, h, i: (b, i, h)),
            pl.BlockSpec((1, seq, LANES), lambda b, h, i: (b, 0, h)),
            pl.BlockSpec((1, seq, LANES), lambda b, h, i: (b, 0, h)),
            pl.BlockSpec((ATT_V_DIM, 1), lambda b, h, i: (0, 0)),
        ],
        out_specs=pl.BlockSpec((1, tq, LANES), lambda b, h, i: (b, i, h)),
        out_shape=jax.ShapeDtypeStruct((bsz, seq, ATT_WIDTH), BF16),
        scratch_shapes=[
            pltpu.VMEM((ATT_V_DIM, seq), BF16),
            pltpu.VMEM((1, 2 * tq), F32),
            pltpu.VMEM((1, 2 * tq), F32),
            pltpu.VMEM((ATT_V_DIM, 2 * tq), F32),
            pltpu.VMEM((2, tk, 2 * tq), BF16),
            pltpu.VMEM((2, 1, 2 * tq), F32),
            pltpu.VMEM((1, 2 * tq), F32),
        ],
        compiler_params=_params(("parallel", "parallel", "arbitrary")),
        name="diff_attn",
    )(lam.reshape(1), q, k, v, subln_g.reshape(ATT_V_DIM, 1))


def _ssm_weights(lam_re, lam_im, log_dt, b_re, b_im, c_re, c_im, d_skip):
    T, G, P, CH = SSM_CHUNK, SSM_GROUPS, SSM_STATE, SSM_CH
    n = jnp.arange(T + 1, dtype=F32)[:, None, None]
    pw_re, pw_im, bb_re, bb_im = [], [], [], []
    for dirn in range(2):
        dt = jnp.exp(log_dt[dirn])[:, None]
        lr, li = lam_re[dirn], lam_im[dirn]
        mag = jnp.exp(lr * dt)
        a_re, a_im = mag * jnp.cos(li * dt), mag * jnp.sin(li * dt)
        den = lr * lr + li * li
        n_re = a_re - 1.0
        q_re = (n_re * lr + a_im * li) / den
        q_im = (a_im * lr - n_re * li) / den
        bb_re.append(q_re[..., None] * b_re[dirn] - q_im[..., None] * b_im[dirn])
        bb_im.append(q_re[..., None] * b_im[dirn] + q_im[..., None] * b_re[dirn])
        magn = jnp.exp(n * (lr * dt)[None])
        pw_re.append(magn * jnp.cos(n * (li * dt)[None]))
        pw_im.append(magn * jnp.sin(n * (li * dt)[None]))

    def cmul(ar, ai, br, bi):
        return ar * br - ai * bi, ar * bi + ai * br

    kern = []
    for dirn in range(2):
        car, cai = cmul(c_re[dirn][None], c_im[dirn][None],
                        pw_re[dirn][:T, :, None, :], pw_im[dirn][:T, :, None, :])
        kern.append(jnp.einsum('dgcp,gpk->dgck', car, bb_re[dirn], precision=HIGHEST)
                    - jnp.einsum('dgcp,gpk->dgck', cai, bb_im[dirn], precision=HIGHEST))
    tj = jnp.arange(T)
    lag = tj[None, :] - tj[:, None]
    k0 = kern[0][0] + kern[1][0] + jnp.eye(CH, dtype=F32)[None] * d_skip[:, :, None]
    k_lag = jnp.concatenate([kern[1][:0:-1], k0[None], kern[0][1:]], axis=0).astype(BF16)
    eye2 = jnp.eye(2, dtype=BF16)
    k_strip = k_lag.reshape(2 * T - 1, G // 2, 2, CH, CH).transpose(1, 2, 4, 0, 3)
    k_strip = k_strip[:, :, :, :, None, :] * eye2[None, :, None, None, :, None]
    k_strip = k_strip.reshape(G // 2, 2 * CH, (2 * T - 1) * 2 * CH).astype(F32)
    k_strip = jnp.pad(k_strip, ((0, 0), (0, 0), (0, 2 * CH)))

    def w_dir(dirn, steps):
        pr = pw_re[dirn][steps].transpose(1, 0, 2)[:, :, None, :]
        pi = pw_im[dirn][steps].transpose(1, 0, 2)[:, :, None, :]
        return cmul(pr, pi, bb_re[dirn].transpose(0, 2, 1)[:, None], bb_im[dirn].transpose(0, 2, 1)[:, None])
    wf_re, wf_im = w_dir(0, T - 1 - tj)
    wb_re, wb_im = w_dir(1, tj)
    w_all = jnp.stack([wf_re, wf_im, wb_re, wb_im], axis=3).astype(BF16)
    w_all = w_all.reshape(G // 2, 2, T, CH, 4, P).transpose(0, 2, 1, 3, 4, 5)
    w_pair = (w_all[:, :, :, :, :, None, :] * eye2[None, None, :, None, None, :, None]).reshape(G // 2, T * 2 * CH, 4 * 2 * P)

    def z_dir(dirn, steps):
        pr = pw_re[dirn][steps].transpose(1, 2, 0)[:, :, :, None]
        pi = pw_im[dirn][steps].transpose(1, 2, 0)[:, :, :, None]
        return cmul(c_re[dirn].transpose(0, 2, 1)[:, :, None, :], c_im[dirn].transpose(0, 2, 1)[:, :, None, :], pr, pi)
    zf_re, zf_im = z_dir(0, 1 + tj)
    zb_re, zb_im = z_dir(1, T - tj)
    z_all = jnp.stack([zf_re, -zf_im, zb_re, -zb_im], axis=1).astype(BF16)
    z_all = z_all.reshape(G // 2, 2, 4, P, T, CH)
    z_pair = (z_all.transpose(0, 2, 1, 3, 4, 5)[:, :, :, :, :, None, :] * eye2[None, None, :, None, None, :, None])
    z_pair = z_pair.reshape(G // 2, 4 * 2 * P, T * 2 * CH)

    a_tab = []
    for dirn in range(2):
        dt = jnp.exp(log_dt[dirn])[:, None]
        order = jnp.arange(SUBLANES) if dirn == 0 else SUBLANES - 1 - jnp.arange(SUBLANES)
        nn = jnp.concatenate([order, jnp.array([1, 2, 4, 8])]).astype(F32)[:, None, None] * T
        magn = jnp.exp(nn * (lam_re[dirn] * dt)[None])
        ang = nn * (lam_im[dirn] * dt)[None]
        tab = jnp.stack([magn * jnp.cos(ang), magn * jnp.sin(ang)], axis=0)
        tab = jnp.pad(tab, ((0, 0), (0, 2 * SUBLANES - tab.shape[1]), (0, 0), (0, 0)))
        a_tab.append(tab.reshape(2, 2 * SUBLANES, G // 2, 2 * P).transpose(2, 0, 1, 3))
    a_tab = jnp.stack(a_tab, axis=1)
    return k_strip, w_pair, z_pair, a_tab


def _tile_scan(x_re, x_im, a_tab, reverse):
    row = lax.broadcasted_iota(jnp.int32, x_re.shape, 0)
    for lvl, d in enumerate((1, 2, 4)):
        ar = a_tab[0][SUBLANES + lvl:SUBLANES + lvl + 1, :]
        ai = a_tab[1][SUBLANES + lvl:SUBLANES + lvl + 1, :]
        if reverse:
            keep = row < SUBLANES - d
            shift = SUBLANES - d
        else:
            keep = row >= d
            shift = d
        sr = jnp.where(keep, pltpu.roll(x_re, shift, 0), 0.0)
        si = jnp.where(keep, pltpu.roll(x_im, shift, 0), 0.0)
        x_re, x_im = x_re + ar * sr - ai * si, x_im + ar * si + ai * sr
    return x_re, x_im


def _ssm_kernel(u_ref, w_ref, k_ref, z_ref, a_ref, y_ref, v_ref, s_ref, *, nc):
    pl_ = SSM_PAIR_LANES
    nt = nc // SUBLANES
    u = u_ref[0]
    v_ref[...] = jnp.dot(u, w_ref[0], preferred_element_type=F32)

    row = lax.broadcasted_iota(jnp.int32, (SUBLANES, pl_), 0)

    def tile_step(i, carry):
        new = []
        for dirn in range(2):
            c_re, c_im = carry[2 * dirn], carry[2 * dirn + 1]
            a_tab = (a_ref[0, dirn, 0], a_ref[0, dirn, 1])
            t = i if dirn == 0 else nt - 1 - i
            r0 = pl.multiple_of(t * SUBLANES, SUBLANES)
            lo = 2 * dirn * pl_
            x_re, x_im = _tile_scan(v_ref[pl.ds(r0, SUBLANES), lo:lo + pl_],
                                    v_ref[pl.ds(r0, SUBLANES), lo + pl_:lo + 2 * pl_], a_tab, dirn == 1)
            if dirn == 0:
                keep, shift, last = row >= 1, 1, SUBLANES - 1
            else:
                keep, shift, last = row < SUBLANES - 1, SUBLANES - 1, 0
            pr, pi = a_tab[0][0:SUBLANES, :], a_tab[1][0:SUBLANES, :]
            s_ref[pl.ds(r0, SUBLANES), lo:lo + pl_] = (
                pr * c_re - pi * c_im + jnp.where(keep, pltpu.roll(x_re, shift, 0), 0.0))
            s_ref[pl.ds(r0, SUBLANES), lo + pl_:lo + 2 * pl_] = (
                pr * c_im + pi * c_re + jnp.where(keep, pltpu.roll(x_im, shift, 0), 0.0))
            a8r = a_tab[0][SUBLANES + 3:SUBLANES + 4, :]
            a8i = a_tab[1][SUBLANES + 3:SUBLANES + 4, :]
            new.append(a8r * c_re - a8i * c_im + x_re[last:last + 1, :])
            new.append(a8r * c_im + a8i * c_re + x_im[last:last + 1, :])
        return tuple(new)

    zero = jnp.zeros((1, pl_), F32)
    lax.fori_loop(0, nt, tile_step, (zero, zero, zero, zero))

    strip = k_ref[0]
    tw = SSM_CHUNK * SSM_PAIR_CH
    m_pair = jnp.concatenate([strip[:, (SSM_CHUNK - 1 - j) * SSM_PAIR_CH:(SSM_CHUNK - 1 - j) * SSM_PAIR_CH + tw]
                              for j in range(SSM_CHUNK)], axis=0).astype(BF16)
    ys = (jnp.dot(s_ref[...].astype(BF16), z_ref[0], preferred_element_type=F32)
          + jnp.dot(u, m_pair, preferred_element_type=F32))
    y_ref[0] = jax.nn.gelu(ys).astype(BF16)


def _ssm(ug, bsz, seq, weights):
    k_strip, w_pair, z_pair, a_tab = weights
    nc = seq // SSM_CHUNK
    tw = SSM_CHUNK * SSM_PAIR_CH
    kern = functools.partial(_ssm_kernel, nc=nc)
    return pl.pallas_call(
        kern,
        grid=(SSM_GROUPS // 2, bsz),
        in_specs=[
            pl.BlockSpec((1, nc, tw), lambda g, b: (g, b, 0)),
            pl.BlockSpec((1, tw, 4 * SSM_PAIR_LANES), lambda g, b: (g, 0, 0)),
            pl.BlockSpec((1, SSM_PAIR_CH, 2 * tw), lambda g, b: (g, 0, 0)),
            pl.BlockSpec((1, 4 * SSM_PAIR_LANES, tw), lambda g, b: (g, 0, 0)),
            pl.BlockSpec((1, 2, 2, 2 * SUBLANES, SSM_PAIR_LANES), lambda g, b: (g, 0, 0, 0, 0)),
        ],
        out_specs=pl.BlockSpec((1, nc, tw), lambda g, b: (g, b, 0)),
        out_shape=jax.ShapeDtypeStruct(ug.shape, BF16),
        scratch_shapes=[
            pltpu.VMEM((nc, 4 * SSM_PAIR_LANES), F32),
            pltpu.VMEM((nc, 4 * SSM_PAIR_LANES), F32),
        ],
        compiler_params=_params(("parallel", "arbitrary")),
        name="s5_scan",
    )(ug, w_pair, k_strip, z_pair, a_tab)


def _mix_kernel(h_ref, o_ref, yg_ref, gl_ref, wglu_ref, bglu_ref, wa_ref, ws_ref, wo_ref, out_ref, tok_ref):
    _to_token_major(yg_ref, tok_ref, tok_ref.shape[1] // SSM_CHUNK)
    yg = jnp.concatenate([tok_ref[col] for col in range(SSM_WIDTH // LANES)], axis=1)
    z = jnp.dot(yg.astype(BF16), wglu_ref[...], preferred_element_type=F32) + bglu_ref[...]
    ysg = (yg * _sigmoid(z)).astype(BF16)
    br_a = jnp.dot(o_ref[...], wa_ref[...], preferred_element_type=F32)
    br_s = jnp.dot(ysg, ws_ref[...], preferred_element_type=F32)
    gl = gl_ref[...].astype(F32)
    mix = _sigmoid(gl[:, :D_MODEL]) * br_a + _sigmoid(gl[:, D_MODEL:]) * br_s
    out_ref[...] = h_ref[...] + jnp.dot(mix.astype(BF16), wo_ref[...], preferred_element_type=F32)


def _mix(h, o, ys, gl, w_glu, b_glu, w_attn_br, w_ssm_br, w_out):
    n = h.shape[0]
    tm = _pick_tile(n, 512)
    row = lambda i: (i, 0)
    fixed = lambda i: (0, 0)
    return pl.pallas_call(
        _mix_kernel,
        grid=(n // tm,),
        in_specs=[
            pl.BlockSpec((tm, D_MODEL), row),
            pl.BlockSpec((tm, ATT_WIDTH), row),
            pl.BlockSpec((SSM_GROUPS // 2, tm // SSM_CHUNK, SSM_CHUNK * SSM_PAIR_CH), lambda i: (0, i, 0)),
            pl.BlockSpec((tm, 2 * D_MODEL), row),
            pl.BlockSpec((SSM_WIDTH, SSM_WIDTH), fixed),
            pl.BlockSpec((1, SSM_WIDTH), fixed),
            pl.BlockSpec((ATT_WIDTH, D_MODEL), fixed),
            pl.BlockSpec((SSM_WIDTH, D_MODEL), fixed),
            pl.BlockSpec((D_MODEL, D_MODEL), fixed),
        ],
        out_specs=pl.BlockSpec((tm, D_MODEL), row),
        out_shape=jax.ShapeDtypeStruct((n, D_MODEL), F32),
        scratch_shapes=[pltpu.VMEM((SSM_WIDTH // LANES, tm, LANES), F32)],
        compiler_params=_params(("parallel",)),
        name="branch_mix",
    )(h, o, ys, gl, w_glu, b_glu.reshape(1, SSM_WIDTH), w_attn_br, w_ssm_br, w_out)


def _lane_max(x):
    return jnp.max(x, axis=-1, keepdims=True)


def _router(u, wr_ref, br_ref):
    logits = jnp.dot(u, wr_ref[...], preferred_element_type=F32) + br_ref[...]
    lane = lax.broadcasted_iota(jnp.int32, logits.shape, 1).astype(F32)
    far = float(4 * LANES)
    is_grp = (lane >= N_EXPERTS) & (lane < N_EXPERTS + N_GROUPS)
    lg = jnp.where(is_grp, logits, NEG_BIG)
    g_max = _lane_max(lg)
    p_sel = 1.0 / jnp.sum(jnp.where(is_grp, jnp.exp(lg - g_max), 0.0), axis=-1, keepdims=True)
    g_sel = jnp.min(jnp.where(lg == g_max, lane, far), axis=-1, keepdims=True) - N_EXPERTS
    g_lo = g_sel * EXPERTS_PER_GROUP
    in_grp = (lane >= g_lo) & (lane < g_lo + EXPERTS_PER_GROUP)
    l2 = jnp.where(in_grp, logits, NEG_BIG)
    m1 = _lane_max(l2)
    i1 = jnp.min(jnp.where(l2 == m1, lane, far), axis=-1, keepdims=True)
    l2b = jnp.where(lane == i1, NEG_BIG, l2)
    m2 = _lane_max(l2b)
    i2 = jnp.min(jnp.where(l2b == m2, lane, far), axis=-1, keepdims=True)
    e2 = jnp.exp(m2 - m1)
    w1 = 1.0 / (1.0 + e2)
    w2 = e2 * w1
    gates = p_sel * (jnp.where(lane == i1, w1, 0.0) + jnp.where(lane == i2, w2, 0.0))
    return gates, jnp.where(lane == g_sel, 1.0, 0.0)


def _moe_kernel(h_ref, g_ref, wr_ref, br_ref, tri_ref, w13_ref, w2_ref, out_ref,
                ts_ref, gs_ref, acc_ref, pt_ref, meta_ref, *, tm, win, eps):
    step = pl.program_id(1)

    @pl.when(step == 0)
    def _():
        u = _rms(h_ref[...], g_ref[...]).astype(BF16)
        gates, onehot = _router(u, wr_ref, br_ref)
        rank = jnp.dot(tri_ref[...], onehot.astype(BF16), preferred_element_type=F32)
        cnt = jnp.sum(onehot, axis=0, keepdims=True)
        off = pltpu.roll(cnt, 1, 1) + pltpu.roll(cnt, 2, 1) + pltpu.roll(cnt, 3, 1)
        pos = jnp.sum(onehot * (off + rank), axis=-1, keepdims=True)
        slot_lane = lax.broadcasted_iota(jnp.int32, (tm, tm), 1).astype(F32)
        pt_ref[...] = jnp.where(slot_lane == pos, 1.0, 0.0).astype(BF16)
        pos_row = jnp.broadcast_to(pos, (tm, LANES)).T[0:1, :]
        slot_sub = lax.broadcasted_iota(jnp.int32, (tm, tm), 0).astype(F32)
        perm = jnp.where(slot_sub == pos_row, 1.0, 0.0).astype(BF16)
        ts_ref[0:tm, :] = jnp.dot(perm, u, preferred_element_type=F32).astype(BF16)
        g_hi = gates.astype(BF16)
        g_lo = (gates - g_hi.astype(F32)).astype(BF16)
        gs_ref[0:tm, :] = (jnp.dot(perm, g_hi, preferred_element_type=F32)
                           + jnp.dot(perm, g_lo, preferred_element_type=F32))
        ts_ref[tm:tm + win, :] = jnp.zeros((win, D_MODEL), BF16)
        gs_ref[tm:tm + win, :] = jnp.zeros((win, LANES), F32)
        acc_ref[...] = jnp.zeros(acc_ref.shape, F32)
        off_i = off.astype(jnp.int32)
        end_i = (off + cnt).astype(jnp.int32)
        for g in range(N_GROUPS):
            meta_ref[0, g] = off_i[0, g]
            meta_ref[1, g] = end_i[0, g]

    e0 = step * eps
    grp = lax.shift_right_logical(e0, 3)
    start = lax.shift_left(lax.shift_right_logical(meta_ref[0, grp], 4), 4)
    span = meta_ref[1, grp] - start
    nwin = 1 + sum((span > k * win).astype(jnp.int32) for k in range(1, tm // win + 2))

    def window(i, c):
        r0 = pl.multiple_of(start + i * win, 16)
        x = ts_ref[pl.ds(r0, win), :]
        gsl = gs_ref[pl.ds(r0, win), :]
        lane = lax.broadcasted_iota(jnp.int32, gsl.shape, 1)
        y = None
        for k in range(eps):
            ab = jnp.dot(x, w13_ref[k], preferred_element_type=F32)
            a = ab[:, :EXPERT_FF]
            hdn = a * _sigmoid(a) * ab[:, EXPERT_FF:]
            ge = jnp.sum(jnp.where(lane == e0 + k, gsl, 0.0), axis=-1, keepdims=True)
            yk = jnp.dot((hdn * ge).astype(BF16), w2_ref[k], preferred_element_type=F32)
            y = yk if y is None else y + yk
        acc_ref[pl.ds(r0, win), :] += y
        return c
    lax.fori_loop(0, nwin, window, 0)

    @pl.when(step == pl.num_programs(1) - 1)
    def _():
        out_ref[...] = h_ref[...] + jnp.dot(pt_ref[...], acc_ref[0:tm, :].astype(BF16), preferred_element_type=F32)


def _moe(h, g_ffn, w_router, b_router, w13, w2, win=MOE_WINDOW, eps=MOE_EXPERTS_PER_STEP):
    n = h.shape[0]
    tm = _pick_tile(n, 1024)
    tri = jnp.tri(tm, tm, -1, dtype=BF16)
    row = lambda i, e: (i, 0)
    fixed = lambda i, e: (0, 0)
    kern = functools.partial(_moe_kernel, tm=tm, win=win, eps=eps)
    return pl.pallas_call(
        kern,
        grid=(n // tm, N_EXPERTS // eps),
        in_specs=[
            pl.BlockSpec((tm, D_MODEL), row),
            pl.BlockSpec((1, D_MODEL), fixed),
            pl.BlockSpec((D_MODEL, LANES), fixed),
            pl.BlockSpec((1, LANES), fixed),
            pl.BlockSpec((tm, tm), fixed),
            pl.BlockSpec((eps, D_MODEL, 2 * EXPERT_FF), lambda i, e: (e, 0, 0)),
            pl.BlockSpec((eps, EXPERT_FF, D_MODEL), lambda i, e: (e, 0, 0)),
        ],
        out_specs=pl.BlockSpec((tm, D_MODEL), row),
        out_shape=jax.ShapeDtypeStruct((n, D_MODEL), F32),
        scratch_shapes=[
            pltpu.VMEM((tm + win, D_MODEL), BF16),
            pltpu.VMEM((tm + win, LANES), F32),
            pltpu.VMEM((tm + win, D_MODEL), F32),
            pltpu.VMEM((tm, tm), BF16),
            pltpu.SMEM((2, N_GROUPS), jnp.int32),
        ],
        compiler_params=_params(("parallel", "arbitrary")),
        name="hier_moe",
    )(h, g_ffn.reshape(1, D_MODEL), w_router, b_router, tri, w13, w2)


def _ple_kernel(h_ref, p_ref, gp_ref, wg_ref, wp_ref, gf_ref, out_ref):
    h = h_ref[...]
    gate = _sigmoid(jnp.dot(_rms(h, gp_ref[...]).astype(BF16), wg_ref[...], preferred_element_type=F32))
    pe = jnp.dot(p_ref[...].astype(BF16), wp_ref[...], preferred_element_type=F32)
    out_ref[...] = _rms(h + gate * pe, gf_ref[...])


def _ple(h, p, g_ple, w_ple_gate, w_ple, g_final):
    n = h.shape[0]
    tm = _pick_tile(n, 512)
    row = lambda i: (i, 0)
    fixed = lambda i: (0, 0)
    return pl.pallas_call(
        _ple_kernel,
        grid=(n // tm,),
        in_specs=[
            pl.BlockSpec((tm, D_MODEL), row),
            pl.BlockSpec((tm, PLE_DIM), row),
            pl.BlockSpec((1, D_MODEL), fixed),
            pl.BlockSpec((D_MODEL, D_MODEL), fixed),
            pl.BlockSpec((PLE_DIM, D_MODEL), fixed),
            pl.BlockSpec((1, D_MODEL), fixed),
        ],
        out_specs=pl.BlockSpec((tm, D_MODEL), row),
        out_shape=jax.ShapeDtypeStruct((n, D_MODEL), F32),
        compiler_params=_params(("parallel",)),
        name="ple_final",
    )(h, p, g_ple.reshape(1, D_MODEL), w_ple_gate, w_ple, g_final.reshape(1, D_MODEL))


def _prep_weights(g_mix, w_in, lam_q1, lam_k1, lam_q2, lam_k2, subln_g,
                  ssm_lam_re, ssm_lam_im, ssm_log_dt, ssm_b_re, ssm_b_im, ssm_c_re, ssm_c_im, ssm_d,
                  w_glu, b_glu, w_attn_br, w_ssm_br, w_out, g_ffn, w_r1, b_r1, w_r2, b_r2,
                  w_e1, w_e3, w_e2, g_ple, w_ple_gate, w_ple, g_final):
    i = 0
    lam = (jnp.exp(jnp.sum(lam_q1[i] * lam_k1[i])) - jnp.exp(jnp.sum(lam_q2[i] * lam_k2[i])) + LAM_INIT)
    pad = LANES - N_EXPERTS - N_GROUPS
    w_router = jnp.concatenate([w_r2[i], w_r1[i], jnp.zeros((D_MODEL, pad), F32)], axis=1).astype(BF16)
    b_router = jnp.concatenate([b_r2[i], b_r1[i], jnp.zeros((pad,), F32)]).reshape(1, LANES)
    return dict(
        g_mix=g_mix[i], w_in=w_in[i].astype(BF16), lam=lam.astype(F32), subln_g=subln_g[i],
        ssm=_ssm_weights(ssm_lam_re[i], ssm_lam_im[i], ssm_log_dt[i], ssm_b_re[i], ssm_b_im[i],
                         ssm_c_re[i], ssm_c_im[i], ssm_d[i]),
        w_glu=w_glu[i].astype(BF16), b_glu=b_glu[i], w_attn_br=w_attn_br[i].astype(BF16),
        w_ssm_br=w_ssm_br[i].astype(BF16), w_out=w_out[i].astype(BF16), g_ffn=g_ffn[i],
        w_router=w_router, b_router=b_router,
        w13=jnp.concatenate([w_e1[i], w_e3[i]], axis=-1).astype(BF16), w2=w_e2[i].astype(BF16),
        g_ple=g_ple[i], w_ple_gate=w_ple_gate[i].astype(BF16), w_ple=w_ple[i].astype(BF16), g_final=g_final,
    )


def _trunk(x, p, w):
    bsz, seq, _ = x.shape
    n = bsz * seq
    x2 = x.reshape(n, D_MODEL)
    q, k, v, s_in, gl = _in_proj(x2, seq, w["g_mix"], w["w_in"], w["rope"])
    shp = (bsz, seq, ATT_WIDTH)
    o = _attention(q.reshape(shp), k.reshape(shp), v.reshape(shp), w["lam"], w["subln_g"])
    ys = _ssm(s_in, bsz, seq, w["ssm"])
    h = _mix(x2, o.reshape(n, ATT_WIDTH), ys, gl, w["w_glu"], w["b_glu"], w["w_attn_br"], w["w_ssm_br"], w["w_out"])
    h = _moe(h, w["g_ffn"], w["w_router"], w["b_router"], w["w13"], w["w2"])
    y = _ple(h, p[0].reshape(n, PLE_DIM), w["g_ple"], w["w_ple_gate"], w["w_ple"], w["g_final"])
    return y.reshape(bsz, seq, D_MODEL)


def kernel(x_prompt, x_sample, p_prompt, p_sample, g_mix, w_in, lam_q1, lam_k1, lam_q2, lam_k2, subln_g, ssm_lam_re, ssm_lam_im, ssm_log_dt, ssm_b_re, ssm_b_im, ssm_c_re, ssm_c_im, ssm_d, w_glu, b_glu, w_attn_br, w_ssm_br, w_out, g_ffn, w_r1, b_r1, w_r2, b_r2, w_e1, w_e3, w_e2, g_ple, w_ple_gate, w_ple, g_final):
    w = _prep_weights(g_mix, w_in, lam_q1, lam_k1, lam_q2, lam_k2, subln_g,
                      ssm_lam_re, ssm_lam_im, ssm_log_dt, ssm_b_re, ssm_b_im, ssm_c_re, ssm_c_im, ssm_d,
                      w_glu, b_glu, w_attn_br, w_ssm_br, w_out, g_ffn, w_r1, b_r1, w_r2, b_r2,
                      w_e1, w_e3, w_e2, g_ple, w_ple_gate, w_ple, g_final)
    w["rope"] = _rope_tables(max(x_prompt.shape[1], x_sample.shape[1]))
    return (_trunk(x_prompt, p_prompt, w), _trunk(x_sample, p_sample, w))
```

```python
import functools
import math

import jax
import jax.numpy as jnp
from jax import lax
from jax.experimental import pallas as pl
from jax.experimental.pallas import tpu as pltpu

F32 = jnp.float32
BF16 = jnp.bfloat16
HIGHEST = lax.Precision.HIGHEST

D_MODEL = 1024
N_HEADS = 8
HEAD_DIM = 64
ATT_V_DIM = 2 * HEAD_DIM
ATT_WIDTH = N_HEADS * ATT_V_DIM
ROT_DIM = HEAD_DIM // 4
ROPE_THETA = 500000.0
SSM_CH = 16
SSM_GROUPS = 32
SSM_WIDTH = SSM_GROUPS * SSM_CH
SSM_STATE = 64
N_GROUPS = 4
EXPERTS_PER_GROUP = 8
N_EXPERTS = N_GROUPS * EXPERTS_PER_GROUP
EXPERT_FF = 256
PLE_DIM = 256
EPS = 1e-6
LAM_INIT = 0.8 - 0.6 * math.exp(-0.3 * 0)

LANES = 128
SUBLANES = 8
SSM_CHUNK = 16
SSM_PAIR_LANES = 2 * SSM_STATE
SSM_ROWS_PER_STEP = 1024
SSM_PAIR_CH = 2 * SSM_CH
SSM_PIECES = 128 // SSM_PAIR_CH
VMEM_LIMIT = 56 * 1024 * 1024
NEG_BIG = -1e30
LOG2E = math.log2(math.e)
ATTN_ROW_BLOCK = 16
ATTN_OVERFLOW_MARGIN = 100.0
ATTN_SHORT_SEQ = 2048
ATTN_TILES_SHORT = (2048, 1024)
ATTN_TILES_LONG = (1024, 1024)
MOE_WINDOW = 320
MOE_EXPERTS_PER_STEP = 4


def _params(sem):
    return pltpu.CompilerParams(dimension_semantics=sem, vmem_limit_bytes=VMEM_LIMIT)


def _rms(x, g):
    return x * lax.rsqrt(jnp.mean(x * x, axis=-1, keepdims=True) + EPS) * g


def _sigmoid(x):
    return 1.0 / (1.0 + jnp.exp(-x))


def _pick_tile(n, pref):
    t = min(n, pref)
    while n % t:
        t //= 2
    return t


def _lane_piece(rows):
    return lax.shift_right_logical(lax.broadcasted_iota(jnp.int32, (rows, LANES), 1), SSM_PAIR_CH.bit_length() - 1)


def _to_chunk_major(tok_ref, out_ref, rows):
    T = SSM_CHUNK
    piece_id = _lane_piece(rows)
    for k in range(T // SSM_PIECES):
        xs = [[tok_ref[col, pl.ds(SSM_PIECES * k + m, rows, stride=T), :] for col in range(SSM_WIDTH // LANES)]
              for m in range(SSM_PIECES)]
        for gp in range(SSM_GROUPS // 2):
            col, src = divmod(gp, SSM_PIECES)
            dest = None
            for m in range(SSM_PIECES):
                piece = xs[m][col]
                shift = ((m - src) * SSM_PAIR_CH) % LANES
                if shift:
                    piece = pltpu.roll(piece, shift, 1)
                dest = piece if dest is None else jnp.where(piece_id == m, piece, dest)
            out_ref[gp, :, k * LANES:(k + 1) * LANES] = dest.astype(out_ref.dtype)


def _to_token_major(chunk_ref, tok_ref, rows):
    T = SSM_CHUNK
    piece_id = _lane_piece(rows)
    for t in range(T):
        col, src = divmod(t, SSM_PIECES)
        for k in range(SSM_WIDTH // LANES):
            dest = None
            for m in range(SSM_PIECES):
                piece = chunk_ref[SSM_PIECES * k + m, :, col * LANES:(col + 1) * LANES].astype(F32)
                shift = ((m - src) * SSM_PAIR_CH) % LANES
                if shift:
                    piece = pltpu.roll(piece, shift, 1)
                dest = piece if dest is None else jnp.where(piece_id == m, piece, dest)
            tok_ref[k, pl.ds(t, rows, stride=T), :] = dest


def _in_proj_kernel(x_ref, g_ref, w_ref, rope_ref, q_ref, k_ref, v_ref, s_ref, gl_ref, tok_ref):
    u = _rms(x_ref[...], g_ref[...]).astype(BF16)
    cos = rope_ref[:, 0:LANES]
    sa = rope_ref[:, LANES:2 * LANES]
    sb = rope_ref[:, 2 * LANES:3 * LANES]
    for sec, out_ref, scale in ((0, q_ref, HEAD_DIM ** -0.5 * LOG2E), (1, k_ref, 1.0)):
        y = jnp.dot(u, w_ref[:, sec * ATT_WIDTH:(sec + 1) * ATT_WIDTH], preferred_element_type=F32)
        for h in range(N_HEADS):
            yh = y[:, h * LANES:(h + 1) * LANES]
            r = yh * cos + pltpu.roll(yh, LANES - ROT_DIM // 2, 1) * sa + pltpu.roll(yh, ROT_DIM // 2, 1) * sb
            out_ref[:, h * LANES:(h + 1) * LANES] = (r * scale).astype(BF16)
    c0 = 2 * ATT_WIDTH
    v_ref[...] = jnp.dot(u, w_ref[:, c0:c0 + ATT_WIDTH], preferred_element_type=F32).astype(BF16)
    c0 += ATT_WIDTH
    s_tok = jnp.dot(u, w_ref[:, c0:c0 + SSM_WIDTH], preferred_element_type=F32)
    for col in range(SSM_WIDTH // LANES):
        tok_ref[col] = s_tok[:, col * LANES:(col + 1) * LANES]
    _to_chunk_major(tok_ref, s_ref, tok_ref.shape[1] // SSM_CHUNK)
    c0 += SSM_WIDTH
    gl_ref[...] = jnp.dot(u, w_ref[:, c0:c0 + 2 * D_MODEL], preferred_element_type=F32).astype(BF16)


def _rope_tables(seq):
    half = ROT_DIM // 2
    inv = ROPE_THETA ** (-jnp.arange(0, ROT_DIM, 2, dtype=F32) / ROT_DIM)
    ang = jnp.arange(seq, dtype=F32)[:, None] * inv[None, :]
    cos, sin = jnp.cos(ang), jnp.sin(ang)
    one = jnp.ones((seq, HEAD_DIM - ROT_DIM), F32)
    zero = jnp.zeros((seq, HEAD_DIM - ROT_DIM), F32)
    zh = jnp.zeros((seq, half), F32)
    cos_t = [cos, cos, one]
    sa_t = [-sin, zh, zero]
    sb_t = [zh, sin, zero]
    return jnp.concatenate(cos_t * 2 + sa_t * 2 + sb_t * 2, axis=1)


def _in_proj(x2, seq, g_mix, w_in, rope):
    n = x2.shape[0]
    tm = _pick_tile(seq, 512)
    nl = seq // tm
    in_cols = w_in.shape[1]
    row = lambda i: (i, 0)
    fixed = lambda i: (0, 0)
    tab = lambda i: (i % nl, 0)
    return pl.pallas_call(
        _in_proj_kernel,
        grid=(n // tm,),
        in_specs=[
            pl.BlockSpec((tm, D_MODEL), row),
            pl.BlockSpec((1, D_MODEL), fixed),
            pl.BlockSpec((D_MODEL, in_cols), fixed),
            pl.BlockSpec((tm, 3 * LANES), tab),
        ],
        out_specs=[
            pl.BlockSpec((tm, ATT_WIDTH), row),
            pl.BlockSpec((tm, ATT_WIDTH), row),
            pl.BlockSpec((tm, ATT_WIDTH), row),
            pl.BlockSpec((SSM_GROUPS // 2, tm // SSM_CHUNK, SSM_CHUNK * SSM_PAIR_CH), lambda i: (0, i, 0)),
            pl.BlockSpec((tm, 2 * D_MODEL), row),
        ],
        out_shape=[
            jax.ShapeDtypeStruct((n, ATT_WIDTH), BF16),
            jax.ShapeDtypeStruct((n, ATT_WIDTH), BF16),
            jax.ShapeDtypeStruct((n, ATT_WIDTH), BF16),
            jax.ShapeDtypeStruct((SSM_GROUPS // 2, n // SSM_CHUNK, SSM_CHUNK * SSM_PAIR_CH), BF16),
            jax.ShapeDtypeStruct((n, 2 * D_MODEL), BF16),
        ],
        scratch_shapes=[pltpu.VMEM((SSM_WIDTH // LANES, tm, LANES), F32)],
        compiler_params=_params(("parallel",)),
        name="in_proj",
    )(x2, g_mix.reshape(1, D_MODEL), w_in, rope)


def _col_reduce(x, pair_op, reduce_op):
    r = x.shape[0]
    rb = min(ATTN_ROW_BLOCK, r)
    out = None
    for i in range(r // rb):
        blk = x[i * rb:(i + 1) * rb]
        n = rb
        while n > SUBLANES:
            n //= 2
            blk = pair_op(blk[:n], blk[n:])
        out = blk if out is None else pair_op(out, blk)
    return reduce_op(out, axis=0, keepdims=True)


def _attn_kernel(lam_ref, q_ref, k_ref, v_ref, g_ref, o_ref, vt_ref, r_ref, l_ref, acc_ref, p_ref, be_ref, fl_ref,
                 *, tq, tk, nk):
    qi = pl.program_id(2)

    @pl.when(qi == 0)
    def _():
        def tr(j, c):
            off = pl.multiple_of(j * tk, tk)
            vt_ref[:, pl.ds(off, tk)] = v_ref[0, pl.ds(off, tk), :].astype(F32).T.astype(BF16)
            return c
        lax.fori_loop(0, nk, tr, 0)

    qt = q_ref[0].astype(F32).T
    row = lax.broadcasted_iota(jnp.int32, qt.shape, 0)
    qm = jnp.concatenate([jnp.where(row < HEAD_DIM, qt, 0.0),
                          jnp.where(row >= HEAD_DIM, qt, 0.0)], axis=1).astype(BF16)

    def scores(j):
        off = pl.multiple_of(j * tk, tk)
        return jnp.dot(k_ref[0, pl.ds(off, tk), :], qm, preferred_element_type=F32)

    def values(j, slot):
        off = pl.multiple_of(j * tk, tk)
        pv = jnp.dot(vt_ref[:, pl.ds(off, tk)], p_ref[slot], preferred_element_type=F32)
        acc_ref[...] = be_ref[slot] * (acc_ref[...] + pv)

    rb = min(ATTN_ROW_BLOCK, tk)
    head = jnp.dot(k_ref[0, 0:rb, :], qm, preferred_element_type=F32)
    r_ref[...] = _col_reduce(head, jnp.maximum, jnp.max)
    l_ref[...] = jnp.zeros(l_ref.shape, F32)
    fl_ref[...] = jnp.zeros(fl_ref.shape, F32)
    acc_ref[...] = jnp.zeros(acc_ref.shape, F32)

    def step(j, slot, with_values=True):
        s = scores(j)
        ref = r_ref[...]
        p = jnp.exp2(s - ref)
        cm = _col_reduce(s, jnp.maximum, jnp.max)
        new = jnp.maximum(ref, cm)
        beta = jnp.exp2(ref - new)
        fl_ref[...] = jnp.maximum(fl_ref[...], cm - ref)
        l_ref[...] = beta * (l_ref[...] + _col_reduce(p, jnp.add, jnp.sum))
        r_ref[...] = new
        be_ref[slot] = beta
        p_ref[slot] = p.astype(BF16)
        if with_values:
            values(j - 1, 1 - slot)

    step(0, 0, with_values=False)
    if nk > 1:
        def body(i, c):
            step(2 * i + 1, 1)
            step(2 * i + 2, 0)
            return c
        lax.fori_loop(0, (nk - 2) // 2, body, 0)
        step(nk - 1, 1)
    values(nk - 1, (nk - 1) % 2)

    @pl.when(jnp.max(fl_ref[...]) > ATTN_OVERFLOW_MARGIN)
    def _():
        r_ref[...] = jnp.full(r_ref.shape, NEG_BIG, F32)
        l_ref[...] = jnp.zeros(l_ref.shape, F32)
        acc_ref[...] = jnp.zeros(acc_ref.shape, F32)

        def safe(j, c):
            s = scores(j)
            m_old = r_ref[...]
            m_new = jnp.maximum(m_old, _col_reduce(s, jnp.maximum, jnp.max))
            alpha = jnp.exp2(m_old - m_new)
            p = jnp.exp2(s - m_new)
            l_ref[...] = alpha * l_ref[...] + _col_reduce(p, jnp.add, jnp.sum)
            r_ref[...] = m_new
            off = pl.multiple_of(j * tk, tk)
            pv = jnp.dot(vt_ref[:, pl.ds(off, tk)], p.astype(BF16), preferred_element_type=F32)
            acc_ref[...] = alpha * acc_ref[...] + pv
            return c
        lax.fori_loop(0, nk, safe, 0)

    lam = lam_ref[0]
    o_all = acc_ref[...] * (1.0 / l_ref[...])
    o = o_all[:, :tq] - lam * o_all[:, tq:]
    o = o * lax.rsqrt(jnp.mean(o * o, axis=0, keepdims=True) + EPS) * g_ref[...]
    o_ref[0] = (o * (1.0 - LAM_INIT)).T.astype(BF16)


def _attention(q, k, v, lam, subln_g):
    bsz, seq, _ = q.shape
    tq_pref, tk_pref = ATTN_TILES_SHORT if seq <= ATTN_SHORT_SEQ else ATTN_TILES_LONG
    tq = _pick_tile(seq, tq_pref)
    tk = _pick_tile(seq, tk_pref)
    nk = seq // tk
    assert nk == 1 or nk % 2 == 0
    kern = functools.partial(_attn_kernel, tq=tq, tk=tk, nk=nk)
    return pl.pallas_call(
        kern,
        grid=(bsz, N_HEADS, seq // tq),
        in_specs=[
            pl.BlockSpec(memory_space=pltpu.SMEM),
            pl.BlockSpec((1, tq, LANES), lambda b, h, i: (b, i, h)),
            pl.BlockSpec((1, seq, LANES), lambda b, h, i: (b, 0, h)),
            pl.BlockSpec((1, seq, LANES), lambda b, h, i: (b, 0, h)),
            pl.BlockSpec((ATT_V_DIM, 1), lambda b, h, i: (0, 0)),
        ],
        out_specs=pl.BlockSpec((1, tq, LANES), lambda b, h, i: (b, i, h)),
        out_shape=jax.ShapeDtypeStruct((bsz, seq, ATT_WIDTH), BF16),
        scratch_shapes=[
            pltpu.VMEM((ATT_V_DIM, seq), BF16),
            pltpu.VMEM((1, 2 * tq), F32),
            pltpu.VMEM((1, 2 * tq), F32),
            pltpu.VMEM((ATT_V_DIM, 2 * tq), F32),
            pltpu.VMEM((2, tk, 2 * tq), BF16),
            pltpu.VMEM((2, 1, 2 * tq), F32),
            pltpu.VMEM((1, 2 * tq), F32),
        ],
        compiler_params=_params(("parallel", "parallel", "arbitrary")),
        name="diff_attn",
    )(lam.reshape(1), q, k, v, subln_g.reshape(ATT_V_DIM, 1))


def _ssm_weights(lam_re, lam_im, log_dt, b_re, b_im, c_re, c_im, d_skip):
    T, G, P, CH = SSM_CHUNK, SSM_GROUPS, SSM_STATE, SSM_CH
    n = jnp.arange(T + 1, dtype=F32)[:, None, None]
    pw_re, pw_im, bb_re, bb_im = [], [], [], []
    for dirn in range(2):
        dt = jnp.exp(log_dt[dirn])[:, None]
        lr, li = lam_re[dirn], lam_im[dirn]
        mag = jnp.exp(lr * dt)
        a_re, a_im = mag * jnp.cos(li * dt), mag * jnp.sin(li * dt)
        den = lr * lr + li * li
        n_re = a_re - 1.0
        q_re = (n_re * lr + a_im * li) / den
        q_im = (a_im * lr - n_re * li) / den
        bb_re.append(q_re[..., None] * b_re[dirn] - q_im[..., None] * b_im[dirn])
        bb_im.append(q_re[..., None] * b_im[dirn] + q_im[..., None] * b_re[dirn])
        magn = jnp.exp(n * (lr * dt)[None])
        pw_re.append(magn * jnp.cos(n * (li * dt)[None]))
        pw_im.append(magn * jnp.sin(n * (li * dt)[None]))

    def cmul(ar, ai, br, bi):
        return ar * br - ai * bi, ar * bi + ai * br

    kern = []
    for dirn in range(2):
        car, cai = cmul(c_re[dirn][None], c_im[dirn][None],
                        pw_re[dirn][:T, :, None, :], pw_im[dirn][:T, :, None, :])
        kern.append(jnp.einsum('dgcp,gpk->dgck', car, bb_re[dirn], precision=HIGHEST)
                    - jnp.einsum('dgcp,gpk->dgck', cai, bb_im[dirn], precision=HIGHEST))
    tj = jnp.arange(T)
    lag = tj[None, :] - tj[:, None]
    k0 = kern[0][0] + kern[1][0] + jnp.eye(CH, dtype=F32)[None] * d_skip[:, :, None]
    k_lag = jnp.concatenate([kern[1][:0:-1], k0[None], kern[0][1:]], axis=0).astype(BF16)
    eye2 = jnp.eye(2, dtype=BF16)
    k_strip = k_lag.reshape(2 * T - 1, G // 2, 2, CH, CH).transpose(1, 2, 4, 0, 3)
    k_strip = k_strip[:, :, :, :, None, :] * eye2[None, :, None, None, :, None]
    k_strip = k_strip.reshape(G // 2, 2 * CH, (2 * T - 1) * 2 * CH).astype(F32)
    k_strip = jnp.pad(k_strip, ((0, 0), (0, 0), (0, 2 * CH)))

    def w_dir(dirn, steps):
        pr = pw_re[dirn][steps].transpose(1, 0, 2)[:, :, None, :]
        pi = pw_im[dirn][steps].transpose(1, 0, 2)[:, :, None, :]
        return cmul(pr, pi, bb_re[dirn].transpose(0, 2, 1)[:, None], bb_im[dirn].transpose(0, 2, 1)[:, None])
    wf_re, wf_im = w_dir(0, T - 1 - tj)
    wb_re, wb_im = w_dir(1, tj)
    w_all = jnp.stack([wf_re, wf_im, wb_re, wb_im], axis=3).astype(BF16)
    w_all = w_all.reshape(G // 2, 2, T, CH, 4, P).transpose(0, 2, 1, 3, 4, 5)
    w_pair = (w_all[:, :, :, :, :, None, :] * eye2[None, None, :, None, None, :, None]).reshape(G // 2, T * 2 * CH, 4 * 2 * P)

    def z_dir(dirn, steps):
        pr = pw_re[dirn][steps].transpose(1, 2, 0)[:, :, :, None]
        pi = pw_im[dirn][steps].transpose(1, 2, 0)[:, :, :, None]
        return cmul(c_re[dirn].transpose(0, 2, 1)[:, :, None, :], c_im[dirn].transpose(0, 2, 1)[:, :, None, :], pr, pi)
    zf_re, zf_im = z_dir(0, 1 + tj)
    zb_re, zb_im = z_dir(1, T - tj)
    z_all = jnp.stack([zf_re, -zf_im, zb_re, -zb_im], axis=1).astype(BF16)
    z_all = z_all.reshape(G // 2, 2, 4, P, T, CH)
    z_pair = (z_all.transpose(0, 2, 1, 3, 4, 5)[:, :, :, :, :, None, :] * eye2[None, None, :, None, None, :, None])
    z_pair = z_pair.reshape(G // 2, 4 * 2 * P, T * 2 * CH)

    a_tab = []
    for dirn in range(2):
        dt = jnp.exp(log_dt[dirn])[:, None]
        order = jnp.arange(SUBLANES) if dirn == 0 else SUBLANES - 1 - jnp.arange(SUBLANES)
        nn = jnp.concatenate([order, jnp.array([1, 2, 4, 8])]).astype(F32)[:, None, None] * T
        magn = jnp.exp(nn * (lam_re[dirn] * dt)[None])
        ang = nn * (lam_im[dirn] * dt)[None]
        tab = jnp.stack([magn * jnp.cos(ang), magn * jnp.sin(ang)], axis=0)
        tab = jnp.pad(tab, ((0, 0), (0, 2 * SUBLANES - tab.shape[1]), (0, 0), (0, 0)))
        a_tab.append(tab.reshape(2, 2 * SUBLANES, G // 2, 2 * P).transpose(2, 0, 1, 3))
    a_tab = jnp.stack(a_tab, axis=1)
    return k_strip, w_pair, z_pair, a_tab


def _tile_scan(x_re, x_im, a_tab, reverse):
    row = lax.broadcasted_iota(jnp.int32, x_re.shape, 0)
    for lvl, d in enumerate((1, 2, 4)):
        ar = a_tab[0][SUBLANES + lvl:SUBLANES + lvl + 1, :]
        ai = a_tab[1][SUBLANES + lvl:SUBLANES + lvl + 1, :]
        if reverse:
            keep = row < SUBLANES - d
            shift = SUBLANES - d
        else:
            keep = row >= d
            shift = d
        sr = jnp.where(keep, pltpu.roll(x_re, shift, 0), 0.0)
        si = jnp.where(keep, pltpu.roll(x_im, shift, 0), 0.0)
        x_re, x_im = x_re + ar * sr - ai * si, x_im + ar * si + ai * sr
    return x_re, x_im


def _ssm_kernel(u_ref, w_ref, k_ref, z_ref, a_ref, y_ref, v_ref, s_ref, *, nc, nb):
    pl_ = SSM_PAIR_LANES
    nt = nc // SUBLANES
    u = u_ref[0]
    v_ref[...] = jnp.dot(u, w_ref[0], preferred_element_type=F32)

    row = lax.broadcasted_iota(jnp.int32, (SUBLANES, pl_), 0)

    def tile_step(i, carry, base):
        new = []
        for dirn in range(2):
            c_re, c_im = carry[2 * dirn], carry[2 * dirn + 1]
            a_tab = (a_ref[0, dirn, 0], a_ref[0, dirn, 1])
            t = i if dirn == 0 else nt - 1 - i
            r0 = pl.multiple_of(base + t * SUBLANES, SUBLANES)
            lo = 2 * dirn * pl_
            x_re, x_im = _tile_scan(v_ref[pl.ds(r0, SUBLANES), lo:lo + pl_],
                                    v_ref[pl.ds(r0, SUBLANES), lo + pl_:lo + 2 * pl_], a_tab, dirn == 1)
            if dirn == 0:
                keep, shift, last = row >= 1, 1, SUBLANES - 1
            else:
                keep, shift, last = row < SUBLANES - 1, SUBLANES - 1, 0
            pr, pi = a_tab[0][0:SUBLANES, :], a_tab[1][0:SUBLANES, :]
            s_ref[pl.ds(r0, SUBLANES), lo:lo + pl_] = (
                pr * c_re - pi * c_im + jnp.where(keep, pltpu.roll(x_re, shift, 0), 0.0))
            s_ref[pl.ds(r0, SUBLANES), lo + pl_:lo + 2 * pl_] = (
                pr * c_im + pi * c_re + jnp.where(keep, pltpu.roll(x_im, shift, 0), 0.0))
            a8r = a_tab[0][SUBLANES + 3:SUBLANES + 4, :]
            a8i = a_tab[1][SUBLANES + 3:SUBLANES + 4, :]
            new.append(a8r * c_re - a8i * c_im + x_re[last:last + 1, :])
            new.append(a8r * c_im + a8i * c_re + x_im[last:last + 1, :])
        return tuple(new)

    zero = jnp.zeros((1, pl_), F32)

    def sequence(bi, c):
        lax.fori_loop(0, nt, functools.partial(tile_step, base=bi * nc), (zero, zero, zero, zero))
        return c
    lax.fori_loop(0, nb, sequence, 0)

    strip = k_ref[0]
    tw = SSM_CHUNK * SSM_PAIR_CH
    m_pair = jnp.concatenate([strip[:, (SSM_CHUNK - 1 - j) * SSM_PAIR_CH:(SSM_CHUNK - 1 - j) * SSM_PAIR_CH + tw]
                              for j in range(SSM_CHUNK)], axis=0).astype(BF16)
    ys = (jnp.dot(s_ref[...].astype(BF16), z_ref[0], preferred_element_type=F32)
          + jnp.dot(u, m_pair, preferred_element_type=F32))
    y_ref[0] = jax.nn.gelu(ys).astype(BF16)


def _ssm(ug, bsz, seq, weights):
    k_strip, w_pair, z_pair, a_tab = weights
    nc = seq // SSM_CHUNK
    tw = SSM_CHUNK * SSM_PAIR_CH
    nb = max(d for d in range(1, bsz + 1) if bsz % d == 0 and d * nc <= max(nc, SSM_ROWS_PER_STEP))
    kern = functools.partial(_ssm_kernel, nc=nc, nb=nb)
    return pl.pallas_call(
        kern,
        grid=(SSM_GROUPS // 2, bsz // nb),
        in_specs=[
            pl.BlockSpec((1, nb * nc, tw), lambda g, b: (g, b, 0)),
            pl.BlockSpec((1, tw, 4 * SSM_PAIR_LANES), lambda g, b: (g, 0, 0)),
            pl.BlockSpec((1, SSM_PAIR_CH, 2 * tw), lambda g, b: (g, 0, 0)),
            pl.BlockSpec((1, 4 * SSM_PAIR_LANES, tw), lambda g, b: (g, 0, 0)),
            pl.BlockSpec((1, 2, 2, 2 * SUBLANES, SSM_PAIR_LANES), lambda g, b: (g, 0, 0, 0, 0)),
        ],
        out_specs=pl.BlockSpec((1, nb * nc, tw), lambda g, b: (g, b, 0)),
        out_shape=jax.ShapeDtypeStruct(ug.shape, BF16),
        scratch_shapes=[
            pltpu.VMEM((nb * nc, 4 * SSM_PAIR_LANES), F32),
            pltpu.VMEM((nb * nc, 4 * SSM_PAIR_LANES), F32),
        ],
        compiler_params=_params(("parallel", "arbitrary")),
        name="s5_scan",
    )(ug, w_pair, k_strip, z_pair, a_tab)


def _mix_kernel(h_ref, o_ref, yg_ref, gl_ref, wglu_ref, bglu_ref, wa_ref, ws_ref, wo_ref, out_ref, tok_ref):
    _to_token_major(yg_ref, tok_ref, tok_ref.shape[1] // SSM_CHUNK)
    yg = jnp.concatenate([tok_ref[col] for col in range(SSM_WIDTH // LANES)], axis=1)
    z = jnp.dot(yg.astype(BF16), wglu_ref[...], preferred_element_type=F32) + bglu_ref[...]
    ysg = (yg * _sigmoid(z)).astype(BF16)
    br_a = jnp.dot(o_ref[...], wa_ref[...], preferred_element_type=F32)
    br_s = jnp.dot(ysg, ws_ref[...], preferred_element_type=F32)
    gl = gl_ref[...].astype(F32)
    mix = _sigmoid(gl[:, :D_MODEL]) * br_a + _sigmoid(gl[:, D_MODEL:]) * br_s
    out_ref[...] = h_ref[...] + jnp.dot(mix.astype(BF16), wo_ref[...], preferred_element_type=F32)


def _mix(h, o, ys, gl, w_glu, b_glu, w_attn_br, w_ssm_br, w_out):
    n = h.shape[0]
    tm = _pick_tile(n, 512)
    row = lambda i: (i, 0)
    fixed = lambda i: (0, 0)
    return pl.pallas_call(
        _mix_kernel,
        grid=(n // tm,),
        in_specs=[
            pl.BlockSpec((tm, D_MODEL), row),
            pl.BlockSpec((tm, ATT_WIDTH), row),
            pl.BlockSpec((SSM_GROUPS // 2, tm // SSM_CHUNK, SSM_CHUNK * SSM_PAIR_CH), lambda i: (0, i, 0)),
            pl.BlockSpec((tm, 2 * D_MODEL), row),
            pl.BlockSpec((SSM_WIDTH, SSM_WIDTH), fixed),
            pl.BlockSpec((1, SSM_WIDTH), fixed),
            pl.BlockSpec((ATT_WIDTH, D_MODEL), fixed),
            pl.BlockSpec((SSM_WIDTH, D_MODEL), fixed),
            pl.BlockSpec((D_MODEL, D_MODEL), fixed),
        ],
        out_specs=pl.BlockSpec((tm, D_MODEL), row),
        out_shape=jax.ShapeDtypeStruct((n, D_MODEL), F32),
        scratch_shapes=[pltpu.VMEM((SSM_WIDTH // LANES, tm, LANES), F32)],
        compiler_params=_params(("parallel",)),
        name="branch_mix",
    )(h, o, ys, gl, w_glu, b_glu.reshape(1, SSM_WIDTH), w_attn_br, w_ssm_br, w_out)


def _lane_max(x):
    return jnp.max(x, axis=-1, keepdims=True)


def _router(u, wr_ref, br_ref):
    logits = jnp.dot(u, wr_ref[...], preferred_element_type=F32) + br_ref[...]
    lane = lax.broadcasted_iota(jnp.int32, logits.shape, 1).astype(F32)
    far = float(4 * LANES)
    is_grp = (lane >= N_EXPERTS) & (lane < N_EXPERTS + N_GROUPS)
    lg = jnp.where(is_grp, logits, NEG_BIG)
    g_max = _lane_max(lg)
    p_sel = 1.0 / jnp.sum(jnp.where(is_grp, jnp.exp(lg - g_max), 0.0), axis=-1, keepdims=True)
    g_sel = jnp.min(jnp.where(lg == g_max, lane, far), axis=-1, keepdims=True) - N_EXPERTS
    g_lo = g_sel * EXPERTS_PER_GROUP
    in_grp = (lane >= g_lo) & (lane < g_lo + EXPERTS_PER_GROUP)
    l2 = jnp.where(in_grp, logits, NEG_BIG)
    m1 = _lane_max(l2)
    i1 = jnp.min(jnp.where(l2 == m1, lane, far), axis=-1, keepdims=True)
    l2b = jnp.where(lane == i1, NEG_BIG, l2)
    m2 = _lane_max(l2b)
    i2 = jnp.min(jnp.where(l2b == m2, lane, far), axis=-1, keepdims=True)
    e2 = jnp.exp(m2 - m1)
    w1 = 1.0 / (1.0 + e2)
    w2 = e2 * w1
    gates = p_sel * (jnp.where(lane == i1, w1, 0.0) + jnp.where(lane == i2, w2, 0.0))
    return gates, jnp.where(lane == g_sel, 1.0, 0.0)


def _moe_kernel(h_ref, g_ref, wr_ref, br_ref, tri_ref, w13_ref, w2_ref, out_ref,
                ts_ref, gs_ref, acc_ref, pt_ref, meta_ref, *, tm, win, eps):
    step = pl.program_id(1)

    @pl.when(step == 0)
    def _():
        u = _rms(h_ref[...], g_ref[...]).astype(BF16)
        gates, onehot = _router(u, wr_ref, br_ref)
        rank = jnp.dot(tri_ref[...], onehot.astype(BF16), preferred_element_type=F32)
        cnt = jnp.sum(onehot, axis=0, keepdims=True)
        off = pltpu.roll(cnt, 1, 1) + pltpu.roll(cnt, 2, 1) + pltpu.roll(cnt, 3, 1)
        pos = jnp.sum(onehot * (off + rank), axis=-1, keepdims=True)
        slot_lane = lax.broadcasted_iota(jnp.int32, (tm, tm), 1).astype(F32)
        pt_ref[...] = jnp.where(slot_lane == pos, 1.0, 0.0).astype(BF16)
        pos_row = jnp.broadcast_to(pos, (tm, LANES)).T[0:1, :]
        slot_sub = lax.broadcasted_iota(jnp.int32, (tm, tm), 0).astype(F32)
        perm = jnp.where(slot_sub == pos_row, 1.0, 0.0).astype(BF16)
        ts_ref[0:tm, :] = jnp.dot(perm, u, preferred_element_type=F32).astype(BF16)
        g_hi = gates.astype(BF16)
        g_lo = (gates - g_hi.astype(F32)).astype(BF16)
        gs_ref[0:tm, :] = (jnp.dot(perm, g_hi, preferred_element_type=F32)
                           + jnp.dot(perm, g_lo, preferred_element_type=F32))
        ts_ref[tm:tm + win, :] = jnp.zeros((win, D_MODEL), BF16)
        gs_ref[tm:tm + win, :] = jnp.zeros((win, LANES), F32)
        acc_ref[...] = jnp.zeros(acc_ref.shape, F32)
        off_i = off.astype(jnp.int32)
        end_i = (off + cnt).astype(jnp.int32)
        for g in range(N_GROUPS):
            meta_ref[0, g] = off_i[0, g]
            meta_ref[1, g] = end_i[0, g]

    e0 = step * eps
    grp = lax.shift_right_logical(e0, 3)
    start = lax.shift_left(lax.shift_right_logical(meta_ref[0, grp], 4), 4)
    span = meta_ref[1, grp] - start
    nwin = 1 + sum((span > k * win).astype(jnp.int32) for k in range(1, tm // win + 2))

    def window(i, c):
        r0 = pl.multiple_of(start + i * win, 16)
        x = ts_ref[pl.ds(r0, win), :]
        gsl = gs_ref[pl.ds(r0, win), :]
        lane = lax.broadcasted_iota(jnp.int32, gsl.shape, 1)
        y = None
        for k in range(eps):
            ab = jnp.dot(x, w13_ref[k], preferred_element_type=F32)
            a = ab[:, :EXPERT_FF]
            hdn = a * _sigmoid(a) * ab[:, EXPERT_FF:]
            ge = jnp.sum(jnp.where(lane == e0 + k, gsl, 0.0), axis=-1, keepdims=True)
            yk = jnp.dot((hdn * ge).astype(BF16), w2_ref[k], preferred_element_type=F32)
            y = yk if y is None else y + yk
        acc_ref[pl.ds(r0, win), :] += y
        return c
    lax.fori_loop(0, nwin, window, 0)

    @pl.when(step == pl.num_programs(1) - 1)
    def _():
        out_ref[...] = h_ref[...] + jnp.dot(pt_ref[...], acc_ref[0:tm, :].astype(BF16), preferred_element_type=F32)


def _moe(h, g_ffn, w_router, b_router, w13, w2, win=MOE_WINDOW, eps=MOE_EXPERTS_PER_STEP):
    n = h.shape[0]
    tm = _pick_tile(n, 1024)
    tri = jnp.tri(tm, tm, -1, dtype=BF16)
    row = lambda i, e: (i, 0)
    fixed = lambda i, e: (0, 0)
    kern = functools.partial(_moe_kernel, tm=tm, win=win, eps=eps)
    return pl.pallas_call(
        kern,
        grid=(n // tm, N_EXPERTS // eps),
        in_specs=[
            pl.BlockSpec((tm, D_MODEL), row),
            pl.BlockSpec((1, D_MODEL), fixed),
            pl.BlockSpec((D_MODEL, LANES), fixed),
            pl.BlockSpec((1, LANES), fixed),
            pl.BlockSpec((tm, tm), fixed),
            pl.BlockSpec((eps, D_MODEL, 2 * EXPERT_FF), lambda i, e: (e, 0, 0)),
            pl.BlockSpec((eps, EXPERT_FF, D_MODEL), lambda i, e: (e, 0, 0)),
        ],
        out_specs=pl.BlockSpec((tm, D_MODEL), row),
        out_shape=jax.ShapeDtypeStruct((n, D_MODEL), F32),
        scratch_shapes=[
            pltpu.VMEM((tm + win, D_MODEL), BF16),
            pltpu.VMEM((tm + win, LANES), F32),
            pltpu.VMEM((tm + win, D_MODEL), F32),
            pltpu.VMEM((tm, tm), BF16),
            pltpu.SMEM((2, N_GROUPS), jnp.int32),
        ],
        compiler_params=_params(("parallel", "arbitrary")),
        name="hier_moe",
    )(h, g_ffn.reshape(1, D_MODEL), w_router, b_router, tri, w13, w2)


def _ple_kernel(h_ref, p_ref, gp_ref, wg_ref, wp_ref, gf_ref, out_ref):
    h = h_ref[...]
    gate = _sigmoid(jnp.dot(_rms(h, gp_ref[...]).astype(BF16), wg_ref[...], preferred_element_type=F32))
    pe = jnp.dot(p_ref[...].astype(BF16), wp_ref[...], preferred_element_type=F32)
    out_ref[...] = _rms(h + gate * pe, gf_ref[...])


def _ple(h, p, g_ple, w_ple_gate, w_ple, g_final):
    n = h.shape[0]
    tm = _pick_tile(n, 512)
    row = lambda i: (i, 0)
    fixed = lambda i: (0, 0)
    return pl.pallas_call(
        _ple_kernel,
        grid=(n // tm,),
        in_specs=[
            pl.BlockSpec((tm, D_MODEL), row),
            pl.BlockSpec((tm, PLE_DIM), row),
            pl.BlockSpec((1, D_MODEL), fixed),
            pl.BlockSpec((D_MODEL, D_MODEL), fixed),
            pl.BlockSpec((PLE_DIM, D_MODEL), fixed),
            pl.BlockSpec((1, D_MODEL), fixed),
        ],
        out_specs=pl.BlockSpec((tm, D_MODEL), row),
        out_shape=jax.ShapeDtypeStruct((n, D_MODEL), F32),
        compiler_params=_params(("parallel",)),
        name="ple_final",
    )(h, p, g_ple.reshape(1, D_MODEL), w_ple_gate, w_ple, g_final.reshape(1, D_MODEL))


def _prep_weights(g_mix, w_in, lam_q1, lam_k1, lam_q2, lam_k2, subln_g,
                  ssm_lam_re, ssm_lam_im, ssm_log_dt, ssm_b_re, ssm_b_im, ssm_c_re, ssm_c_im, ssm_d,
                  w_glu, b_glu, w_attn_br, w_ssm_br, w_out, g_ffn, w_r1, b_r1, w_r2, b_r2,
                  w_e1, w_e3, w_e2, g_ple, w_ple_gate, w_ple, g_final):
    i = 0
    lam = (jnp.exp(jnp.sum(lam_q1[i] * lam_k1[i])) - jnp.exp(jnp.sum(lam_q2[i] * lam_k2[i])) + LAM_INIT)
    pad = LANES - N_EXPERTS - N_GROUPS
    w_router = jnp.concatenate([w_r2[i], w_r1[i], jnp.zeros((D_MODEL, pad), F32)], axis=1).astype(BF16)
    b_router = jnp.concatenate([b_r2[i], b_r1[i], jnp.zeros((pad,), F32)]).reshape(1, LANES)
    return dict(
        g_mix=g_mix[i], w_in=w_in[i].astype(BF16), lam=lam.astype(F32), subln_g=subln_g[i],
        ssm=_ssm_weights(ssm_lam_re[i], ssm_lam_im[i], ssm_log_dt[i], ssm_b_re[i], ssm_b_im[i],
                         ssm_c_re[i], ssm_c_im[i], ssm_d[i]),
        w_glu=w_glu[i].astype(BF16), b_glu=b_glu[i], w_attn_br=w_attn_br[i].astype(BF16),
        w_ssm_br=w_ssm_br[i].astype(BF16), w_out=w_out[i].astype(BF16), g_ffn=g_ffn[i],
        w_router=w_router, b_router=b_router,
        w13=jnp.concatenate([w_e1[i], w_e3[i]], axis=-1).astype(BF16), w2=w_e2[i].astype(BF16),
        g_ple=g_ple[i], w_ple_gate=w_ple_gate[i].astype(BF16), w_ple=w_ple[i].astype(BF16), g_final=g_final,
    )


def _trunk(x, p, w):
    bsz, seq, _ = x.shape
    n = bsz * seq
    x2 = x.reshape(n, D_MODEL)
    q, k, v, s_in, gl = _in_proj(x2, seq, w["g_mix"], w["w_in"], w["rope"])
    shp = (bsz, seq, ATT_WIDTH)
    o = _attention(q.reshape(shp), k.reshape(shp), v.reshape(shp), w["lam"], w["subln_g"])
    ys = _ssm(s_in, bsz, seq, w["ssm"])
    h = _mix(x2, o.reshape(n, ATT_WIDTH), ys, gl, w["w_glu"], w["b_glu"], w["w_attn_br"], w["w_ssm_br"], w["w_out"])
    h = _moe(h, w["g_ffn"], w["w_router"], w["b_router"], w["w13"], w["w2"])
    y = _ple(h, p[0].reshape(n, PLE_DIM), w["g_ple"], w["w_ple_gate"], w["w_ple"], w["g_final"])
    return y.reshape(bsz, seq, D_MODEL)


def kernel(x_prompt, x_sample, p_prompt, p_sample, g_mix, w_in, lam_q1, lam_k1, lam_q2, lam_k2, subln_g, ssm_lam_re, ssm_lam_im, ssm_log_dt, ssm_b_re, ssm_b_im, ssm_c_re, ssm_c_im, ssm_d, w_glu, b_glu, w_attn_br, w_ssm_br, w_out, g_ffn, w_r1, b_r1, w_r2, b_r2, w_e1, w_e3, w_e2, g_ple, w_ple_gate, w_ple, g_final):
    w = _prep_weights(g_mix, w_in, lam_q1, lam_k1, lam_q2, lam_k2, subln_g,
                      ssm_lam_re, ssm_lam_im, ssm_log_dt, ssm_b_re, ssm_b_im, ssm_c_re, ssm_c_im, ssm_d,
                      w_glu, b_glu, w_attn_br, w_ssm_br, w_out, g_ffn, w_r1, b_r1, w_r2, b_r2,
                      w_e1, w_e3, w_e2, g_ple, w_ple_gate, w_ple, g_final)
    w["rope"] = _rope_tables(max(x_prompt.shape[1], x_sample.shape[1]))
    return (_trunk(x_prompt, p_prompt, w), _trunk(x_sample, p_sample, w))
```

```python
import functools
import math

import jax
import jax.numpy as jnp
from jax import lax
from jax.experimental import pallas as pl
from jax.experimental.pallas import tpu as pltpu

F32 = jnp.float32
BF16 = jnp.bfloat16
HIGHEST = lax.Precision.HIGHEST

D_MODEL = 1024
N_HEADS = 8
HEAD_DIM = 64
ATT_V_DIM = 2 * HEAD_DIM
ATT_WIDTH = N_HEADS * ATT_V_DIM
ROT_DIM = HEAD_DIM // 4
ROPE_THETA = 500000.0
SSM_CH = 16
SSM_GROUPS = 32
SSM_WIDTH = SSM_GROUPS * SSM_CH
SSM_STATE = 64
N_GROUPS = 4
EXPERTS_PER_GROUP = 8
N_EXPERTS = N_GROUPS * EXPERTS_PER_GROUP
EXPERT_FF = 256
PLE_DIM = 256
EPS = 1e-6
LAM_INIT = 0.8 - 0.6 * math.exp(-0.3 * 0)

LANES = 128
SUBLANES = 8
SSM_CHUNK = 16
SSM_PAIR_LANES = 2 * SSM_STATE
SSM_ROWS_PER_STEP = 1024
SSM_PAIR_CH = 2 * SSM_CH
SSM_PIECES = 128 // SSM_PAIR_CH
VMEM_LIMIT = 56 * 1024 * 1024
NEG_BIG = -1e30
LOG2E = math.log2(math.e)
ATTN_ROW_BLOCK = 16
ATTN_OVERFLOW_MARGIN = 100.0
ATTN_SHORT_SEQ = 2048
ATTN_TILES_SHORT = (2048, 1024)
ATTN_TILES_LONG = (1024, 1024)
MOE_WINDOW = 320
MOE_EXPERTS_PER_STEP = 4


def _params(sem):
    return pltpu.CompilerParams(dimension_semantics=sem, vmem_limit_bytes=VMEM_LIMIT)


def _rms(x, g):
    return x * lax.rsqrt(jnp.mean(x * x, axis=-1, keepdims=True) + EPS) * g


def _sigmoid(x):
    return 1.0 / (1.0 + jnp.exp(-x))


def _pick_tile(n, pref):
    t = min(n, pref)
    while n % t:
        t //= 2
    return t


def _lane_piece(rows):
    return lax.shift_right_logical(lax.broadcasted_iota(jnp.int32, (rows, LANES), 1), SSM_PAIR_CH.bit_length() - 1)


def _to_chunk_major(tok_ref, out_ref, rows):
    T = SSM_CHUNK
    piece_id = _lane_piece(rows)
    for k in range(T // SSM_PIECES):
        xs = [[tok_ref[col, pl.ds(SSM_PIECES * k + m, rows, stride=T), :] for col in range(SSM_WIDTH // LANES)]
              for m in range(SSM_PIECES)]
        for gp in range(SSM_GROUPS // 2):
            col, src = divmod(gp, SSM_PIECES)
            dest = None
            for m in range(SSM_PIECES):
                piece = xs[m][col]
                shift = ((m - src) * SSM_PAIR_CH) % LANES
                if shift:
                    piece = pltpu.roll(piece, shift, 1)
                dest = piece if dest is None else jnp.where(piece_id == m, piece, dest)
            out_ref[gp, :, k * LANES:(k + 1) * LANES] = dest.astype(out_ref.dtype)


def _to_token_major(chunk_ref, tok_ref, rows):
    T = SSM_CHUNK
    piece_id = _lane_piece(rows)
    for t in range(T):
        col, src = divmod(t, SSM_PIECES)
        for k in range(SSM_WIDTH // LANES):
            dest = None
            for m in range(SSM_PIECES):
                piece = chunk_ref[SSM_PIECES * k + m, :, col * LANES:(col + 1) * LANES].astype(F32)
                shift = ((m - src) * SSM_PAIR_CH) % LANES
                if shift:
                    piece = pltpu.roll(piece, shift, 1)
                dest = piece if dest is None else jnp.where(piece_id == m, piece, dest)
            tok_ref[k, pl.ds(t, rows, stride=T), :] = dest


def _in_proj_kernel(x_ref, g_ref, w_ref, rope_ref, q_ref, k_ref, v_ref, s_ref, gl_ref, tok_ref):
    u = _rms(x_ref[...], g_ref[...]).astype(BF16)
    cos = rope_ref[:, 0:LANES]
    sa = rope_ref[:, LANES:2 * LANES]
    sb = rope_ref[:, 2 * LANES:3 * LANES]
    for sec, out_ref, scale in ((0, q_ref, HEAD_DIM ** -0.5 * LOG2E), (1, k_ref, 1.0)):
        y = jnp.dot(u, w_ref[:, sec * ATT_WIDTH:(sec + 1) * ATT_WIDTH], preferred_element_type=F32)
        for h in range(N_HEADS):
            yh = y[:, h * LANES:(h + 1) * LANES]
            r = yh * cos + pltpu.roll(yh, LANES - ROT_DIM // 2, 1) * sa + pltpu.roll(yh, ROT_DIM // 2, 1) * sb
            out_ref[h] = (r * scale).astype(BF16)
    c0 = 2 * ATT_WIDTH
    vv = jnp.dot(u, w_ref[:, c0:c0 + ATT_WIDTH], preferred_element_type=F32)
    for h in range(N_HEADS):
        v_ref[h] = vv[:, h * LANES:(h + 1) * LANES].astype(BF16)
    c0 += ATT_WIDTH
    s_tok = jnp.dot(u, w_ref[:, c0:c0 + SSM_WIDTH], preferred_element_type=F32)
    for col in range(SSM_WIDTH // LANES):
        tok_ref[col] = s_tok[:, col * LANES:(col + 1) * LANES]
    _to_chunk_major(tok_ref, s_ref, tok_ref.shape[1] // SSM_CHUNK)
    c0 += SSM_WIDTH
    gl_ref[...] = jnp.dot(u, w_ref[:, c0:c0 + 2 * D_MODEL], preferred_element_type=F32).astype(BF16)


def _rope_tables(seq):
    half = ROT_DIM // 2
    inv = ROPE_THETA ** (-jnp.arange(0, ROT_DIM, 2, dtype=F32) / ROT_DIM)
    ang = jnp.arange(seq, dtype=F32)[:, None] * inv[None, :]
    cos, sin = jnp.cos(ang), jnp.sin(ang)
    one = jnp.ones((seq, HEAD_DIM - ROT_DIM), F32)
    zero = jnp.zeros((seq, HEAD_DIM - ROT_DIM), F32)
    zh = jnp.zeros((seq, half), F32)
    cos_t = [cos, cos, one]
    sa_t = [-sin, zh, zero]
    sb_t = [zh, sin, zero]
    return jnp.concatenate(cos_t * 2 + sa_t * 2 + sb_t * 2, axis=1)


def _in_proj(x2, seq, g_mix, w_in, rope):
    n = x2.shape[0]
    tm = _pick_tile(seq, 512)
    nl = seq // tm
    in_cols = w_in.shape[1]
    row = lambda i: (i, 0)
    fixed = lambda i: (0, 0)
    tab = lambda i: (i % nl, 0)
    return pl.pallas_call(
        _in_proj_kernel,
        grid=(n // tm,),
        in_specs=[
            pl.BlockSpec((tm, D_MODEL), row),
            pl.BlockSpec((1, D_MODEL), fixed),
            pl.BlockSpec((D_MODEL, in_cols), fixed),
            pl.BlockSpec((tm, 3 * LANES), tab),
        ],
        out_specs=[
            pl.BlockSpec((N_HEADS, tm, LANES), lambda i: (0, i, 0)),
            pl.BlockSpec((N_HEADS, tm, LANES), lambda i: (0, i, 0)),
            pl.BlockSpec((N_HEADS, tm, LANES), lambda i: (0, i, 0)),
            pl.BlockSpec((SSM_GROUPS // 2, tm // SSM_CHUNK, SSM_CHUNK * SSM_PAIR_CH), lambda i: (0, i, 0)),
            pl.BlockSpec((tm, 2 * D_MODEL), row),
        ],
        out_shape=[
            jax.ShapeDtypeStruct((N_HEADS, n, LANES), BF16),
            jax.ShapeDtypeStruct((N_HEADS, n, LANES), BF16),
            jax.ShapeDtypeStruct((N_HEADS, n, LANES), BF16),
            jax.ShapeDtypeStruct((SSM_GROUPS // 2, n // SSM_CHUNK, SSM_CHUNK * SSM_PAIR_CH), BF16),
            jax.ShapeDtypeStruct((n, 2 * D_MODEL), BF16),
        ],
        scratch_shapes=[pltpu.VMEM((SSM_WIDTH // LANES, tm, LANES), F32)],
        compiler_params=_params(("parallel",)),
        name="in_proj",
    )(x2, g_mix.reshape(1, D_MODEL), w_in, rope)


def _col_reduce(x, pair_op, reduce_op):
    r = x.shape[0]
    rb = min(ATTN_ROW_BLOCK, r)
    out = None
    for i in range(r // rb):
        blk = x[i * rb:(i + 1) * rb]
        n = rb
        while n > SUBLANES:
            n //= 2
            blk = pair_op(blk[:n], blk[n:])
        out = blk if out is None else pair_op(out, blk)
    return reduce_op(out, axis=0, keepdims=True)


def _attn_kernel(lam_ref, q_ref, k_ref, v_ref, g_ref, o_ref, vt_ref, r_ref, l_ref, acc_ref, p_ref, be_ref, fl_ref,
                 *, tq, tk, nk):
    qi = pl.program_id(2)

    @pl.when(qi == 0)
    def _():
        def tr(j, c):
            off = pl.multiple_of(j * tk, tk)
            vt_ref[:, pl.ds(off, tk)] = v_ref[0, 0, pl.ds(off, tk), :].astype(F32).T.astype(BF16)
            return c
        lax.fori_loop(0, nk, tr, 0)

    qt = q_ref[0, 0].astype(F32).T
    row = lax.broadcasted_iota(jnp.int32, qt.shape, 0)
    qm = jnp.concatenate([jnp.where(row < HEAD_DIM, qt, 0.0),
                          jnp.where(row >= HEAD_DIM, qt, 0.0)], axis=1).astype(BF16)

    def scores(j):
        off = pl.multiple_of(j * tk, tk)
        return jnp.dot(k_ref[0, 0, pl.ds(off, tk), :], qm, preferred_element_type=F32)

    def values(j, slot):
        off = pl.multiple_of(j * tk, tk)
        pv = jnp.dot(vt_ref[:, pl.ds(off, tk)], p_ref[slot], preferred_element_type=F32)
        acc_ref[...] = be_ref[slot] * (acc_ref[...] + pv)

    rb = min(ATTN_ROW_BLOCK, tk)
    head = jnp.dot(k_ref[0, 0, 0:rb, :], qm, preferred_element_type=F32)
    r_ref[...] = _col_reduce(head, jnp.maximum, jnp.max)
    l_ref[...] = jnp.zeros(l_ref.shape, F32)
    fl_ref[...] = jnp.zeros(fl_ref.shape, F32)
    acc_ref[...] = jnp.zeros(acc_ref.shape, F32)

    def step(j, slot, with_values=True):
        s = scores(j)
        ref = r_ref[...]
        p = jnp.exp2(s - ref)
        cm = _col_reduce(s, jnp.maximum, jnp.max)
        new = jnp.maximum(ref, cm)
        beta = jnp.exp2(ref - new)
        fl_ref[...] = jnp.maximum(fl_ref[...], cm - ref)
        l_ref[...] = beta * (l_ref[...] + _col_reduce(p, jnp.add, jnp.sum))
        r_ref[...] = new
        be_ref[slot] = beta
        p_ref[slot] = p.astype(BF16)
        if with_values:
            values(j - 1, 1 - slot)

    step(0, 0, with_values=False)
    if nk > 1:
        def body(i, c):
            step(2 * i + 1, 1)
            step(2 * i + 2, 0)
            return c
        lax.fori_loop(0, (nk - 2) // 2, body, 0)
        step(nk - 1, 1)
    values(nk - 1, (nk - 1) % 2)

    @pl.when(jnp.max(fl_ref[...]) > ATTN_OVERFLOW_MARGIN)
    def _():
        r_ref[...] = jnp.full(r_ref.shape, NEG_BIG, F32)
        l_ref[...] = jnp.zeros(l_ref.shape, F32)
        acc_ref[...] = jnp.zeros(acc_ref.shape, F32)

        def safe(j, c):
            s = scores(j)
            m_old = r_ref[...]
            m_new = jnp.maximum(m_old, _col_reduce(s, jnp.maximum, jnp.max))
            alpha = jnp.exp2(m_old - m_new)
            p = jnp.exp2(s - m_new)
            l_ref[...] = alpha * l_ref[...] + _col_reduce(p, jnp.add, jnp.sum)
            r_ref[...] = m_new
            off = pl.multiple_of(j * tk, tk)
            pv = jnp.dot(vt_ref[:, pl.ds(off, tk)], p.astype(BF16), preferred_element_type=F32)
            acc_ref[...] = alpha * acc_ref[...] + pv
            return c
        lax.fori_loop(0, nk, safe, 0)

    lam = lam_ref[0]
    o_all = acc_ref[...] * (1.0 / l_ref[...])
    o = o_all[:, :tq] - lam * o_all[:, tq:]
    o = o * lax.rsqrt(jnp.mean(o * o, axis=0, keepdims=True) + EPS) * g_ref[...]
    o_ref[0, 0] = (o * (1.0 - LAM_INIT)).T.astype(BF16)


def _attention(q, k, v, lam, subln_g):
    _, bsz, seq, _ = q.shape
    tq_pref, tk_pref = ATTN_TILES_SHORT if seq <= ATTN_SHORT_SEQ else ATTN_TILES_LONG
    tq = _pick_tile(seq, tq_pref)
    tk = _pick_tile(seq, tk_pref)
    nk = seq // tk
    assert nk == 1 or nk % 2 == 0
    kern = functools.partial(_attn_kernel, tq=tq, tk=tk, nk=nk)
    return pl.pallas_call(
        kern,
        grid=(bsz, N_HEADS, seq // tq),
        in_specs=[
            pl.BlockSpec(memory_space=pltpu.SMEM),
            pl.BlockSpec((1, 1, tq, LANES), lambda b, h, i: (h, b, i, 0)),
            pl.BlockSpec((1, 1, seq, LANES), lambda b, h, i: (h, b, 0, 0)),
            pl.BlockSpec((1, 1, seq, LANES), lambda b, h, i: (h, b, 0, 0)),
            pl.BlockSpec((ATT_V_DIM, 1), lambda b, h, i: (0, 0)),
        ],
        out_specs=pl.BlockSpec((1, 1, tq, LANES), lambda b, h, i: (h, b, i, 0)),
        out_shape=jax.ShapeDtypeStruct((N_HEADS, bsz, seq, LANES), BF16),
        scratch_shapes=[
            pltpu.VMEM((ATT_V_DIM, seq), BF16),
            pltpu.VMEM((1, 2 * tq), F32),
            pltpu.VMEM((1, 2 * tq), F32),
            pltpu.VMEM((ATT_V_DIM, 2 * tq), F32),
            pltpu.VMEM((2, tk, 2 * tq), BF16),
            pltpu.VMEM((2, 1, 2 * tq), F32),
            pltpu.VMEM((1, 2 * tq), F32),
        ],
        compiler_params=_params(("parallel", "parallel", "arbitrary")),
        name="diff_attn",
    )(lam.reshape(1), q, k, v, subln_g.reshape(ATT_V_DIM, 1))


def _ssm_weights(lam_re, lam_im, log_dt, b_re, b_im, c_re, c_im, d_skip):
    T, G, P, CH = SSM_CHUNK, SSM_GROUPS, SSM_STATE, SSM_CH
    n = jnp.arange(T + 1, dtype=F32)[:, None, None]
    pw_re, pw_im, bb_re, bb_im = [], [], [], []
    for dirn in range(2):
        dt = jnp.exp(log_dt[dirn])[:, None]
        lr, li = lam_re[dirn], lam_im[dirn]
        mag = jnp.exp(lr * dt)
        a_re, a_im = mag * jnp.cos(li * dt), mag * jnp.sin(li * dt)
        den = lr * lr + li * li
        n_re = a_re - 1.0
        q_re = (n_re * lr + a_im * li) / den
        q_im = (a_im * lr - n_re * li) / den
        bb_re.append(q_re[..., None] * b_re[dirn] - q_im[..., None] * b_im[dirn])
        bb_im.append(q_re[..., None] * b_im[dirn] + q_im[..., None] * b_re[dirn])
        magn = jnp.exp(n * (lr * dt)[None])
        pw_re.append(magn * jnp.cos(n * (li * dt)[None]))
        pw_im.append(magn * jnp.sin(n * (li * dt)[None]))

    def cmul(ar, ai, br, bi):
        return ar * br - ai * bi, ar * bi + ai * br

    kern = []
    for dirn in range(2):
        car, cai = cmul(c_re[dirn][None], c_im[dirn][None],
                        pw_re[dirn][:T, :, None, :], pw_im[dirn][:T, :, None, :])
        kern.append(jnp.einsum('dgcp,gpk->dgck', car, bb_re[dirn], precision=HIGHEST)
                    - jnp.einsum('dgcp,gpk->dgck', cai, bb_im[dirn], precision=HIGHEST))
    tj = jnp.arange(T)
    lag = tj[None, :] - tj[:, None]
    k0 = kern[0][0] + kern[1][0] + jnp.eye(CH, dtype=F32)[None] * d_skip[:, :, None]
    k_lag = jnp.concatenate([kern[1][:0:-1], k0[None], kern[0][1:]], axis=0).astype(BF16)
    eye2 = jnp.eye(2, dtype=BF16)
    k_strip = k_lag.reshape(2 * T - 1, G // 2, 2, CH, CH).transpose(1, 2, 4, 0, 3)
    k_strip = k_strip[:, :, :, :, None, :] * eye2[None, :, None, None, :, None]
    k_strip = k_strip.reshape(G // 2, 2 * CH, (2 * T - 1) * 2 * CH).astype(F32)
    k_strip = jnp.pad(k_strip, ((0, 0), (0, 0), (0, 2 * CH)))

    def w_dir(dirn, steps):
        pr = pw_re[dirn][steps].transpose(1, 0, 2)[:, :, None, :]
        pi = pw_im[dirn][steps].transpose(1, 0, 2)[:, :, None, :]
        return cmul(pr, pi, bb_re[dirn].transpose(0, 2, 1)[:, None], bb_im[dirn].transpose(0, 2, 1)[:, None])
    wf_re, wf_im = w_dir(0, T - 1 - tj)
    wb_re, wb_im = w_dir(1, tj)
    w_all = jnp.stack([wf_re, wf_im, wb_re, wb_im], axis=3).astype(BF16)
    w_all = w_all.reshape(G // 2, 2, T, CH, 4, P).transpose(0, 2, 1, 3, 4, 5)
    w_pair = (w_all[:, :, :, :, :, None, :] * eye2[None, None, :, None, None, :, None]).reshape(G // 2, T * 2 * CH, 4 * 2 * P)

    def z_dir(dirn, steps):
        pr = pw_re[dirn][steps].transpose(1, 2, 0)[:, :, :, None]
        pi = pw_im[dirn][steps].transpose(1, 2, 0)[:, :, :, None]
        return cmul(c_re[dirn].transpose(0, 2, 1)[:, :, None, :], c_im[dirn].transpose(0, 2, 1)[:, :, None, :], pr, pi)
    zf_re, zf_im = z_dir(0, 1 + tj)
    zb_re, zb_im = z_dir(1, T - tj)
    z_all = jnp.stack([zf_re, -zf_im, zb_re, -zb_im], axis=1).astype(BF16)
    z_all = z_all.reshape(G // 2, 2, 4, P, T, CH)
    z_pair = (z_all.transpose(0, 2, 1, 3, 4, 5)[:, :, :, :, :, None, :] * eye2[None, None, :, None, None, :, None])
    z_pair = z_pair.reshape(G // 2, 4 * 2 * P, T * 2 * CH)

    a_tab = []
    for dirn in range(2):
        dt = jnp.exp(log_dt[dirn])[:, None]
        order = jnp.arange(SUBLANES) if dirn == 0 else SUBLANES - 1 - jnp.arange(SUBLANES)
        nn = jnp.concatenate([order, jnp.array([1, 2, 4, 8])]).astype(F32)[:, None, None] * T
        magn = jnp.exp(nn * (lam_re[dirn] * dt)[None])
        ang = nn * (lam_im[dirn] * dt)[None]
        tab = jnp.stack([magn * jnp.cos(ang), magn * jnp.sin(ang)], axis=0)
        tab = jnp.pad(tab, ((0, 0), (0, 2 * SUBLANES - tab.shape[1]), (0, 0), (0, 0)))
        a_tab.append(tab.reshape(2, 2 * SUBLANES, G // 2, 2 * P).transpose(2, 0, 1, 3))
    a_tab = jnp.stack(a_tab, axis=1)
    return k_strip, w_pair, z_pair, a_tab


def _tile_scan(x_re, x_im, a_tab, reverse):
    row = lax.broadcasted_iota(jnp.int32, x_re.shape, 0)
    for lvl, d in enumerate((1, 2, 4)):
        ar = a_tab[0][SUBLANES + lvl:SUBLANES + lvl + 1, :]
        ai = a_tab[1][SUBLANES + lvl:SUBLANES + lvl + 1, :]
        if reverse:
            keep = row < SUBLANES - d
            shift = SUBLANES - d
        else:
            keep = row >= d
            shift = d
        sr = jnp.where(keep, pltpu.roll(x_re, shift, 0), 0.0)
        si = jnp.where(keep, pltpu.roll(x_im, shift, 0), 0.0)
        x_re, x_im = x_re + ar * sr - ai * si, x_im + ar * si + ai * sr
    return x_re, x_im


def _ssm_kernel(u_ref, w_ref, k_ref, z_ref, a_ref, y_ref, v_ref, s_ref, *, nc, nb):
    pl_ = SSM_PAIR_LANES
    nt = nc // SUBLANES
    u = u_ref[0]
    v_ref[...] = jnp.dot(u, w_ref[0], preferred_element_type=F32)

    row = lax.broadcasted_iota(jnp.int32, (SUBLANES, pl_), 0)

    def tile_step(i, carry, base):
        new = []
        for dirn in range(2):
            c_re, c_im = carry[2 * dirn], carry[2 * dirn + 1]
            a_tab = (a_ref[0, dirn, 0], a_ref[0, dirn, 1])
            t = i if dirn == 0 else nt - 1 - i
            r0 = pl.multiple_of(base + t * SUBLANES, SUBLANES)
            lo = 2 * dirn * pl_
            x_re, x_im = _tile_scan(v_ref[pl.ds(r0, SUBLANES), lo:lo + pl_],
                                    v_ref[pl.ds(r0, SUBLANES), lo + pl_:lo + 2 * pl_], a_tab, dirn == 1)
            if dirn == 0:
                keep, shift, last = row >= 1, 1, SUBLANES - 1
            else:
                keep, shift, last = row < SUBLANES - 1, SUBLANES - 1, 0
            pr, pi = a_tab[0][0:SUBLANES, :], a_tab[1][0:SUBLANES, :]
            s_ref[pl.ds(r0, SUBLANES), lo:lo + pl_] = (
                pr * c_re - pi * c_im + jnp.where(keep, pltpu.roll(x_re, shift, 0), 0.0))
            s_ref[pl.ds(r0, SUBLANES), lo + pl_:lo + 2 * pl_] = (
                pr * c_im + pi * c_re + jnp.where(keep, pltpu.roll(x_im, shift, 0), 0.0))
            a8r = a_tab[0][SUBLANES + 3:SUBLANES + 4, :]
            a8i = a_tab[1][SUBLANES + 3:SUBLANES + 4, :]
            new.append(a8r * c_re - a8i * c_im + x_re[last:last + 1, :])
            new.append(a8r * c_im + a8i * c_re + x_im[last:last + 1, :])
        return tuple(new)

    zero = jnp.zeros((1, pl_), F32)

    def sequence(bi, c):
        lax.fori_loop(0, nt, functools.partial(tile_step, base=bi * nc), (zero, zero, zero, zero))
        return c
    lax.fori_loop(0, nb, sequence, 0)

    strip = k_ref[0]
    tw = SSM_CHUNK * SSM_PAIR_CH
    m_pair = jnp.concatenate([strip[:, (SSM_CHUNK - 1 - j) * SSM_PAIR_CH:(SSM_CHUNK - 1 - j) * SSM_PAIR_CH + tw]
                              for j in range(SSM_CHUNK)], axis=0).astype(BF16)
    ys = (jnp.dot(s_ref[...].astype(BF16), z_ref[0], preferred_element_type=F32)
          + jnp.dot(u, m_pair, preferred_element_type=F32))
    y_ref[0] = jax.nn.gelu(ys).astype(BF16)


def _ssm(ug, bsz, seq, weights):
    k_strip, w_pair, z_pair, a_tab = weights
    nc = seq // SSM_CHUNK
    tw = SSM_CHUNK * SSM_PAIR_CH
    nb = max(d for d in range(1, bsz + 1) if bsz % d == 0 and d * nc <= max(nc, SSM_ROWS_PER_STEP))
    kern = functools.partial(_ssm_kernel, nc=nc, nb=nb)
    return pl.pallas_call(
        kern,
        grid=(SSM_GROUPS // 2, bsz // nb),
        in_specs=[
            pl.BlockSpec((1, nb * nc, tw), lambda g, b: (g, b, 0)),
            pl.BlockSpec((1, tw, 4 * SSM_PAIR_LANES), lambda g, b: (g, 0, 0)),
            pl.BlockSpec((1, SSM_PAIR_CH, 2 * tw), lambda g, b: (g, 0, 0)),
            pl.BlockSpec((1, 4 * SSM_PAIR_LANES, tw), lambda g, b: (g, 0, 0)),
            pl.BlockSpec((1, 2, 2, 2 * SUBLANES, SSM_PAIR_LANES), lambda g, b: (g, 0, 0, 0, 0)),
        ],
        out_specs=pl.BlockSpec((1, nb * nc, tw), lambda g, b: (g, b, 0)),
        out_shape=jax.ShapeDtypeStruct(ug.shape, BF16),
        scratch_shapes=[
            pltpu.VMEM((nb * nc, 4 * SSM_PAIR_LANES), F32),
            pltpu.VMEM((nb * nc, 4 * SSM_PAIR_LANES), F32),
        ],
        compiler_params=_params(("parallel", "arbitrary")),
        name="s5_scan",
    )(ug, w_pair, k_strip, z_pair, a_tab)


def _mix_kernel(h_ref, o_ref, yg_ref, gl_ref, wglu_ref, bglu_ref, wa_ref, ws_ref, wo_ref, out_ref, tok_ref):
    _to_token_major(yg_ref, tok_ref, tok_ref.shape[1] // SSM_CHUNK)
    yg = jnp.concatenate([tok_ref[col] for col in range(SSM_WIDTH // LANES)], axis=1)
    z = jnp.dot(yg.astype(BF16), wglu_ref[...], preferred_element_type=F32) + bglu_ref[...]
    ysg = (yg * _sigmoid(z)).astype(BF16)
    o = jnp.concatenate([o_ref[h] for h in range(N_HEADS)], axis=1)
    br_a = jnp.dot(o, wa_ref[...], preferred_element_type=F32)
    br_s = jnp.dot(ysg, ws_ref[...], preferred_element_type=F32)
    gl = gl_ref[...].astype(F32)
    mix = _sigmoid(gl[:, :D_MODEL]) * br_a + _sigmoid(gl[:, D_MODEL:]) * br_s
    out_ref[...] = h_ref[...] + jnp.dot(mix.astype(BF16), wo_ref[...], preferred_element_type=F32)


def _mix(h, o, ys, gl, w_glu, b_glu, w_attn_br, w_ssm_br, w_out):
    n = h.shape[0]
    tm = _pick_tile(n, 512)
    row = lambda i: (i, 0)
    fixed = lambda i: (0, 0)
    return pl.pallas_call(
        _mix_kernel,
        grid=(n // tm,),
        in_specs=[
            pl.BlockSpec((tm, D_MODEL), row),
            pl.BlockSpec((N_HEADS, tm, LANES), lambda i: (0, i, 0)),
            pl.BlockSpec((SSM_GROUPS // 2, tm // SSM_CHUNK, SSM_CHUNK * SSM_PAIR_CH), lambda i: (0, i, 0)),
            pl.BlockSpec((tm, 2 * D_MODEL), row),
            pl.BlockSpec((SSM_WIDTH, SSM_WIDTH), fixed),
            pl.BlockSpec((1, SSM_WIDTH), fixed),
            pl.BlockSpec((ATT_WIDTH, D_MODEL), fixed),
            pl.BlockSpec((SSM_WIDTH, D_MODEL), fixed),
            pl.BlockSpec((D_MODEL, D_MODEL), fixed),
        ],
        out_specs=pl.BlockSpec((tm, D_MODEL), row),
        out_shape=jax.ShapeDtypeStruct((n, D_MODEL), F32),
        scratch_shapes=[pltpu.VMEM((SSM_WIDTH // LANES, tm, LANES), F32)],
        compiler_params=_params(("parallel",)),
        name="branch_mix",
    )(h, o, ys, gl, w_glu, b_glu.reshape(1, SSM_WIDTH), w_attn_br, w_ssm_br, w_out)


def _lane_max(x):
    return jnp.max(x, axis=-1, keepdims=True)


def _router(u, wr_ref, br_ref):
    logits = jnp.dot(u, wr_ref[...], preferred_element_type=F32) + br_ref[...]
    lane = lax.broadcasted_iota(jnp.int32, logits.shape, 1).astype(F32)
    far = float(4 * LANES)
    is_grp = (lane >= N_EXPERTS) & (lane < N_EXPERTS + N_GROUPS)
    lg = jnp.where(is_grp, logits, NEG_BIG)
    g_max = _lane_max(lg)
    p_sel = 1.0 / jnp.sum(jnp.where(is_grp, jnp.exp(lg - g_max), 0.0), axis=-1, keepdims=True)
    g_sel = jnp.min(jnp.where(lg == g_max, lane, far), axis=-1, keepdims=True) - N_EXPERTS
    g_lo = g_sel * EXPERTS_PER_GROUP
    in_grp = (lane >= g_lo) & (lane < g_lo + EXPERTS_PER_GROUP)
    l2 = jnp.where(in_grp, logits, NEG_BIG)
    m1 = _lane_max(l2)
    i1 = jnp.min(jnp.where(l2 == m1, lane, far), axis=-1, keepdims=True)
    l2b = jnp.where(lane == i1, NEG_BIG, l2)
    m2 = _lane_max(l2b)
    i2 = jnp.min(jnp.where(l2b == m2, lane, far), axis=-1, keepdims=True)
    e2 = jnp.exp(m2 - m1)
    w1 = 1.0 / (1.0 + e2)
    w2 = e2 * w1
    gates = p_sel * (jnp.where(lane == i1, w1, 0.0) + jnp.where(lane == i2, w2, 0.0))
    return gates, jnp.where(lane == g_sel, 1.0, 0.0)


def _moe_kernel(h_ref, g_ref, wr_ref, br_ref, tri_ref, w13_ref, w2_ref, out_ref,
                ts_ref, gs_ref, acc_ref, pt_ref, meta_ref, *, tm, win, eps):
    step = pl.program_id(1)

    @pl.when(step == 0)
    def _():
        u = _rms(h_ref[...], g_ref[...]).astype(BF16)
        gates, onehot = _router(u, wr_ref, br_ref)
        rank = jnp.dot(tri_ref[...], onehot.astype(BF16), preferred_element_type=F32)
        cnt = jnp.sum(onehot, axis=0, keepdims=True)
        off = pltpu.roll(cnt, 1, 1) + pltpu.roll(cnt, 2, 1) + pltpu.roll(cnt, 3, 1)
        pos = jnp.sum(onehot * (off + rank), axis=-1, keepdims=True)
        slot_lane = lax.broadcasted_iota(jnp.int32, (tm, tm), 1).astype(F32)
        pt_ref[...] = jnp.where(slot_lane == pos, 1.0, 0.0).astype(BF16)
        pos_row = jnp.broadcast_to(pos, (tm, LANES)).T[0:1, :]
        slot_sub = lax.broadcasted_iota(jnp.int32, (tm, tm), 0).astype(F32)
        perm = jnp.where(slot_sub == pos_row, 1.0, 0.0).astype(BF16)
        ts_ref[0:tm, :] = jnp.dot(perm, u, preferred_element_type=F32).astype(BF16)
        g_hi = gates.astype(BF16)
        g_lo = (gates - g_hi.astype(F32)).astype(BF16)
        gs_ref[0:tm, :] = (jnp.dot(perm, g_hi, preferred_element_type=F32)
                           + jnp.dot(perm, g_lo, preferred_element_type=F32))
        ts_ref[tm:tm + win, :] = jnp.zeros((win, D_MODEL), BF16)
        gs_ref[tm:tm + win, :] = jnp.zeros((win, LANES), F32)
        acc_ref[...] = jnp.zeros(acc_ref.shape, F32)
        off_i = off.astype(jnp.int32)
        end_i = (off + cnt).astype(jnp.int32)
        for g in range(N_GROUPS):
            meta_ref[0, g] = off_i[0, g]
            meta_ref[1, g] = end_i[0, g]

    e0 = step * eps
    grp = lax.shift_right_logical(e0, 3)
    start = lax.shift_left(lax.shift_right_logical(meta_ref[0, grp], 4), 4)
    span = meta_ref[1, grp] - start
    nwin = 1 + sum((span > k * win).astype(jnp.int32) for k in range(1, tm // win + 2))

    def window(i, c):
        r0 = pl.multiple_of(start + i * win, 16)
        x = ts_ref[pl.ds(r0, win), :]
        gsl = gs_ref[pl.ds(r0, win), :]
        lane = lax.broadcasted_iota(jnp.int32, gsl.shape, 1)
        y = None
        for k in range(eps):
            ab = jnp.dot(x, w13_ref[k], preferred_element_type=F32)
            a = ab[:, :EXPERT_FF]
            hdn = a * _sigmoid(a) * ab[:, EXPERT_FF:]
            ge = jnp.sum(jnp.where(lane == e0 + k, gsl, 0.0), axis=-1, keepdims=True)
            yk = jnp.dot((hdn * ge).astype(BF16), w2_ref[k], preferred_element_type=F32)
            y = yk if y is None else y + yk
        acc_ref[pl.ds(r0, win), :] += y
        return c
    lax.fori_loop(0, nwin, window, 0)

    @pl.when(step == pl.num_programs(1) - 1)
    def _():
        out_ref[...] = h_ref[...] + jnp.dot(pt_ref[...], acc_ref[0:tm, :].astype(BF16), preferred_element_type=F32)


def _moe(h, g_ffn, w_router, b_router, w13, w2, win=MOE_WINDOW, eps=MOE_EXPERTS_PER_STEP):
    n = h.shape[0]
    tm = _pick_tile(n, 1024)
    tri = jnp.tri(tm, tm, -1, dtype=BF16)
    row = lambda i, e: (i, 0)
    fixed = lambda i, e: (0, 0)
    kern = functools.partial(_moe_kernel, tm=tm, win=win, eps=eps)
    return pl.pallas_call(
        kern,
        grid=(n // tm, N_EXPERTS // eps),
        in_specs=[
            pl.BlockSpec((tm, D_MODEL), row),
            pl.BlockSpec((1, D_MODEL), fixed),
            pl.BlockSpec((D_MODEL, LANES), fixed),
            pl.BlockSpec((1, LANES), fixed),
            pl.BlockSpec((tm, tm), fixed),
            pl.BlockSpec((eps, D_MODEL, 2 * EXPERT_FF), lambda i, e: (e, 0, 0)),
            pl.BlockSpec((eps, EXPERT_FF, D_MODEL), lambda i, e: (e, 0, 0)),
        ],
        out_specs=pl.BlockSpec((tm, D_MODEL), row),
        out_shape=jax.ShapeDtypeStruct((n, D_MODEL), F32),
        scratch_shapes=[
            pltpu.VMEM((tm + win, D_MODEL), BF16),
            pltpu.VMEM((tm + win, LANES), F32),
            pltpu.VMEM((tm + win, D_MODEL), F32),
            pltpu.VMEM((tm, tm), BF16),
            pltpu.SMEM((2, N_GROUPS), jnp.int32),
        ],
        compiler_params=_params(("parallel", "arbitrary")),
        name="hier_moe",
    )(h, g_ffn.reshape(1, D_MODEL), w_router, b_router, tri, w13, w2)


def _ple_kernel(h_ref, p_ref, gp_ref, wg_ref, wp_ref, gf_ref, out_ref):
    h = h_ref[...]
    gate = _sigmoid(jnp.dot(_rms(h, gp_ref[...]).astype(BF16), wg_ref[...], preferred_element_type=F32))
    pe = jnp.dot(p_ref[...].astype(BF16), wp_ref[...], preferred_element_type=F32)
    out_ref[...] = _rms(h + gate * pe, gf_ref[...])


def _ple(h, p, g_ple, w_ple_gate, w_ple, g_final):
    n = h.shape[0]
    tm = _pick_tile(n, 512)
    row = lambda i: (i, 0)
    fixed = lambda i: (0, 0)
    return pl.pallas_call(
        _ple_kernel,
        grid=(n // tm,),
        in_specs=[
            pl.BlockSpec((tm, D_MODEL), row),
            pl.BlockSpec((tm, PLE_DIM), row),
            pl.BlockSpec((1, D_MODEL), fixed),
            pl.BlockSpec((D_MODEL, D_MODEL), fixed),
            pl.BlockSpec((PLE_DIM, D_MODEL), fixed),
            pl.BlockSpec((1, D_MODEL), fixed),
        ],
        out_specs=pl.BlockSpec((tm, D_MODEL), row),
        out_shape=jax.ShapeDtypeStruct((n, D_MODEL), F32),
        compiler_params=_params(("parallel",)),
        name="ple_final",
    )(h, p, g_ple.reshape(1, D_MODEL), w_ple_gate, w_ple, g_final.reshape(1, D_MODEL))


def _prep_weights(g_mix, w_in, lam_q1, lam_k1, lam_q2, lam_k2, subln_g,
                  ssm_lam_re, ssm_lam_im, ssm_log_dt, ssm_b_re, ssm_b_im, ssm_c_re, ssm_c_im, ssm_d,
                  w_glu, b_glu, w_attn_br, w_ssm_br, w_out, g_ffn, w_r1, b_r1, w_r2, b_r2,
                  w_e1, w_e3, w_e2, g_ple, w_ple_gate, w_ple, g_final):
    i = 0
    lam = (jnp.exp(jnp.sum(lam_q1[i] * lam_k1[i])) - jnp.exp(jnp.sum(lam_q2[i] * lam_k2[i])) + LAM_INIT)
    pad = LANES - N_EXPERTS - N_GROUPS
    w_router = jnp.concatenate([w_r2[i], w_r1[i], jnp.zeros((D_MODEL, pad), F32)], axis=1).astype(BF16)
    b_router = jnp.concatenate([b_r2[i], b_r1[i], jnp.zeros((pad,), F32)]).reshape(1, LANES)
    return dict(
        g_mix=g_mix[i], w_in=w_in[i].astype(BF16), lam=lam.astype(F32), subln_g=subln_g[i],
        ssm=_ssm_weights(ssm_lam_re[i], ssm_lam_im[i], ssm_log_dt[i], ssm_b_re[i], ssm_b_im[i],
                         ssm_c_re[i], ssm_c_im[i], ssm_d[i]),
        w_glu=w_glu[i].astype(BF16), b_glu=b_glu[i], w_attn_br=w_attn_br[i].astype(BF16),
        w_ssm_br=w_ssm_br[i].astype(BF16), w_out=w_out[i].astype(BF16), g_ffn=g_ffn[i],
        w_router=w_router, b_router=b_router,
        w13=jnp.concatenate([w_e1[i], w_e3[i]], axis=-1).astype(BF16), w2=w_e2[i].astype(BF16),
        g_ple=g_ple[i], w_ple_gate=w_ple_gate[i].astype(BF16), w_ple=w_ple[i].astype(BF16), g_final=g_final,
    )


def _trunk(x, p, w):
    bsz, seq, _ = x.shape
    n = bsz * seq
    x2 = x.reshape(n, D_MODEL)
    q, k, v, s_in, gl = _in_proj(x2, seq, w["g_mix"], w["w_in"], w["rope"])
    shp = (N_HEADS, bsz, seq, LANES)
    o = _attention(q.reshape(shp), k.reshape(shp), v.reshape(shp), w["lam"], w["subln_g"])
    ys = _ssm(s_in, bsz, seq, w["ssm"])
    h = _mix(x2, o.reshape(N_HEADS, n, LANES), ys, gl, w["w_glu"], w["b_glu"], w["w_attn_br"], w["w_ssm_br"], w["w_out"])
    h = _moe(h, w["g_ffn"], w["w_router"], w["b_router"], w["w13"], w["w2"])
    y = _ple(h, p[0].reshape(n, PLE_DIM), w["g_ple"], w["w_ple_gate"], w["w_ple"], w["g_final"])
    return y.reshape(bsz, seq, D_MODEL)


def kernel(x_prompt, x_sample, p_prompt, p_sample, g_mix, w_in, lam_q1, lam_k1, lam_q2, lam_k2, subln_g, ssm_lam_re, ssm_lam_im, ssm_log_dt, ssm_b_re, ssm_b_im, ssm_c_re, ssm_c_im, ssm_d, w_glu, b_glu, w_attn_br, w_ssm_br, w_out, g_ffn, w_r1, b_r1, w_r2, b_r2, w_e1, w_e3, w_e2, g_ple, w_ple_gate, w_ple, g_final):
    w = _prep_weights(g_mix, w_in, lam_q1, lam_k1, lam_q2, lam_k2, subln_g,
                      ssm_lam_re, ssm_lam_im, ssm_log_dt, ssm_b_re, ssm_b_im, ssm_c_re, ssm_c_im, ssm_d,
                      w_glu, b_glu, w_attn_br, w_ssm_br, w_out, g_ffn, w_r1, b_r1, w_r2, b_r2,
                      w_e1, w_e3, w_e2, g_ple, w_ple_gate, w_ple, g_final)
    w["rope"] = _rope_tables(max(x_prompt.shape[1], x_sample.shape[1]))
    return (_trunk(x_prompt, p_prompt, w), _trunk(x_sample, p_sample, w))
```
